```python
import math
import jax
import jax.numpy as jnp
from jax import lax
import numpy as np

D_MODEL = 2048
BATCH = 4
SEQ = 2048
DEPTH = 4
DEC_BATCH = 32
DEC_SEQ = 8
PAST_LEN = 16384
PAGE_SIZE = 128

HEAD_DIM = 64
A_DIM = D_MODEL // 4
B_DIM = D_MODEL // 4
C_DIM = D_MODEL // 2
A_HEADS = A_DIM // HEAD_DIM
B_HEADS = B_DIM // HEAD_DIM
C_Q_HEADS = C_DIM // HEAD_DIM
C_KV_HEADS = 2
C_GROUP = C_Q_HEADS // C_KV_HEADS
MIX_DIM = A_DIM + B_DIM + C_DIM
C_KV_DIM = C_KV_HEADS * HEAD_DIM
A_BRANCHES = ((128, 1), (512, 4), (2048, 16))
A_MAX_WINDOW = 2048
C_WINDOW = 128
BLOCK = 128
ROPE_THETA = 500000.0
ROPE_DIM = HEAD_DIM // 4
B_DECAY_LORA = 96
B_AAA_LORA = 96
B_GATE_LORA = 64
B_COLS = 3 * B_DIM + B_DECAY_LORA + B_AAA_LORA + B_GATE_LORA
IN_COLS = 3 * A_DIM + B_COLS + C_DIM + 2 * C_KV_DIM
IN_SPLITS = (A_DIM, 2 * A_DIM, 3 * A_DIM, 3 * A_DIM + B_COLS, 3 * A_DIM + B_COLS + C_DIM, 3 * A_DIM + B_COLS + C_DIM + C_KV_DIM)
B_SPLITS = (B_DIM, 2 * B_DIM, 3 * B_DIM, 3 * B_DIM + B_DECAY_LORA, 3 * B_DIM + B_DECAY_LORA + B_AAA_LORA)
D_FF = -(-8 * D_MODEL // (3 * 256)) * 256
RMS_EPS = 1e-6
GN_EPS = 64e-5
ATTN_SCALE = HEAD_DIM ** -0.5

kernel_name = 'hybrid_dilated_rwkv7_swa_decoder_step'


def rms_norm(x, g):
    xf = x.astype(jnp.float32)
    y = xf * lax.rsqrt(jnp.mean(xf * xf, axis=-1, keepdims=True) + RMS_EPS)
    return (y * g.astype(jnp.float32)).astype(x.dtype)


def rope_partial(x, pos):
    half = ROPE_DIM // 2
    inv = jnp.exp(-math.log(ROPE_THETA) * jnp.arange(half, dtype=jnp.float32) * 2.0 / ROPE_DIM)
    ang = pos.astype(jnp.float32)[:, None] * inv[None, :]
    shp = (ang.shape[0],) + (1,) * (x.ndim - 3) + (half,)
    cos = jnp.cos(ang).reshape(shp).astype(x.dtype)
    sin = jnp.sin(ang).reshape(shp).astype(x.dtype)
    x1 = x[..., :half]
    x2 = x[..., half:ROPE_DIM]
    return jnp.concatenate([x1 * cos - x2 * sin, x2 * cos + x1 * sin, x[..., ROPE_DIM:]], axis=-1)


def band_attention(q, k, v, window):
    B, L, Hk, G, hd = q.shape
    Lp = -(-L // BLOCK) * BLOCK
    pad = Lp - L
    if pad:
        q = jnp.pad(q, [(0, 0), (0, pad), (0, 0), (0, 0), (0, 0)])
        k = jnp.pad(k, [(0, 0), (0, pad), (0, 0), (0, 0)])
        v = jnp.pad(v, [(0, 0), (0, pad), (0, 0), (0, 0)])
    nb = Lp // BLOCK
    qb = q.reshape(B, nb, BLOCK, Hk, G, hd)
    kb = k.reshape(B, nb, BLOCK, Hk, hd)
    vb = v.reshape(B, nb, BLOCK, Hk, hd)
    k2 = jnp.concatenate([jnp.concatenate([jnp.zeros_like(kb[:, :1]), kb[:, :-1]], axis=1), kb], axis=2)
    v2 = jnp.concatenate([jnp.concatenate([jnp.zeros_like(vb[:, :1]), vb[:, :-1]], axis=1), vb], axis=2)
    s = jnp.einsum('bnqhgd,bnkhd->bnhgqk', qb, k2, preferred_element_type=jnp.float32) * ATTN_SCALE
    kpos = jnp.arange(2 * BLOCK)
    dist = (jnp.arange(BLOCK) + BLOCK)[:, None] - kpos[None, :]
    band = (dist >= 0) & (dist <= window)
    has_prev = (jnp.arange(nb)[:, None, None] > 0) | (kpos[None, None, :] >= BLOCK)
    mask = band[None] & has_prev
    s = jnp.where(mask[None, :, None, None], s, -jnp.inf)
    lse = jax.nn.logsumexp(s, axis=-1)
    p = jnp.exp(s - lse[..., None]).astype(v.dtype)
    o = jnp.einsum('bnhgqk,bnkhd->bnqhgd', p, v2).reshape(B, Lp, Hk, G, hd)[:, :L]
    lse = lse.transpose(0, 1, 4, 2, 3).reshape(B, Lp, Hk, G)[:, :L]
    return o, lse


def combine_branches(outs, lses):
    wts = jax.nn.softmax(jnp.stack(lses, axis=0), axis=0)
    o = jnp.stack(outs, axis=0).astype(jnp.float32)
    return jnp.sum(wts[..., None] * o, axis=0).astype(outs[0].dtype)


def dilated_attention_prompt(q, k, v):
    B, L, H, hd = q.shape
    outs, lses = [], []
    for window, dil in A_BRANCHES:
        Ls = L // dil

        def to_sub(t):
            return t.reshape(B, Ls, dil, H, hd).transpose(0, 2, 1, 3, 4).reshape(B * dil, Ls, H, hd)

        o, lse = band_attention(to_sub(q)[:, :, :, None, :], to_sub(k), to_sub(v), window // dil)
        outs.append(o[:, :, :, 0].reshape(B, dil, Ls, H, hd).transpose(0, 2, 1, 3, 4).reshape(B, L, H, hd))
        lses.append(lse[..., 0].reshape(B, dil, Ls, H).transpose(0, 2, 1, 3).reshape(B, L, H))
    return combine_branches(outs, lses)


def dilated_attention_sample(q, k_all, v_all, n_buf):
    T = q.shape[1]
    qi = n_buf + jnp.arange(T)
    outs, lses = [], []
    for window, dil in A_BRANCHES:
        n_keys = window // dil + 1
        idx = qi[:, None] - dil * jnp.arange(n_keys)[None, :]
        valid = idx >= 0
        idx = jnp.maximum(idx, 0)
        kg = k_all[:, idx]
        vg = v_all[:, idx]
        s = jnp.einsum('bthd,btjhd->bthj', q, kg, preferred_element_type=jnp.float32) * ATTN_SCALE
        s = jnp.where(valid[None, :, None, :], s, -jnp.inf)
        lse = jax.nn.logsumexp(s, axis=-1)
        p = jnp.exp(s - lse[..., None]).astype(vg.dtype)
        outs.append(jnp.einsum('bthj,btjhd->bthd', p, vg))
        lses.append(lse)
    return combine_branches(outs, lses)


def apply_sink(o, lse, sink):
    scale = jax.nn.sigmoid(lse - sink.astype(jnp.float32))
    return (o.astype(jnp.float32) * scale[..., None]).astype(o.dtype)


def window_attention_prompt(q, k, v, sink):
    o, lse = band_attention(q, k, v, C_WINDOW)
    return apply_sink(o, lse, sink)


def window_attention_sample(q, k_all, v_all, n_buf, sink):
    T = q.shape[1]
    M = k_all.shape[1]
    dist = (n_buf + jnp.arange(T))[:, None] - jnp.arange(M)[None, :]
    mask = (dist >= 0) & (dist <= C_WINDOW)
    s = jnp.einsum('bthgd,bmhd->bhgtm', q, k_all, preferred_element_type=jnp.float32) * ATTN_SCALE
    s = jnp.where(mask, s, -jnp.inf)
    lse = jax.nn.logsumexp(s, axis=-1)
    p = jnp.exp(s - lse[..., None]).astype(v_all.dtype)
    o = jnp.einsum('bhgtm,bmhd->bthgd', p, v_all)
    return apply_sink(o, lse.transpose(0, 3, 1, 2), sink)


def rwkv7_time_mix(pb, pb_prev, s0, mu, w0, w2, a0, a2, g2, k_k, k_a, r_k, lnx_w, lnx_b):
    f32 = jnp.float32
    B, L, _ = pb.shape
    p = pb.astype(f32)
    shifted = jnp.concatenate([pb_prev.astype(f32)[:, None], p[:, :-1]], axis=1)
    p = p + (shifted - p) * mu.astype(f32)
    r, k, v, wl, al, gl = jnp.split(p, B_SPLITS, axis=-1)
    w = -jax.nn.softplus(-(w0.astype(f32) + jnp.tanh(wl) @ w2.astype(f32))) - 0.5
    decay = jnp.exp(-jnp.exp(w))
    a = jax.nn.sigmoid(a0.astype(f32) + al @ a2.astype(f32))
    g = jax.nn.sigmoid(gl) @ g2.astype(f32)

    def heads(t):
        return t.reshape(B, L, B_HEADS, HEAD_DIM)

    kk = heads(k * k_k.astype(f32))
    kk = kk / jnp.maximum(jnp.sqrt(jnp.sum(kk * kk, axis=-1, keepdims=True)), 1e-12)
    k = k * (1.0 + (a - 1.0) * k_a.astype(f32))
    r_h, k_h, v_h, d_h, a_h = heads(r), heads(k), heads(v), heads(decay), heads(a)
    b_h = kk * a_h

    def step(S, inp):
        r_t, d_t, k_t, v_t, kk_t, b_t = inp
        sa = jnp.einsum('bhij,bhj->bhi', S, -kk_t)
        S = S * d_t[:, :, None, :] + sa[..., None] * b_t[:, :, None, :] + v_t[..., None] * k_t[:, :, None, :]
        return S, jnp.einsum('bhij,bhj->bhi', S, r_t)

    xs = tuple(jnp.moveaxis(t, 1, 0) for t in (r_h, d_h, k_h, v_h, kk, b_h))
    S, y = lax.scan(step, s0.astype(f32), xs)
    y = jnp.moveaxis(y, 0, 1)
    mean = jnp.mean(y, axis=-1, keepdims=True)
    var = jnp.mean(jnp.square(y - mean), axis=-1, keepdims=True)
    y = ((y - mean) * lax.rsqrt(var + GN_EPS)).reshape(B, L, B_DIM) * lnx_w.astype(f32) + lnx_b.astype(f32)
    bonus = jnp.sum(r_h * k_h * r_k.astype(f32).reshape(B_HEADS, HEAD_DIM), axis=-1, keepdims=True) * v_h
    y = (y + bonus.reshape(B, L, B_DIM)) * g
    return y.astype(pb.dtype), S.astype(pb.dtype), pb[:, -1]


def trunk_layer(x, pos, lw, past):
    (g_mix, w_in, w_out, mu, w0, w2, a0, a2, g2, k_k, k_a, r_k, lnx_w, lnx_b,
     sink, g_ffn, w_gate, w_up, w_down) = lw
    nb, L, _ = x.shape
    h = rms_norm(x, g_mix)
    p = h @ w_in
    qa, ka, va, pb, qc, kc, vc = jnp.split(p, IN_SPLITS, axis=-1)
    qa = rope_partial(qa.reshape(nb, L, A_HEADS, HEAD_DIM), pos)
    ka = rope_partial(ka.reshape(nb, L, A_HEADS, HEAD_DIM), pos)
    va = va.reshape(nb, L, A_HEADS, HEAD_DIM)
    qc = rope_partial(qc.reshape(nb, L, C_KV_HEADS, C_GROUP, HEAD_DIM), pos)
    kc = rope_partial(kc.reshape(nb, L, C_KV_HEADS, HEAD_DIM), pos)
    vc = vc.reshape(nb, L, C_KV_HEADS, HEAD_DIM)
    rwkv_w = (mu, w0, w2, a0, a2, g2, k_k, k_a, r_k, lnx_w, lnx_b)
    if past is None:
        oa = dilated_attention_prompt(qa, ka, va)
        oc = window_attention_prompt(qc, kc, vc, sink)
        ob, wkv, shift = rwkv7_time_mix(pb, jnp.zeros((nb, B_COLS), p.dtype),
                                        jnp.zeros((nb, B_HEADS, HEAD_DIM, HEAD_DIM), jnp.float32), *rwkv_w)
        a_keep = min(A_MAX_WINDOW, L)
        c_keep = min(C_WINDOW, L)
        new = (ka[:, L - a_keep:], va[:, L - a_keep:], kc[:, L - c_keep:], vc[:, L - c_keep:], wkv, shift)
    else:
        a_k_buf, a_v_buf, c_k_buf, c_v_buf, wkv0, shift0 = past
        oa = dilated_attention_sample(qa, jnp.concatenate([a_k_buf.astype(ka.dtype), ka], axis=1),
                                      jnp.concatenate([a_v_buf.astype(va.dtype), va], axis=1), a_k_buf.shape[1])
        oc = window_attention_sample(qc, jnp.concatenate([c_k_buf.astype(kc.dtype), kc], axis=1),
                                     jnp.concatenate([c_v_buf.astype(vc.dtype), vc], axis=1), c_k_buf.shape[1], sink)
        ob, wkv, shift = rwkv7_time_mix(pb, shift0, wkv0, *rwkv_w)
        new = (ka, va, kc, vc, wkv, shift)
    mixed = jnp.concatenate([oa.reshape(nb, L, A_DIM), ob, oc.reshape(nb, L, C_DIM)], axis=-1)
    x = x + mixed @ w_out
    h = rms_norm(x, g_ffn)
    x = x + (jax.nn.silu(h @ w_gate) * (h @ w_up)) @ w_down
    return x, new


def setup_inputs(seed: int = 0) -> dict:
    key = jax.random.key(seed)
    ks = jax.random.split(key, 32)
    f32 = jnp.float32
    a_win = min(A_MAX_WINDOW, PAST_LEN)
    c_win = min(C_WINDOW, PAST_LEN)

    def nrm(k, shape, scale):
        return jax.random.normal(k, shape, f32) * scale

    return {
        'x_prompt': nrm(ks[0], (BATCH, SEQ, D_MODEL), 1.0),
        'x_sample': nrm(ks[1], (DEC_BATCH, DEC_SEQ, D_MODEL), 1.0),
        'cache_a_k': nrm(ks[2], (DEPTH, DEC_BATCH, a_win, A_HEADS, HEAD_DIM), 1.0),
        'cache_a_v': nrm(ks[3], (DEPTH, DEC_BATCH, a_win, A_HEADS, HEAD_DIM), 1.0),
        'cache_c_k': nrm(ks[4], (DEPTH, DEC_BATCH, c_win, C_KV_HEADS, HEAD_DIM), 1.0),
        'cache_c_v': nrm(ks[5], (DEPTH, DEC_BATCH, c_win, C_KV_HEADS, HEAD_DIM), 1.0),
        'state_b_wkv': nrm(ks[6], (DEPTH, DEC_BATCH, B_HEADS, HEAD_DIM, HEAD_DIM), 0.3),
        'state_b_shift': nrm(ks[7], (DEPTH, DEC_BATCH, B_COLS), 1.0),
        'g_mix': 1.0 + nrm(ks[8], (DEPTH, D_MODEL), 0.02),
        'w_in': nrm(ks[9], (DEPTH, D_MODEL, IN_COLS), D_MODEL ** -0.5),
        'w_out': nrm(ks[10], (DEPTH, MIX_DIM, D_MODEL), 0.5 * MIX_DIM ** -0.5),
        'b_mu': jax.random.uniform(ks[11], (DEPTH, B_COLS), f32, 0.0, 1.0),
        'b_w0': jax.random.uniform(ks[12], (DEPTH, B_DIM), f32, -6.0, -1.0),
        'b_w2': nrm(ks[13], (DEPTH, B_DECAY_LORA, B_DIM), 0.1),
        'b_a0': nrm(ks[14], (DEPTH, B_DIM), 0.1),
        'b_a2': nrm(ks[15], (DEPTH, B_AAA_LORA, B_DIM), 0.1),
        'b_g2': nrm(ks[16], (DEPTH, B_GATE_LORA, B_DIM), B_GATE_LORA ** -0.5),
        'b_k_k': 0.85 + nrm(ks[17], (DEPTH, B_DIM), 0.05),
        'b_k_a': 1.0 + nrm(ks[18], (DEPTH, B_DIM), 0.05),
        'b_r_k': nrm(ks[19], (DEPTH, B_DIM), 0.1),
        'b_lnx_w': 1.0 + nrm(ks[20], (DEPTH, B_DIM), 0.02),
        'b_lnx_b': nrm(ks[21], (DEPTH, B_DIM), 0.02),
        'c_sink': nrm(ks[22], (DEPTH, C_KV_HEADS, C_GROUP), 1.0),
        'g_ffn': 1.0 + nrm(ks[23], (DEPTH, D_MODEL), 0.02),
        'w_gate': nrm(ks[24], (DEPTH, D_MODEL, D_FF), D_MODEL ** -0.5),
        'w_up': nrm(ks[25], (DEPTH, D_MODEL, D_FF), D_MODEL ** -0.5),
        'w_down': nrm(ks[26], (DEPTH, D_FF, D_MODEL), 0.5 * D_FF ** -0.5),
        'g_final': 1.0 + nrm(ks[27], (D_MODEL,), 0.02),
    }


def reference(x_prompt, x_sample, cache_a_k, cache_a_v, cache_c_k, cache_c_v, state_b_wkv, state_b_shift,
              g_mix, w_in, w_out, b_mu, b_w0, b_w2, b_a0, b_a2, b_g2, b_k_k, b_k_a, b_r_k, b_lnx_w, b_lnx_b,
              c_sink, g_ffn, w_gate, w_up, w_down, g_final):
    pos_p = jnp.arange(x_prompt.shape[1], dtype=jnp.int32)
    pos_s = PAST_LEN + jnp.arange(x_sample.shape[1], dtype=jnp.int32)
    hp, hs = x_prompt, x_sample
    new_p = [[] for _ in range(6)]
    new_s = [[] for _ in range(6)]
    for l in range(DEPTH):
        lw = (g_mix[l], w_in[l], w_out[l], b_mu[l], b_w0[l], b_w2[l], b_a0[l], b_a2[l], b_g2[l],
              b_k_k[l], b_k_a[l], b_r_k[l], b_lnx_w[l], b_lnx_b[l], c_sink[l], g_ffn[l],
              w_gate[l], w_up[l], w_down[l])
        hp, st_p = trunk_layer(hp, pos_p, lw, None)
        hs, st_s = trunk_layer(hs, pos_s, lw, (cache_a_k[l], cache_a_v[l], cache_c_k[l], cache_c_v[l],
                                               state_b_wkv[l], state_b_shift[l]))
        for i in range(6):
            new_p[i].append(st_p[i])
            new_s[i].append(st_s[i])
    y_prompt = rms_norm(hp, g_final)
    y_sample = rms_norm(hs, g_final)
    a_k_p, a_v_p, c_k_p, c_v_p, wkv_p, shift_p = [jnp.stack(t, axis=0) for t in new_p]
    a_k_s, a_v_s, c_k_s, c_v_s, wkv_s, shift_s = [jnp.stack(t, axis=0) for t in new_s]
    return (y_prompt, y_sample, a_k_p, a_v_p, c_k_p, c_v_p, wkv_p, shift_p,
            a_k_s, a_v_s, c_k_s, c_v_s, wkv_s, shift_s)
```

```python
import functools
import math

import numpy as np
import jax
import jax.numpy as jnp
from jax import lax
from jax.experimental import pallas as pl
from jax.experimental.pallas import tpu as pltpu

F32 = jnp.float32
BF16 = jnp.bfloat16

LANES = 128
SUBLANES = 8
VMEM_LIMIT = 52 * 1024 * 1024

D_MODEL = 2048
HEAD = 64
A_DIM = 512
B_DIM = 512
C_DIM = 1024
C_KV_DIM = 128
A_HEADS = 8
B_HEADS = 8
C_Q_HEADS = 16
C_GROUP = 8
LORA_COLS = 256
B_COLS = 3 * B_DIM + LORA_COLS
IN_COLS = 3 * A_DIM + B_COLS + C_DIM + 2 * C_KV_DIM
D_FF = 5632
OFF_QA, OFF_KA, OFF_VA = 0, A_DIM, 2 * A_DIM
OFF_PB = 3 * A_DIM
OFF_LORA = OFF_PB + 3 * B_DIM
OFF_QC = OFF_PB + B_COLS
OFF_KC = OFF_QC + C_DIM
OFF_VC = OFF_KC + C_KV_DIM
A_BRANCHES = ((128, 1), (512, 4), (2048, 16))
C_WINDOW = 128
BLOCK = 128
QC_BLK = 256
A_SAMPLE_CHUNK = 4096
PAST_LEN = 16384
ROPE_THETA = 500000.0
ROPE_DIM = 16
RMS_EPS = 1e-6
GN_EPS = 64e-5
ATTN_SCALE = HEAD ** -0.5
NEG = -1e30

_NT = (((1,), (1,)), ((), ()))


def _cparams(n_grid):
    return pltpu.CompilerParams(dimension_semantics=("arbitrary",) * n_grid, vmem_limit_bytes=VMEM_LIMIT)


def _dot(a, b):
    return jnp.dot(a, b, preferred_element_type=F32)


def _split_bf16(x):
    hi = x.astype(BF16)
    lo = (x - hi.astype(F32)).astype(BF16)
    return hi, lo


def _dot_hi(a, b):
    a_hi, a_lo = _split_bf16(a)
    b_hi, b_lo = _split_bf16(b)
    return _dot(a_hi, b_hi) + (_dot(a_lo, b_hi) + _dot(a_hi, b_lo))


def _seg_sum(x, bd):
    hi, lo = _split_bf16(x)
    return _dot(hi, bd) + _dot(lo, bd)


def _sigmoid(x):
    return 1.0 / (1.0 + jnp.exp(-x))


def _rmsnorm_kernel(x_ref, g_ref, o_ref):
    x = x_ref[...]
    ms = jnp.mean(x * x, axis=-1, keepdims=True)
    o_ref[...] = (x * lax.rsqrt(ms + RMS_EPS) * g_ref[...]).astype(o_ref.dtype)


def _rmsnorm(x, g, out_dtype):
    m, d = x.shape
    tm = min(m, 256)
    return pl.pallas_call(
        _rmsnorm_kernel,
        name="rmsnorm",
        grid=(m // tm,),
        in_specs=[pl.BlockSpec((tm, d), lambda i: (i, 0)), pl.BlockSpec((1, d), lambda i: (0, 0))],
        out_specs=pl.BlockSpec((tm, d), lambda i: (i, 0)),
        out_shape=jax.ShapeDtypeStruct((m, d), out_dtype),
        compiler_params=_cparams(1),
    )(x, g.reshape(1, d))


def _inproj_kernel(h_ref, w_ref, flag_ref, c_ref, s1_ref, s2_ref, o_ref, wbf_ref):
    @pl.when(pl.program_id(1) == 0)
    def _():
        wbf_ref[...] = w_ref[...].astype(BF16)

    acc = _dot(h_ref[...], wbf_ref[...])
    c, s1, s2 = c_ref[...], s1_ref[...], s2_ref[...]
    for j in range(acc.shape[1] // LANES):
        sl = slice(j * LANES, (j + 1) * LANES)
        x = acc[:, sl]
        rot = x * c + pltpu.roll(x, LANES - 8, 1) * s1 + pltpu.roll(x, 8, 1) * s2
        o_ref[:, sl] = jnp.where(flag_ref[:, sl] > 0.0, rot, x)


def _inproj(h, w_all, layer, flag, tabs, tm, tn=512):
    m, k = h.shape
    n = w_all.shape[2]
    tab_blocks = tabs[0].shape[0] // tm
    tab_spec = pl.BlockSpec((tm, LANES), lambda j, i: (i % tab_blocks, 0))
    return pl.pallas_call(
        _inproj_kernel,
        name="inproj",
        grid=(n // tn, m // tm),
        in_specs=[
            pl.BlockSpec((tm, k), lambda j, i: (i, 0)),
            pl.BlockSpec((None, k, tn), lambda j, i: (layer, 0, j)),
            pl.BlockSpec((1, tn), lambda j, i: (0, j)),
            tab_spec, tab_spec, tab_spec,
        ],
        out_specs=pl.BlockSpec((tm, tn), lambda j, i: (i, j)),
        out_shape=jax.ShapeDtypeStruct((m, n), F32),
        scratch_shapes=[pltpu.VMEM((k, tn), BF16)],
        compiler_params=_cparams(2),
    )(h, w_all, flag, *tabs)


def _outproj_kernel(x_ref, oa_ref, ob_ref, oc_ref, w_ref, o_ref, wbf_ref):
    @pl.when(pl.program_id(1) == 0)
    def _():
        wbf_ref[...] = w_ref[...].astype(BF16)

    acc = _dot(oa_ref[...].astype(BF16), wbf_ref[0:A_DIM, :])
    acc += _dot(ob_ref[...].astype(BF16), wbf_ref[A_DIM:A_DIM + B_DIM, :])
    acc += _dot(oc_ref[...].astype(BF16), wbf_ref[A_DIM + B_DIM:, :])
    o_ref[...] = x_ref[...] + acc


def _outproj(x, oa, ob, oc, w_all, layer, tm, tn=512):
    m, d = x.shape
    k = w_all.shape[1]
    return pl.pallas_call(
        _outproj_kernel,
        name="outproj",
        grid=(d // tn, m // tm),
        in_specs=[
            pl.BlockSpec((tm, tn), lambda j, i: (i, j)),
            pl.BlockSpec((tm, A_DIM), lambda j, i: (i, 0)),
            pl.BlockSpec((tm, B_DIM), lambda j, i: (i, 0)),
            pl.BlockSpec((tm, C_DIM), lambda j, i: (i, 0)),
            pl.BlockSpec((None, k, tn), lambda j, i: (layer, 0, j)),
        ],
        out_specs=pl.BlockSpec((tm, tn), lambda j, i: (i, j)),
        out_shape=jax.ShapeDtypeStruct((m, d), F32),
        scratch_shapes=[pltpu.VMEM((k, tn), BF16)],
        compiler_params=_cparams(2),
    )(x, oa, ob, oc, w_all)


def _ffn_up_kernel(h_ref, wg_ref, wu_ref, o_ref, wgbf_ref, wubf_ref):
    @pl.when(pl.program_id(1) == 0)
    def _():
        wgbf_ref[...] = wg_ref[...].astype(BF16)
        wubf_ref[...] = wu_ref[...].astype(BF16)

    h = h_ref[...]
    gate = _dot(h, wgbf_ref[...])
    up = _dot(h, wubf_ref[...])
    o_ref[...] = (gate * _sigmoid(gate) * up).astype(o_ref.dtype)


def _ffn_up(h, wg_all, wu_all, layer, tm, tn=512):
    m, k = h.shape
    n = wg_all.shape[2]
    w_spec = pl.BlockSpec((None, k, tn), lambda j, i: (layer, 0, j))
    return pl.pallas_call(
        _ffn_up_kernel,
        name="ffn_up",
        grid=(n // tn, m // tm),
        in_specs=[pl.BlockSpec((tm, k), lambda j, i: (i, 0)), w_spec, w_spec],
        out_specs=pl.BlockSpec((tm, tn), lambda j, i: (i, j)),
        out_shape=jax.ShapeDtypeStruct((m, n), BF16),
        scratch_shapes=[pltpu.VMEM((k, tn), BF16), pltpu.VMEM((k, tn), BF16)],
        compiler_params=_cparams(2),
    )(h, wg_all, wu_all)


def _ffn_down_kernel(x_ref, a_ref, w_ref, o_ref, wbf_ref):
    @pl.when(pl.program_id(1) == 0)
    def _():
        wbf_ref[...] = w_ref[...].astype(BF16)

    o_ref[...] = x_ref[...] + _dot(a_ref[...], wbf_ref[...])


def _ffn_down(x, act, w_all, layer, tm, tn=512):
    m, d = x.shape
    k = w_all.shape[1]
    return pl.pallas_call(
        _ffn_down_kernel,
        name="ffn_down",
        grid=(d // tn, m // tm),
        in_specs=[
            pl.BlockSpec((tm, tn), lambda j, i: (i, j)),
            pl.BlockSpec((tm, k), lambda j, i: (i, 0)),
            pl.BlockSpec((None, k, tn), lambda j, i: (layer, 0, j)),
        ],
        out_specs=pl.BlockSpec((tm, tn), lambda j, i: (i, j)),
        out_shape=jax.ShapeDtypeStruct((m, d), F32),
        scratch_shapes=[pltpu.VMEM((k, tn), BF16)],
        compiler_params=_cparams(2),
    )(x, act, w_all)


def _band_mask(window, has_prev, n_keys):
    qi = lax.broadcasted_iota(jnp.int32, (BLOCK, n_keys), 0) + (n_keys - BLOCK)
    kj = lax.broadcasted_iota(jnp.int32, (BLOCK, n_keys), 1)
    dist = qi - kj
    band = (dist >= 0) & (dist <= window)
    if n_keys > BLOCK:
        band = band & ((kj >= n_keys - BLOCK) | has_prev)
    return band


def _head_attn(q2, k2, v2, lane_sel, mask):
    qm = jnp.where(lane_sel, q2, 0.0).astype(BF16)
    s = lax.dot_general(qm, k2, _NT, preferred_element_type=F32) * ATTN_SCALE
    s = jnp.where(mask, s, NEG)
    m = jnp.max(s, axis=-1, keepdims=True)
    p = jnp.exp(s - m)
    l = jnp.sum(p, axis=-1, keepdims=True)
    o = _dot(p.astype(BF16), v2)
    return o, m, l


def _attn_a_kernel(*refs, window, dil, pairs, with_prev, chained):
    refs = list(refs)
    q_ref = refs.pop(0)
    kp_ref = refs.pop(0) if with_prev else None
    kc_ref = refs.pop(0)
    vp_ref = refs.pop(0) if with_prev else None
    vc_ref = refs.pop(0)
    op_ref, lp_ref = (refs.pop(0), refs.pop(0)) if chained else (None, None)
    o_ref, l_ref = refs
    n_keys = 2 * BLOCK if with_prev else BLOCK
    mask = _band_mask(window, pl.program_id(1) > 0, n_keys)
    lane = lax.broadcasted_iota(jnp.int32, (BLOCK, LANES), 1)
    lane_lo = lane < HEAD
    for r in range(dil):
        rows = pl.ds(r, BLOCK, stride=dil) if dil > 1 else slice(None)
        for pp in range(pairs):
            sl = slice(pp * LANES, (pp + 1) * LANES)
            q2 = q_ref[rows, sl]
            if with_prev:
                k2 = jnp.concatenate([kp_ref[rows, sl], kc_ref[rows, sl]], axis=0).astype(BF16)
                v2 = jnp.concatenate([vp_ref[rows, sl], vc_ref[rows, sl]], axis=0).astype(BF16)
            else:
                k2 = kc_ref[rows, sl].astype(BF16)
                v2 = vc_ref[rows, sl].astype(BF16)
            if chained:
                o_prev = op_ref[rows, sl]
                l_prev = lp_ref[rows, sl]
            halves = []
            lse_blk = jnp.zeros((BLOCK, LANES), F32)
            for hh in range(2):
                o, m, l = _head_attn(q2, k2, v2, lane_lo if hh == 0 else ~lane_lo, mask)
                o = o / l
                lse = m + jnp.log(l)
                if chained:
                    lse_p = jnp.sum(jnp.where(lane == hh, l_prev, 0.0), axis=-1, keepdims=True)
                    mx = jnp.maximum(lse_p, lse)
                    wp = jnp.exp(lse_p - mx)
                    wi = jnp.exp(lse - mx)
                    den = wp + wi
                    o = (o_prev * wp + o * wi) / den
                    lse = mx + jnp.log(den)
                halves.append(o)
                lse_blk = jnp.where(lane == hh, lse, lse_blk)
            o_ref[rows, sl] = jnp.where(lane_lo, halves[0], halves[1])
            l_ref[rows, sl] = lse_blk


def _attn_a_branch(p2d, o_prev, l_prev, batch, seq, window, dil):
    rows = batch * seq
    tok = dil * BLOCK
    nblk = seq // tok
    with_prev = nblk > 1
    pairs = 1 if dil > 1 else A_HEADS // 2
    width = pairs * LANES
    n_col = A_DIM // width
    chained = o_prev is not None

    def cur(off):
        return lambda b, n, c: (b * nblk + n, off // width + c)

    def prev(off):
        return lambda b, n, c: (b * nblk + jnp.maximum(n - 1, 0), off // width + c)

    blk = (tok, width)
    in_specs = [pl.BlockSpec(blk, cur(OFF_QA))]
    in_specs += [pl.BlockSpec(blk, prev(OFF_KA))] if with_prev else []
    in_specs += [pl.BlockSpec(blk, cur(OFF_KA))]
    in_specs += [pl.BlockSpec(blk, prev(OFF_VA))] if with_prev else []
    in_specs += [pl.BlockSpec(blk, cur(OFF_VA))]
    args = [p2d] * len(in_specs)
    o_spec = pl.BlockSpec(blk, cur(0))
    if chained:
        in_specs += [o_spec, o_spec]
        args += [o_prev, l_prev]
    o, l = pl.pallas_call(
        functools.partial(_attn_a_kernel, window=window // dil, dil=dil, pairs=pairs, with_prev=with_prev,
                          chained=chained),
        grid=(batch, nblk, n_col),
        in_specs=in_specs,
        out_specs=[o_spec, o_spec],
        out_shape=[jax.ShapeDtypeStruct((rows, A_DIM), F32), jax.ShapeDtypeStruct((rows, A_DIM), F32)],
        compiler_params=_cparams(3),
        name=f"attn_a_dil{dil}",
    )(*args)
    return o, l


def _attn_c_kernel(sink_ref, q0_ref, q1_ref, q2_ref, q3_ref, kp_ref, kc_ref, vp_ref, vc_ref, o_ref):
    q_refs = (q0_ref, q1_ref, q2_ref, q3_ref)
    has_prev = pl.program_id(1) > 0
    mask = _band_mask(C_WINDOW, has_prev, 2 * BLOCK)
    lane = lax.broadcasted_iota(jnp.int32, (BLOCK, LANES), 1)
    lane_lo = lane < HEAD
    lane_lo2 = lax.broadcasted_iota(jnp.int32, (2 * BLOCK, LANES), 1) < HEAD
    k2 = jnp.concatenate([kp_ref[...], kc_ref[...]], axis=0)
    v2 = jnp.concatenate([vp_ref[...], vc_ref[...]], axis=0)
    k2r = pltpu.roll(k2, HEAD, 1)
    v2r = pltpu.roll(v2, HEAD, 1)
    kdup = [jnp.where(lane_lo2, k2, k2r).astype(BF16), jnp.where(lane_lo2, k2r, k2).astype(BF16)]
    vdup = [jnp.where(lane_lo2, v2, v2r).astype(BF16), jnp.where(lane_lo2, v2r, v2).astype(BF16)]
    for pr in range(C_Q_HEADS // 2):
        g = (2 * pr) // C_GROUP
        sl = slice(pr * LANES, (pr + 1) * LANES)
        q2 = q_refs[pr // 2][:, (pr % 2) * LANES:(pr % 2 + 1) * LANES]
        halves = []
        for hh in range(2):
            o, m, l = _head_attn(q2, kdup[g], vdup[g], lane_lo if hh == 0 else ~lane_lo, mask)
            lse = m + jnp.log(l)
            halves.append(o * (_sigmoid(lse - sink_ref[2 * pr + hh]) / l))
        o_ref[:, sl] = jnp.where(lane_lo, halves[0], halves[1])


def _attn_c_prompt(p2d, sink, batch, seq):
    rows = batch * seq
    nb = seq // BLOCK

    def cur(col):
        return lambda b, n: (b * nb + n, col)

    def prev(col):
        return lambda b, n: (b * nb + jnp.maximum(n - 1, 0), col)

    kv_blk = (BLOCK, C_KV_DIM)
    return pl.pallas_call(
        _attn_c_kernel,
        name="attn_c",
        grid=(batch, nb),
        in_specs=[
            pl.BlockSpec(memory_space=pltpu.SMEM),
            *[pl.BlockSpec((BLOCK, QC_BLK), cur(OFF_QC // QC_BLK + i)) for i in range(C_DIM // QC_BLK)],
            pl.BlockSpec(kv_blk, prev(OFF_KC // C_KV_DIM)),
            pl.BlockSpec(kv_blk, cur(OFF_KC // C_KV_DIM)),
            pl.BlockSpec(kv_blk, prev(OFF_VC // C_KV_DIM)),
            pl.BlockSpec(kv_blk, cur(OFF_VC // C_KV_DIM)),
        ],
        out_specs=pl.BlockSpec((BLOCK, C_DIM), lambda b, n: (b * nb + n, 0)),
        out_shape=jax.ShapeDtypeStruct((rows, C_DIM), F32),
        compiler_params=_cparams(2),
    )(sink, *([p2d] * (C_DIM // QC_BLK)), p2d, p2d, p2d, p2d)


def _attn_a_sample_kernel(q_ref, kn_ref, vn_ref, kc_ref, vc_ref, cc_ref, cn_ref, o_ref):
    t = q_ref.shape[0]
    pad = jnp.zeros((BLOCK - t, A_DIM), F32)
    k_new = jnp.concatenate([kn_ref[...], pad], axis=0)
    v_new = jnp.concatenate([vn_ref[...], pad], axis=0)
    cnt_c, cnt_n = cc_ref[...], cn_ref[...]
    outs = []
    for h in range(A_HEADS):
        lanes = slice(h * HEAD, (h + 1) * HEAD)
        q = q_ref[:, lanes].astype(BF16)
        s_c = _dot(q, kc_ref[h].astype(BF16)) * ATTN_SCALE
        s_n = lax.dot_general(q, k_new[:, lanes].astype(BF16), _NT, preferred_element_type=F32) * ATTN_SCALE
        s_c = jnp.where(cnt_c > 0.0, s_c, NEG)
        s_n = jnp.where(cnt_n > 0.0, s_n, NEG)
        m = jnp.maximum(jnp.max(s_c, axis=-1, keepdims=True), jnp.max(s_n, axis=-1, keepdims=True))
        p_c = cnt_c * jnp.exp(s_c - m)
        p_n = cnt_n * jnp.exp(s_n - m)
        l = jnp.sum(p_c, axis=-1, keepdims=True) + jnp.sum(p_n, axis=-1, keepdims=True)
        o = lax.dot_general(p_c.astype(BF16), vc_ref[h].astype(BF16), _NT, preferred_element_type=F32)
        o += _dot(p_n.astype(BF16), v_new[:, lanes].astype(BF16))
        outs.append(o / l)
    o_ref[...] = jnp.concatenate(outs, axis=-1)


def _a_sample_counts(t, n_buf):
    qi = n_buf + np.arange(t)[:, None]

    def count(rows):
        delta = qi - rows[None, :]
        c = np.zeros(delta.shape, np.float32)
        for window, dil in A_BRANCHES:
            c += ((delta >= 0) & (delta <= window) & (delta % dil == 0)).astype(np.float32)
        return c

    return count(np.arange(n_buf)), count(n_buf + np.arange(BLOCK))


def _attn_a_sample(ps, cache_k, cache_v, layer, batch, t):
    n_buf = cache_k.shape[4]
    cnt_c, cnt_n = _a_sample_counts(t, n_buf)
    new_blk = (t, A_DIM)
    cache_spec = pl.BlockSpec((None, None, A_HEADS, HEAD, n_buf), lambda b: (layer, b, 0, 0, 0))
    return pl.pallas_call(
        _attn_a_sample_kernel,
        name="attn_a_sample",
        grid=(batch,),
        in_specs=[
            pl.BlockSpec(new_blk, lambda b: (b, OFF_QA // A_DIM)),
            pl.BlockSpec(new_blk, lambda b: (b, OFF_KA // A_DIM)),
            pl.BlockSpec(new_blk, lambda b: (b, OFF_VA // A_DIM)),
            cache_spec, cache_spec,
            pl.BlockSpec(cnt_c.shape, lambda b: (0, 0)),
            pl.BlockSpec(cnt_n.shape, lambda b: (0, 0)),
        ],
        out_specs=pl.BlockSpec(new_blk, lambda b: (b, 0)),
        out_shape=jax.ShapeDtypeStruct((batch * t, A_DIM), F32),
        compiler_params=_cparams(1),
    )(ps, ps, ps, cache_k, cache_v, jnp.asarray(cnt_c), jnp.asarray(cnt_n))


def _attn_c_sample_kernel(q0_ref, q1_ref, q2_ref, q3_ref, kn_ref, vn_ref, kc_ref, vc_ref, sink_ref, o_ref):
    q_refs = (q0_ref, q1_ref, q2_ref, q3_ref)
    t = q0_ref.shape[0]
    n_buf = kc_ref.shape[1]
    rows = C_Q_HEADS * t
    lane_lo = lax.broadcasted_iota(jnp.int32, (t, LANES), 1) < HEAD
    blocks = []
    for j in range(C_Q_HEADS // 2):
        chunk = q_refs[j // 2][:, (j % 2) * LANES:(j % 2 + 1) * LANES]
        rolled = pltpu.roll(chunk, HEAD, 1)
        if (2 * j) // C_GROUP == 0:
            blocks += [jnp.where(lane_lo, chunk, 0.0), jnp.where(lane_lo, rolled, 0.0)]
        else:
            blocks += [jnp.where(lane_lo, 0.0, rolled), jnp.where(lane_lo, 0.0, chunk)]
    qbd = jnp.concatenate(blocks, axis=0).astype(BF16)
    pad = jnp.zeros((BLOCK - t, C_KV_DIM), F32)
    k_new = jnp.concatenate([kn_ref[...], pad], axis=0).astype(BF16)
    v_new = jnp.concatenate([vn_ref[...], pad], axis=0).astype(BF16)
    s_c = _dot(qbd, kc_ref[...].astype(BF16)) * ATTN_SCALE
    s_n = lax.dot_general(qbd, k_new, _NT, preferred_element_type=F32) * ATTN_SCALE
    qt = lax.broadcasted_iota(jnp.int32, (rows, BLOCK), 0) % t
    kj = lax.broadcasted_iota(jnp.int32, (rows, BLOCK), 1)
    dist_c = n_buf + qt - kj
    s_c = jnp.where((dist_c >= 0) & (dist_c <= C_WINDOW), s_c, NEG)
    s_n = jnp.where(kj <= qt, s_n, NEG)
    m = jnp.maximum(jnp.max(s_c, axis=-1, keepdims=True), jnp.max(s_n, axis=-1, keepdims=True))
    p_c = jnp.exp(s_c - m)
    p_n = jnp.exp(s_n - m)
    l = jnp.sum(p_c, axis=-1, keepdims=True) + jnp.sum(p_n, axis=-1, keepdims=True)
    o = lax.dot_general(p_c.astype(BF16), vc_ref[...].astype(BF16), _NT, preferred_element_type=F32)
    o += _dot(p_n.astype(BF16), v_new)
    lse = m + jnp.log(l)
    o = o * (_sigmoid(lse - sink_ref[...]) / l)
    for j in range(C_Q_HEADS // 2):
        blk_a = o[2 * j * t:(2 * j + 1) * t, :]
        blk_b = o[(2 * j + 1) * t:(2 * j + 2) * t, :]
        if (2 * j) // C_GROUP == 0:
            out = jnp.where(lane_lo, blk_a, pltpu.roll(blk_b, HEAD, 1))
        else:
            out = jnp.where(lane_lo, pltpu.roll(blk_a, HEAD, 1), blk_b)
        o_ref[:, j * LANES:(j + 1) * LANES] = out


def _attn_c_sample(ps, cache_k, cache_v, layer, sink_col, batch, t):
    n_buf = cache_k.shape[3]
    assert n_buf == BLOCK
    kv_blk = (t, C_KV_DIM)
    cache_spec = pl.BlockSpec((None, None, C_KV_DIM, n_buf), lambda b: (layer, b, 0, 0))
    return pl.pallas_call(
        _attn_c_sample_kernel,
        name="attn_c_sample",
        grid=(batch,),
        in_specs=[
            *[pl.BlockSpec((t, QC_BLK), functools.partial(lambda b, i: (b, OFF_QC // QC_BLK + i), i=i))
              for i in range(C_DIM // QC_BLK)],
            pl.BlockSpec(kv_blk, lambda b: (b, OFF_KC // C_KV_DIM)),
            pl.BlockSpec(kv_blk, lambda b: (b, OFF_VC // C_KV_DIM)),
            cache_spec, cache_spec,
            pl.BlockSpec(sink_col.shape, lambda b: (0, 0)),
        ],
        out_specs=pl.BlockSpec((t, C_DIM), lambda b: (b, 0)),
        out_shape=jax.ShapeDtypeStruct((batch * t, C_DIM), F32),
        compiler_params=_cparams(1),
    )(*([ps] * (C_DIM // QC_BLK)), ps, ps, cache_k, cache_v, sink_col)


def _rwkv_kernel(r_ref, k_ref, v_ref, lo_ref, shift_ref, s0_ref, mu_ref, vec_ref, w2_ref, a2_ref, g2_ref,
                 o_ref, sout_ref,
                 s_scr, r_scr, d_scr, k_scr, v_scr, kk_scr, b_scr, g_scr, y_scr, cx_scr, cl_scr, *, nb, tc):
    c = pl.program_id(1)
    n_pairs = B_HEADS // 2
    f32 = F32

    li = lax.broadcasted_iota(jnp.int32, (LANES, LANES), 0) // HEAD
    lj = lax.broadcasted_iota(jnp.int32, (LANES, LANES), 1) // HEAD
    bd = (li == lj).astype(BF16)

    @pl.when(c == 0)
    def _():
        for b in range(nb):
            for p in range(n_pairs):
                s_scr[b * n_pairs + p] = jnp.concatenate([s0_ref[b, 2 * p], s0_ref[b, 2 * p + 1]], axis=-1)
            cx_scr[b] = jnp.broadcast_to(shift_ref[0, b:b + 1, 0:3 * B_DIM], (SUBLANES, 3 * B_DIM))
            cl_scr[b] = jnp.broadcast_to(shift_ref[0, b:b + 1, 3 * B_DIM:], (SUBLANES, LORA_COLS))

    row0 = lax.broadcasted_iota(jnp.int32, (tc, 1), 0) == 0

    def lerp(cur, prev_row, mu):
        sh = cur if tc == 1 else pltpu.roll(cur, 1, 0)
        sh = jnp.where(row0, prev_row, sh)
        return cur + (sh - cur) * mu

    w0, a0, k_k, k_a = vec_ref[0:1, :], vec_ref[1:2, :], vec_ref[2:3, :], vec_ref[3:4, :]
    r_k, lnx_w, lnx_b = vec_ref[4:5, :], vec_ref[5:6, :], vec_ref[6:7, :]

    for b in range(nb):
        r_raw, k_raw, v_raw, lo_raw = r_ref[b], k_ref[b], v_ref[b], lo_ref[b]
        r = lerp(r_raw, cx_scr[b, 0:1, 0:B_DIM], mu_ref[:, 0:B_DIM])
        k = lerp(k_raw, cx_scr[b, 0:1, B_DIM:2 * B_DIM], mu_ref[:, B_DIM:2 * B_DIM])
        v = lerp(v_raw, cx_scr[b, 0:1, 2 * B_DIM:], mu_ref[:, 2 * B_DIM:3 * B_DIM])
        lo = lerp(lo_raw, cl_scr[b, 0:1, :], mu_ref[:, 3 * B_DIM:])
        cx_scr[b, :, 0:B_DIM] = jnp.broadcast_to(r_raw[tc - 1:tc, :], (SUBLANES, B_DIM))
        cx_scr[b, :, B_DIM:2 * B_DIM] = jnp.broadcast_to(k_raw[tc - 1:tc, :], (SUBLANES, B_DIM))
        cx_scr[b, :, 2 * B_DIM:] = jnp.broadcast_to(v_raw[tc - 1:tc, :], (SUBLANES, B_DIM))
        cl_scr[b] = jnp.broadcast_to(lo_raw[tc - 1:tc, :], (SUBLANES, LORA_COLS))

        z = w0 + _dot_hi(jnp.tanh(lo), w2_ref[...])
        sp = jnp.maximum(-z, 0.0) + jnp.log(1.0 + jnp.exp(-jnp.abs(z)))
        decay = jnp.exp(-jnp.exp(-sp - 0.5))
        a = _sigmoid(a0 + _dot_hi(lo, a2_ref[...]))
        g = _dot_hi(_sigmoid(lo), g2_ref[...])
        kkr = k * k_k
        sq = kkr * kkr
        ss = jnp.concatenate([_seg_sum(sq[:, j * LANES:(j + 1) * LANES], bd) for j in range(n_pairs)], axis=-1)
        kk = kkr / jnp.maximum(jnp.sqrt(ss), 1e-12)
        k = k * (1.0 + (a - 1.0) * k_a)
        r_scr[b] = r
        d_scr[b] = decay
        k_scr[b] = k
        v_scr[b] = v
        kk_scr[b] = kk
        b_scr[b] = kk * a
        g_scr[b] = g

    sub = lax.broadcasted_iota(jnp.int32, (HEAD, LANES), 0)
    lane = lax.broadcasted_iota(jnp.int32, (HEAD, LANES), 1)
    diag = (lane % HEAD) == sub
    sub8 = lax.broadcasted_iota(jnp.int32, (SUBLANES, LANES), 0)
    lane8 = lax.broadcasted_iota(jnp.int32, (SUBLANES, LANES), 1)
    head_row = sub8 == (lane8 // HEAD)
    lane8_lo = lane8[0:1, :] < HEAD

    def step8(t8, carry):
        t0 = pl.multiple_of(t8 * SUBLANES, SUBLANES)
        for b in range(nb):
            for p in range(n_pairs):
                sl = slice(p * LANES, (p + 1) * LANES)
                idx = b * n_pairs + p
                rows = [scr[b, pl.ds(t0, SUBLANES), sl] for scr in (kk_scr, v_scr, d_scr, b_scr, k_scr, r_scr)]
                s = s_scr[idx]
                yblk = jnp.zeros((SUBLANES, LANES), f32)
                for j in range(SUBLANES):
                    kk_t, v_t, d_t, b_t, k_t, r_t = [x[j:j + 1, :] for x in rows]
                    sa = _dot((s * kk_t).astype(BF16), bd)
                    vb = _dot(jnp.where(diag, v_t, 0.0).astype(BF16), bd)
                    s = s * d_t - sa * b_t + vb * k_t
                    rsel = jnp.where(head_row, r_t, 0.0).astype(BF16)
                    s_bf = s.astype(BF16)
                    y2 = lax.dot_general(rsel, jnp.concatenate([s_bf, s_bf], axis=0), _NT, preferred_element_type=f32)
                    yblk = jnp.where(sub8 == j, jnp.where(lane8_lo, y2[0:1, :], y2[1:2, :]), yblk)
                s_scr[idx] = s
                y_scr[b, pl.ds(t0, SUBLANES), sl] = yblk
        return carry

    lax.fori_loop(0, tc // SUBLANES, step8, 0)

    for b in range(nb):
        outs = []
        for j in range(n_pairs):
            sl = slice(j * LANES, (j + 1) * LANES)
            y = y_scr[b, :, sl]
            mean = _seg_sum(y, bd) * (1.0 / HEAD)
            yc = y - mean
            var = _seg_sum(yc * yc, bd) * (1.0 / HEAD)
            yn = yc * lax.rsqrt(var + GN_EPS) * lnx_w[:, sl] + lnx_b[:, sl]
            bonus = _seg_sum(r_scr[b, :, sl] * k_scr[b, :, sl] * r_k[:, sl], bd) * v_scr[b, :, sl]
            outs.append((yn + bonus) * g_scr[b, :, sl])
        o_ref[b] = jnp.concatenate(outs, axis=-1)

    @pl.when(c == pl.num_programs(1) - 1)
    def _():
        for b in range(nb):
            for p in range(n_pairs):
                s = s_scr[b * n_pairs + p]
                sout_ref[b, 2 * p] = s[:, 0:HEAD]
                sout_ref[b, 2 * p + 1] = s[:, HEAD:]


def _rwkv(p3d, shift0, s0, mu, vecs, w2p, a2p, g2p, nb, tc):
    batch, seq, _ = p3d.shape
    groups = batch // nb
    chunks = seq // tc
    x_blk = (nb, tc, B_DIM)

    def xmap(col):
        return lambda g, c: (g, c, col)

    const2 = lambda g, c: (0, 0)
    scr = lambda *shape: pltpu.VMEM(shape, F32)
    o, s_out = pl.pallas_call(
        functools.partial(_rwkv_kernel, nb=nb, tc=tc),
        name="rwkv7",
        grid=(groups, chunks),
        in_specs=[
            pl.BlockSpec(x_blk, xmap(OFF_PB // B_DIM)),
            pl.BlockSpec(x_blk, xmap(OFF_PB // B_DIM + 1)),
            pl.BlockSpec(x_blk, xmap(OFF_PB // B_DIM + 2)),
            pl.BlockSpec((nb, tc, LORA_COLS), xmap(OFF_LORA // LORA_COLS)),
            pl.BlockSpec((1, nb, B_COLS), lambda g, c: (g, 0, 0)),
            pl.BlockSpec((nb, B_HEADS, HEAD, HEAD), lambda g, c: (g, 0, 0, 0)),
            pl.BlockSpec((1, B_COLS), const2),
            pl.BlockSpec((SUBLANES, B_DIM), const2),
            pl.BlockSpec((LORA_COLS, B_DIM), const2),
            pl.BlockSpec((LORA_COLS, B_DIM), const2),
            pl.BlockSpec((LORA_COLS, B_DIM), const2),
        ],
        out_specs=[
            pl.BlockSpec(x_blk, lambda g, c: (g, c, 0)),
            pl.BlockSpec((nb, B_HEADS, HEAD, HEAD), lambda g, c: (g, 0, 0, 0)),
        ],
        out_shape=[jax.ShapeDtypeStruct((batch, seq, B_DIM), F32),
                   jax.ShapeDtypeStruct((batch, B_HEADS, HEAD, HEAD), F32)],
        scratch_shapes=[scr(nb * B_HEADS // 2, HEAD, LANES)] + [scr(nb, tc, B_DIM)] * 8
        + [scr(nb, SUBLANES, 3 * B_DIM), scr(nb, SUBLANES, LORA_COLS)],
        compiler_params=_cparams(2),
    )(p3d, p3d, p3d, p3d, shift0.reshape(groups, nb, B_COLS), s0, mu, vecs, w2p, a2p, g2p)
    return o, s_out


def _rope_tables(pos):
    half = ROPE_DIM // 2
    inv = jnp.exp(-math.log(ROPE_THETA) * jnp.arange(half, dtype=F32) * 2.0 / ROPE_DIM)
    ang = pos.astype(F32)[:, None] * inv[None, :]
    cos, sin = jnp.cos(ang), jnp.sin(ang)
    lm = np.arange(LANES) % HEAD
    first = jnp.asarray(lm < half)[None, :]
    second = jnp.asarray((lm >= half) & (lm < ROPE_DIM))[None, :]
    freq = np.where(lm < half, lm, np.where(lm < ROPE_DIM, lm - half, 0))
    cos_l, sin_l = cos[:, freq], sin[:, freq]
    c = jnp.where(first | second, cos_l, 1.0)
    s1 = jnp.where(first, -sin_l, 0.0)
    s2 = jnp.where(second, sin_l, 0.0)
    return c, s1, s2


def _rope_flag():
    col = np.arange(IN_COLS)
    rope = (col < OFF_VA) | ((col >= OFF_QC) & (col < OFF_VC))
    return jnp.asarray(rope.astype(np.float32))[None, :]


def _pad_rows(w, start):
    return jnp.zeros((LORA_COLS, B_DIM), F32).at[start:start + w.shape[0]].set(w)


def _mixers(p2d, batch, seq, layer, is_prompt, cache, rwkv_w, sink):
    p3d = p2d.reshape(batch, seq, IN_COLS)
    mu, vecs, w2p, a2p, g2p = rwkv_w
    if is_prompt:
        o, l = None, None
        for window, dil in A_BRANCHES:
            o, l = _attn_a_branch(p2d, o, l, batch, seq, window, dil)
        oa = o
        oc = _attn_c_prompt(p2d, sink, batch, seq)
        shift0 = jnp.zeros((batch, B_COLS), F32)
        s0 = jnp.zeros((batch, B_HEADS, HEAD, HEAD), F32)
        ob, wkv = _rwkv(p3d, shift0, s0, mu, vecs, w2p, a2p, g2p, nb=batch, tc=min(seq, 128))
    else:
        a_k, a_v, c_k, c_v, wkv0, shift0 = cache
        oa = _attn_a_sample(p2d, a_k, a_v, layer, batch, seq)
        sink_col = jnp.repeat(sink, seq)[:, None]
        oc = _attn_c_sample(p2d, c_k, c_v, layer, sink_col, batch, seq)
        ob, wkv = _rwkv(p3d, shift0[layer], wkv0[layer], mu, vecs, w2p, a2p, g2p, nb=4, tc=seq)
    return oa, ob.reshape(batch * seq, B_DIM), oc, wkv


def kernel(x_prompt, x_sample, cache_a_k, cache_a_v, cache_c_k, cache_c_v, state_b_wkv, state_b_shift, g_mix, w_in, w_out, b_mu, b_w0, b_w2, b_a0, b_a2, b_g2, b_k_k, b_k_a, b_r_k, b_lnx_w, b_lnx_b, c_sink, g_ffn, w_gate, w_up, w_down, g_final):
    depth = w_in.shape[0]
    bp, lp, d = x_prompt.shape
    bs, ls, _ = x_sample.shape
    a_win = cache_a_k.shape[2]
    c_win = cache_c_k.shape[2]
    assert lp % (16 * BLOCK) == 0 and a_win >= A_BRANCHES[-1][0] and c_win == C_WINDOW and bs % 4 == 0

    flag = _rope_flag()
    tabs_p = _rope_tables(jnp.arange(lp, dtype=jnp.int32))
    tabs_s = _rope_tables(jnp.tile(PAST_LEN + jnp.arange(ls, dtype=jnp.int32), bs))
    cak = jnp.transpose(cache_a_k, (0, 1, 3, 4, 2))
    cav = jnp.transpose(cache_a_v, (0, 1, 3, 4, 2))
    cck = jnp.transpose(cache_c_k, (0, 1, 3, 4, 2)).reshape(depth, bs, C_KV_DIM, c_win)
    ccv = jnp.transpose(cache_c_v, (0, 1, 3, 4, 2)).reshape(depth, bs, C_KV_DIM, c_win)
    cache = (cak, cav, cck, ccv, state_b_wkv, state_b_shift)

    xp = x_prompt.reshape(bp * lp, d)
    xs = x_sample.reshape(bs * ls, d)
    tm_p, tm_s = 1024, bs * ls
    new_p = [[] for _ in range(6)]
    new_s = [[] for _ in range(6)]
    for l in range(depth):
        vecs = jnp.stack([b_w0[l], b_a0[l], b_k_k[l], b_k_a[l], b_r_k[l], b_lnx_w[l], b_lnx_b[l],
                          jnp.zeros((B_DIM,), F32)], axis=0)
        rwkv_w = (b_mu[l][None, :], vecs, _pad_rows(b_w2[l], 0), _pad_rows(b_a2[l], 96), _pad_rows(b_g2[l], 192))
        sink = c_sink[l].reshape(C_Q_HEADS)
        streams = []
        for x, batch, seq, tm, tabs, is_prompt in ((xp, bp, lp, tm_p, tabs_p, True), (xs, bs, ls, tm_s, tabs_s, False)):
            h = _rmsnorm(x, g_mix[l], BF16)
            p2d = _inproj(h, w_in, l, flag, tabs, tm)
            oa, ob, oc, wkv = _mixers(p2d, batch, seq, l, is_prompt, cache, rwkv_w, sink)
            x = _outproj(x, oa, ob, oc, w_out, l, tm)
            h = _rmsnorm(x, g_ffn[l], BF16)
            act = _ffn_up(h, w_gate, w_up, l, tm)
            x = _ffn_down(x, act, w_down, l, min(tm, 256))
            p3d = p2d.reshape(batch, seq, IN_COLS)
            a_keep = min(a_win, seq) if is_prompt else seq
            c_keep = min(c_win, seq) if is_prompt else seq
            new = (
                p3d[:, seq - a_keep:, OFF_KA:OFF_VA].reshape(batch, a_keep, A_HEADS, HEAD),
                p3d[:, seq - a_keep:, OFF_VA:OFF_PB].reshape(batch, a_keep, A_HEADS, HEAD),
                p3d[:, seq - c_keep:, OFF_KC:OFF_VC].reshape(batch, c_keep, C_KV_DIM // HEAD, HEAD),
                p3d[:, seq - c_keep:, OFF_VC:].reshape(batch, c_keep, C_KV_DIM // HEAD, HEAD),
                wkv,
                p3d[:, -1, OFF_PB:OFF_QC],
            )
            streams.append((x, new))
        (xp, st_p), (xs, st_s) = streams
        for i in range(6):
            new_p[i].append(st_p[i])
            new_s[i].append(st_s[i])
    y_prompt = _rmsnorm(xp, g_final, F32).reshape(bp, lp, d)
    y_sample = _rmsnorm(xs, g_final, F32).reshape(bs, ls, d)
    outs_p = [jnp.stack(t, axis=0) for t in new_p]
    outs_s = [jnp.stack(t, axis=0) for t in new_s]
    return (y_prompt, y_sample, *outs_p, *outs_s)
```

```python
import functools
import math

import numpy as np
import jax
import jax.numpy as jnp
from jax import lax
from jax.experimental import pallas as pl
from jax.experimental.pallas import tpu as pltpu

F32 = jnp.float32
BF16 = jnp.bfloat16

LANES = 128
SUBLANES = 8
VMEM_LIMIT = 52 * 1024 * 1024

D_MODEL = 2048
HEAD = 64
A_DIM = 512
B_DIM = 512
C_DIM = 1024
C_KV_DIM = 128
A_HEADS = 8
B_HEADS = 8
C_Q_HEADS = 16
C_GROUP = 8
LORA_COLS = 256
B_COLS = 3 * B_DIM + LORA_COLS
IN_COLS = 3 * A_DIM + B_COLS + C_DIM + 2 * C_KV_DIM
D_FF = 5632
OFF_QA, OFF_KA, OFF_VA = 0, A_DIM, 2 * A_DIM
OFF_PB = 3 * A_DIM
OFF_LORA = OFF_PB + 3 * B_DIM
OFF_QC = OFF_PB + B_COLS
OFF_KC = OFF_QC + C_DIM
OFF_VC = OFF_KC + C_KV_DIM
A_BRANCHES = ((128, 1), (512, 4), (2048, 16))
C_WINDOW = 128
BLOCK = 128
QC_BLK = 256
RWKV_MATMUL_PAIRS = 8
PAST_LEN = 16384
ROPE_THETA = 500000.0
ROPE_DIM = 16
RMS_EPS = 1e-6
GN_EPS = 64e-5
ATTN_SCALE = HEAD ** -0.5
NEG = -1e30

_NT = (((1,), (1,)), ((), ()))


def _cparams(n_grid):
    return pltpu.CompilerParams(dimension_semantics=("arbitrary",) * n_grid, vmem_limit_bytes=VMEM_LIMIT)


def _dot(a, b):
    return jnp.dot(a, b, preferred_element_type=F32)


def _split_bf16(x):
    hi = x.astype(BF16)
    lo = (x - hi.astype(F32)).astype(BF16)
    return hi, lo


def _dot_hi(a, b):
    a_hi, a_lo = _split_bf16(a)
    b_hi, b_lo = _split_bf16(b)
    return _dot(a_hi, b_hi) + (_dot(a_lo, b_hi) + _dot(a_hi, b_lo))


def _seg_sum(x, bd):
    hi, lo = _split_bf16(x)
    return _dot(hi, bd) + _dot(lo, bd)


def _sigmoid(x):
    return 1.0 / (1.0 + jnp.exp(-x))


def _rmsnorm_kernel(x_ref, g_ref, o_ref):
    x = x_ref[...]
    ms = jnp.mean(x * x, axis=-1, keepdims=True)
    o_ref[...] = (x * lax.rsqrt(ms + RMS_EPS) * g_ref[...]).astype(o_ref.dtype)


def _rmsnorm(x, g, out_dtype):
    m, d = x.shape
    tm = min(m, 256)
    return pl.pallas_call(
        _rmsnorm_kernel,
        name="rmsnorm",
        grid=(m // tm,),
        in_specs=[pl.BlockSpec((tm, d), lambda i: (i, 0)), pl.BlockSpec((1, d), lambda i: (0, 0))],
        out_specs=pl.BlockSpec((tm, d), lambda i: (i, 0)),
        out_shape=jax.ShapeDtypeStruct((m, d), out_dtype),
        compiler_params=_cparams(1),
    )(x, g.reshape(1, d))


def _inproj_kernel(h_ref, w_ref, flag_ref, c_ref, s1_ref, s2_ref, o_ref, wbf_ref):
    @pl.when(pl.program_id(1) == 0)
    def _():
        wbf_ref[...] = w_ref[...].astype(BF16)

    acc = _dot(h_ref[...], wbf_ref[...])
    c, s1, s2 = c_ref[...], s1_ref[...], s2_ref[...]
    for j in range(acc.shape[1] // LANES):
        sl = slice(j * LANES, (j + 1) * LANES)
        x = acc[:, sl]
        rot = x * c + pltpu.roll(x, LANES - 8, 1) * s1 + pltpu.roll(x, 8, 1) * s2
        o_ref[:, sl] = jnp.where(flag_ref[:, sl] > 0.0, rot, x)


def _inproj(h, w_all, layer, flag, tabs, tm, tn=512):
    m, k = h.shape
    n = w_all.shape[2]
    tab_blocks = tabs[0].shape[0] // tm
    tab_spec = pl.BlockSpec((tm, LANES), lambda j, i: (i % tab_blocks, 0))
    return pl.pallas_call(
        _inproj_kernel,
        name="inproj",
        grid=(n // tn, m // tm),
        in_specs=[
            pl.BlockSpec((tm, k), lambda j, i: (i, 0)),
            pl.BlockSpec((None, k, tn), lambda j, i: (layer, 0, j)),
            pl.BlockSpec((1, tn), lambda j, i: (0, j)),
            tab_spec, tab_spec, tab_spec,
        ],
        out_specs=pl.BlockSpec((tm, tn), lambda j, i: (i, j)),
        out_shape=jax.ShapeDtypeStruct((m, n), F32),
        scratch_shapes=[pltpu.VMEM((k, tn), BF16)],
        compiler_params=_cparams(2),
    )(h, w_all, flag, *tabs)


def _outproj_kernel(x_ref, oa_ref, ob_ref, oc_ref, w_ref, o_ref, wbf_ref):
    @pl.when(pl.program_id(1) == 0)
    def _():
        wbf_ref[...] = w_ref[...].astype(BF16)

    acc = _dot(oa_ref[...].astype(BF16), wbf_ref[0:A_DIM, :])
    acc += _dot(ob_ref[...].astype(BF16), wbf_ref[A_DIM:A_DIM + B_DIM, :])
    acc += _dot(oc_ref[...].astype(BF16), wbf_ref[A_DIM + B_DIM:, :])
    o_ref[...] = x_ref[...] + acc


def _outproj(x, oa, ob, oc, w_all, layer, tm, tn=512):
    m, d = x.shape
    k = w_all.shape[1]
    return pl.pallas_call(
        _outproj_kernel,
        name="outproj",
        grid=(d // tn, m // tm),
        in_specs=[
            pl.BlockSpec((tm, tn), lambda j, i: (i, j)),
            pl.BlockSpec((tm, A_DIM), lambda j, i: (i, 0)),
            pl.BlockSpec((tm, B_DIM), lambda j, i: (i, 0)),
            pl.BlockSpec((tm, C_DIM), lambda j, i: (i, 0)),
            pl.BlockSpec((None, k, tn), lambda j, i: (layer, 0, j)),
        ],
        out_specs=pl.BlockSpec((tm, tn), lambda j, i: (i, j)),
        out_shape=jax.ShapeDtypeStruct((m, d), F32),
        scratch_shapes=[pltpu.VMEM((k, tn), BF16)],
        compiler_params=_cparams(2),
    )(x, oa, ob, oc, w_all)


def _ffn_up_kernel(h_ref, wg_ref, wu_ref, o_ref, wgbf_ref, wubf_ref):
    @pl.when(pl.program_id(1) == 0)
    def _():
        wgbf_ref[...] = wg_ref[...].astype(BF16)
        wubf_ref[...] = wu_ref[...].astype(BF16)

    h = h_ref[...]
    gate = _dot(h, wgbf_ref[...])
    up = _dot(h, wubf_ref[...])
    o_ref[...] = (gate * _sigmoid(gate) * up).astype(o_ref.dtype)


def _ffn_up(h, wg_all, wu_all, layer, tm, tn=512):
    m, k = h.shape
    n = wg_all.shape[2]
    w_spec = pl.BlockSpec((None, k, tn), lambda j, i: (layer, 0, j))
    return pl.pallas_call(
        _ffn_up_kernel,
        name="ffn_up",
        grid=(n // tn, m // tm),
        in_specs=[pl.BlockSpec((tm, k), lambda j, i: (i, 0)), w_spec, w_spec],
        out_specs=pl.BlockSpec((tm, tn), lambda j, i: (i, j)),
        out_shape=jax.ShapeDtypeStruct((m, n), BF16),
        scratch_shapes=[pltpu.VMEM((k, tn), BF16), pltpu.VMEM((k, tn), BF16)],
        compiler_params=_cparams(2),
    )(h, wg_all, wu_all)


def _ffn_down_kernel(x_ref, a_ref, w_ref, o_ref, wbf_ref):
    @pl.when(pl.program_id(1) == 0)
    def _():
        wbf_ref[...] = w_ref[...].astype(BF16)

    o_ref[...] = x_ref[...] + _dot(a_ref[...], wbf_ref[...])


def _ffn_down(x, act, w_all, layer, tm, tn=512):
    m, d = x.shape
    k = w_all.shape[1]
    return pl.pallas_call(
        _ffn_down_kernel,
        name="ffn_down",
        grid=(d // tn, m // tm),
        in_specs=[
            pl.BlockSpec((tm, tn), lambda j, i: (i, j)),
            pl.BlockSpec((tm, k), lambda j, i: (i, 0)),
            pl.BlockSpec((None, k, tn), lambda j, i: (layer, 0, j)),
        ],
        out_specs=pl.BlockSpec((tm, tn), lambda j, i: (i, j)),
        out_shape=jax.ShapeDtypeStruct((m, d), F32),
        scratch_shapes=[pltpu.VMEM((k, tn), BF16)],
        compiler_params=_cparams(2),
    )(x, act, w_all)


def _band_mask(window, has_prev, n_keys):
    qi = lax.broadcasted_iota(jnp.int32, (BLOCK, n_keys), 0) + (n_keys - BLOCK)
    kj = lax.broadcasted_iota(jnp.int32, (BLOCK, n_keys), 1)
    dist = qi - kj
    band = (dist >= 0) & (dist <= window)
    if n_keys > BLOCK:
        band = band & ((kj >= n_keys - BLOCK) | has_prev)
    return band


def _head_attn(q2, k2, v2, lane_sel, mask):
    qm = jnp.where(lane_sel, q2, 0.0).astype(BF16)
    s = lax.dot_general(qm, k2, _NT, preferred_element_type=F32) * ATTN_SCALE
    s = jnp.where(mask, s, NEG)
    m = jnp.max(s, axis=-1, keepdims=True)
    p = jnp.exp(s - m)
    l = jnp.sum(p, axis=-1, keepdims=True)
    o = _dot(p.astype(BF16), v2)
    return o, m, l


def _attn_a_kernel(*refs, window, dil, pairs, with_prev, chained):
    refs = list(refs)
    q_ref = refs.pop(0)
    kp_ref = refs.pop(0) if with_prev else None
    kc_ref = refs.pop(0)
    vp_ref = refs.pop(0) if with_prev else None
    vc_ref = refs.pop(0)
    op_ref, lp_ref = (refs.pop(0), refs.pop(0)) if chained else (None, None)
    o_ref, l_ref = refs
    n_keys = 2 * BLOCK if with_prev else BLOCK
    mask = _band_mask(window, pl.program_id(1) > 0, n_keys)
    lane = lax.broadcasted_iota(jnp.int32, (BLOCK, LANES), 1)
    lane_lo = lane < HEAD
    for r in range(dil):
        rows = pl.ds(r, BLOCK, stride=dil) if dil > 1 else slice(None)
        for pp in range(pairs):
            sl = slice(pp * LANES, (pp + 1) * LANES)
            q2 = q_ref[rows, sl]
            if with_prev:
                k2 = jnp.concatenate([kp_ref[rows, sl], kc_ref[rows, sl]], axis=0).astype(BF16)
                v2 = jnp.concatenate([vp_ref[rows, sl], vc_ref[rows, sl]], axis=0).astype(BF16)
            else:
                k2 = kc_ref[rows, sl].astype(BF16)
                v2 = vc_ref[rows, sl].astype(BF16)
            if chained:
                o_prev = op_ref[rows, sl]
                l_prev = lp_ref[rows, sl]
            halves = []
            lse_blk = jnp.zeros((BLOCK, LANES), F32)
            for hh in range(2):
                o, m, l = _head_attn(q2, k2, v2, lane_lo if hh == 0 else ~lane_lo, mask)
                o = o / l
                lse = m + jnp.log(l)
                if chained:
                    lse_p = jnp.sum(jnp.where(lane == hh, l_prev, 0.0), axis=-1, keepdims=True)
                    mx = jnp.maximum(lse_p, lse)
                    wp = jnp.exp(lse_p - mx)
                    wi = jnp.exp(lse - mx)
                    den = wp + wi
                    o = (o_prev * wp + o * wi) / den
                    lse = mx + jnp.log(den)
                halves.append(o)
                lse_blk = jnp.where(lane == hh, lse, lse_blk)
            o_ref[rows, sl] = jnp.where(lane_lo, halves[0], halves[1])
            l_ref[rows, sl] = lse_blk


def _attn_a_branch(p2d, o_prev, l_prev, batch, seq, window, dil):
    rows = batch * seq
    tok = dil * BLOCK
    nblk = seq // tok
    with_prev = nblk > 1
    pairs = 1 if dil > 1 else A_HEADS // 2
    width = pairs * LANES
    n_col = A_DIM // width
    chained = o_prev is not None

    def cur(off):
        return lambda b, n, c: (b * nblk + n, off // width + c)

    def prev(off):
        return lambda b, n, c: (b * nblk + jnp.maximum(n - 1, 0), off // width + c)

    blk = (tok, width)
    in_specs = [pl.BlockSpec(blk, cur(OFF_QA))]
    in_specs += [pl.BlockSpec(blk, prev(OFF_KA))] if with_prev else []
    in_specs += [pl.BlockSpec(blk, cur(OFF_KA))]
    in_specs += [pl.BlockSpec(blk, prev(OFF_VA))] if with_prev else []
    in_specs += [pl.BlockSpec(blk, cur(OFF_VA))]
    args = [p2d] * len(in_specs)
    o_spec = pl.BlockSpec(blk, cur(0))
    if chained:
        in_specs += [o_spec, o_spec]
        args += [o_prev, l_prev]
    o, l = pl.pallas_call(
        functools.partial(_attn_a_kernel, window=window // dil, dil=dil, pairs=pairs, with_prev=with_prev,
                          chained=chained),
        grid=(batch, nblk, n_col),
        in_specs=in_specs,
        out_specs=[o_spec, o_spec],
        out_shape=[jax.ShapeDtypeStruct((rows, A_DIM), F32), jax.ShapeDtypeStruct((rows, A_DIM), F32)],
        compiler_params=_cparams(3),
        name=f"attn_a_dil{dil}",
    )(*args)
    return o, l


def _attn_c_kernel(sink_ref, q0_ref, q1_ref, q2_ref, q3_ref, kp_ref, kc_ref, vp_ref, vc_ref, o_ref):
    q_refs = (q0_ref, q1_ref, q2_ref, q3_ref)
    has_prev = pl.program_id(1) > 0
    mask = _band_mask(C_WINDOW, has_prev, 2 * BLOCK)
    lane = lax.broadcasted_iota(jnp.int32, (BLOCK, LANES), 1)
    lane_lo = lane < HEAD
    lane_lo2 = lax.broadcasted_iota(jnp.int32, (2 * BLOCK, LANES), 1) < HEAD
    k2 = jnp.concatenate([kp_ref[...], kc_ref[...]], axis=0)
    v2 = jnp.concatenate([vp_ref[...], vc_ref[...]], axis=0)
    k2r = pltpu.roll(k2, HEAD, 1)
    v2r = pltpu.roll(v2, HEAD, 1)
    kdup = [jnp.where(lane_lo2, k2, k2r).astype(BF16), jnp.where(lane_lo2, k2r, k2).astype(BF16)]
    vdup = [jnp.where(lane_lo2, v2, v2r).astype(BF16), jnp.where(lane_lo2, v2r, v2).astype(BF16)]
    for pr in range(C_Q_HEADS // 2):
        g = (2 * pr) // C_GROUP
        sl = slice(pr * LANES, (pr + 1) * LANES)
        q2 = q_refs[pr // 2][:, (pr % 2) * LANES:(pr % 2 + 1) * LANES]
        halves = []
        for hh in range(2):
            o, m, l = _head_attn(q2, kdup[g], vdup[g], lane_lo if hh == 0 else ~lane_lo, mask)
            lse = m + jnp.log(l)
            halves.append(o * (_sigmoid(lse - sink_ref[2 * pr + hh]) / l))
        o_ref[:, sl] = jnp.where(lane_lo, halves[0], halves[1])


def _attn_c_prompt(p2d, sink, batch, seq):
    rows = batch * seq
    nb = seq // BLOCK

    def cur(col):
        return lambda b, n: (b * nb + n, col)

    def prev(col):
        return lambda b, n: (b * nb + jnp.maximum(n - 1, 0), col)

    kv_blk = (BLOCK, C_KV_DIM)
    return pl.pallas_call(
        _attn_c_kernel,
        name="attn_c",
        grid=(batch, nb),
        in_specs=[
            pl.BlockSpec(memory_space=pltpu.SMEM),
            *[pl.BlockSpec((BLOCK, QC_BLK), cur(OFF_QC // QC_BLK + i)) for i in range(C_DIM // QC_BLK)],
            pl.BlockSpec(kv_blk, prev(OFF_KC // C_KV_DIM)),
            pl.BlockSpec(kv_blk, cur(OFF_KC // C_KV_DIM)),
            pl.BlockSpec(kv_blk, prev(OFF_VC // C_KV_DIM)),
            pl.BlockSpec(kv_blk, cur(OFF_VC // C_KV_DIM)),
        ],
        out_specs=pl.BlockSpec((BLOCK, C_DIM), lambda b, n: (b * nb + n, 0)),
        out_shape=jax.ShapeDtypeStruct((rows, C_DIM), F32),
        compiler_params=_cparams(2),
    )(sink, *([p2d] * (C_DIM // QC_BLK)), p2d, p2d, p2d, p2d)


def _attn_a_sample_kernel(q_ref, kn_ref, vn_ref, kc_ref, vc_ref, cc_ref, cn_ref, o_ref):
    t = q_ref.shape[0]
    pad = jnp.zeros((BLOCK - t, A_DIM), F32)
    k_new = jnp.concatenate([kn_ref[...], pad], axis=0)
    v_new = jnp.concatenate([vn_ref[...], pad], axis=0)
    cnt_c, cnt_n = cc_ref[...], cn_ref[...]
    outs = []
    for h in range(A_HEADS):
        lanes = slice(h * HEAD, (h + 1) * HEAD)
        q = q_ref[:, lanes].astype(BF16)
        s_c = _dot(q, kc_ref[h].astype(BF16)) * ATTN_SCALE
        s_n = lax.dot_general(q, k_new[:, lanes].astype(BF16), _NT, preferred_element_type=F32) * ATTN_SCALE
        s_c = jnp.where(cnt_c > 0.0, s_c, NEG)
        s_n = jnp.where(cnt_n > 0.0, s_n, NEG)
        m = jnp.maximum(jnp.max(s_c, axis=-1, keepdims=True), jnp.max(s_n, axis=-1, keepdims=True))
        p_c = cnt_c * jnp.exp(s_c - m)
        p_n = cnt_n * jnp.exp(s_n - m)
        l = jnp.sum(p_c, axis=-1, keepdims=True) + jnp.sum(p_n, axis=-1, keepdims=True)
        o = lax.dot_general(p_c.astype(BF16), vc_ref[h].astype(BF16), _NT, preferred_element_type=F32)
        o += _dot(p_n.astype(BF16), v_new[:, lanes].astype(BF16))
        outs.append(o / l)
    o_ref[...] = jnp.concatenate(outs, axis=-1)


def _a_sample_counts(t, n_buf):
    qi = n_buf + np.arange(t)[:, None]

    def count(rows):
        delta = qi - rows[None, :]
        c = np.zeros(delta.shape, np.float32)
        for window, dil in A_BRANCHES:
            c += ((delta >= 0) & (delta <= window) & (delta % dil == 0)).astype(np.float32)
        return c

    return count(np.arange(n_buf)), count(n_buf + np.arange(BLOCK))


def _attn_a_sample(ps, cache_k, cache_v, layer, batch, t):
    n_buf = cache_k.shape[4]
    cnt_c, cnt_n = _a_sample_counts(t, n_buf)
    new_blk = (t, A_DIM)
    cache_spec = pl.BlockSpec((None, None, A_HEADS, HEAD, n_buf), lambda b: (layer, b, 0, 0, 0))
    return pl.pallas_call(
        _attn_a_sample_kernel,
        name="attn_a_sample",
        grid=(batch,),
        in_specs=[
            pl.BlockSpec(new_blk, lambda b: (b, OFF_QA // A_DIM)),
            pl.BlockSpec(new_blk, lambda b: (b, OFF_KA // A_DIM)),
            pl.BlockSpec(new_blk, lambda b: (b, OFF_VA // A_DIM)),
            cache_spec, cache_spec,
            pl.BlockSpec(cnt_c.shape, lambda b: (0, 0)),
            pl.BlockSpec(cnt_n.shape, lambda b: (0, 0)),
        ],
        out_specs=pl.BlockSpec(new_blk, lambda b: (b, 0)),
        out_shape=jax.ShapeDtypeStruct((batch * t, A_DIM), F32),
        compiler_params=_cparams(1),
    )(ps, ps, ps, cache_k, cache_v, jnp.asarray(cnt_c), jnp.asarray(cnt_n))


def _attn_c_sample_kernel(q0_ref, q1_ref, q2_ref, q3_ref, kn_ref, vn_ref, kc_ref, vc_ref, sink_ref, o_ref):
    q_refs = (q0_ref, q1_ref, q2_ref, q3_ref)
    t = q0_ref.shape[0]
    n_buf = kc_ref.shape[1]
    rows = C_Q_HEADS * t
    lane_lo = lax.broadcasted_iota(jnp.int32, (t, LANES), 1) < HEAD
    blocks = []
    for j in range(C_Q_HEADS // 2):
        chunk = q_refs[j // 2][:, (j % 2) * LANES:(j % 2 + 1) * LANES]
        rolled = pltpu.roll(chunk, HEAD, 1)
        if (2 * j) // C_GROUP == 0:
            blocks += [jnp.where(lane_lo, chunk, 0.0), jnp.where(lane_lo, rolled, 0.0)]
        else:
            blocks += [jnp.where(lane_lo, 0.0, rolled), jnp.where(lane_lo, 0.0, chunk)]
    qbd = jnp.concatenate(blocks, axis=0).astype(BF16)
    pad = jnp.zeros((BLOCK - t, C_KV_DIM), F32)
    k_new = jnp.concatenate([kn_ref[...], pad], axis=0).astype(BF16)
    v_new = jnp.concatenate([vn_ref[...], pad], axis=0).astype(BF16)
    s_c = _dot(qbd, kc_ref[...].astype(BF16)) * ATTN_SCALE
    s_n = lax.dot_general(qbd, k_new, _NT, preferred_element_type=F32) * ATTN_SCALE
    qt = lax.broadcasted_iota(jnp.int32, (rows, BLOCK), 0) % t
    kj = lax.broadcasted_iota(jnp.int32, (rows, BLOCK), 1)
    dist_c = n_buf + qt - kj
    s_c = jnp.where((dist_c >= 0) & (dist_c <= C_WINDOW), s_c, NEG)
    s_n = jnp.where(kj <= qt, s_n, NEG)
    m = jnp.maximum(jnp.max(s_c, axis=-1, keepdims=True), jnp.max(s_n, axis=-1, keepdims=True))
    p_c = jnp.exp(s_c - m)
    p_n = jnp.exp(s_n - m)
    l = jnp.sum(p_c, axis=-1, keepdims=True) + jnp.sum(p_n, axis=-1, keepdims=True)
    o = lax.dot_general(p_c.astype(BF16), vc_ref[...].astype(BF16), _NT, preferred_element_type=F32)
    o += _dot(p_n.astype(BF16), v_new)
    lse = m + jnp.log(l)
    o = o * (_sigmoid(lse - sink_ref[...]) / l)
    for j in range(C_Q_HEADS // 2):
        blk_a = o[2 * j * t:(2 * j + 1) * t, :]
        blk_b = o[(2 * j + 1) * t:(2 * j + 2) * t, :]
        if (2 * j) // C_GROUP == 0:
            out = jnp.where(lane_lo, blk_a, pltpu.roll(blk_b, HEAD, 1))
        else:
            out = jnp.where(lane_lo, pltpu.roll(blk_a, HEAD, 1), blk_b)
        o_ref[:, j * LANES:(j + 1) * LANES] = out


def _attn_c_sample(ps, cache_k, cache_v, layer, sink_col, batch, t):
    n_buf = cache_k.shape[3]
    assert n_buf == BLOCK
    kv_blk = (t, C_KV_DIM)
    cache_spec = pl.BlockSpec((None, None, C_KV_DIM, n_buf), lambda b: (layer, b, 0, 0))
    return pl.pallas_call(
        _attn_c_sample_kernel,
        name="attn_c_sample",
        grid=(batch,),
        in_specs=[
            *[pl.BlockSpec((t, QC_BLK), functools.partial(lambda b, i: (b, OFF_QC // QC_BLK + i), i=i))
              for i in range(C_DIM // QC_BLK)],
            pl.BlockSpec(kv_blk, lambda b: (b, OFF_KC // C_KV_DIM)),
            pl.BlockSpec(kv_blk, lambda b: (b, OFF_VC // C_KV_DIM)),
            cache_spec, cache_spec,
            pl.BlockSpec(sink_col.shape, lambda b: (0, 0)),
        ],
        out_specs=pl.BlockSpec((t, C_DIM), lambda b: (b, 0)),
        out_shape=jax.ShapeDtypeStruct((batch * t, C_DIM), F32),
        compiler_params=_cparams(1),
    )(*([ps] * (C_DIM // QC_BLK)), ps, ps, cache_k, cache_v, sink_col)


def _rwkv_kernel(r_ref, k_ref, v_ref, lo_ref, shift_ref, s0_ref, mu_ref, vec_ref, w2_ref, a2_ref, g2_ref,
                 o_ref, sout_ref,
                 s_scr, r_scr, d_scr, k_scr, v_scr, kk_scr, b_scr, g_scr, y_scr, dr_scr, br_scr, kr_scr,
                 cx_scr, cl_scr, *, nb, tc):
    c = pl.program_id(1)
    n_pairs = B_HEADS // 2
    f32 = F32

    li = lax.broadcasted_iota(jnp.int32, (LANES, LANES), 0) // HEAD
    lj = lax.broadcasted_iota(jnp.int32, (LANES, LANES), 1) // HEAD
    bd = (li == lj).astype(BF16)

    @pl.when(c == 0)
    def _():
        for b in range(nb):
            for p in range(n_pairs):
                s_scr[b * n_pairs + p] = jnp.concatenate([s0_ref[b, 2 * p], s0_ref[b, 2 * p + 1]], axis=-1)
            cx_scr[b] = jnp.broadcast_to(shift_ref[0, b:b + 1, 0:3 * B_DIM], (SUBLANES, 3 * B_DIM))
            cl_scr[b] = jnp.broadcast_to(shift_ref[0, b:b + 1, 3 * B_DIM:], (SUBLANES, LORA_COLS))

    row0 = lax.broadcasted_iota(jnp.int32, (tc, 1), 0) == 0

    def lerp(cur, prev_row, mu):
        sh = cur if tc == 1 else pltpu.roll(cur, 1, 0)
        sh = jnp.where(row0, prev_row, sh)
        return cur + (sh - cur) * mu

    w0, a0, k_k, k_a = vec_ref[0:1, :], vec_ref[1:2, :], vec_ref[2:3, :], vec_ref[3:4, :]
    r_k, lnx_w, lnx_b = vec_ref[4:5, :], vec_ref[5:6, :], vec_ref[6:7, :]

    for b in range(nb):
        r_raw, k_raw, v_raw, lo_raw = r_ref[b], k_ref[b], v_ref[b], lo_ref[b]
        r = lerp(r_raw, cx_scr[b, 0:1, 0:B_DIM], mu_ref[:, 0:B_DIM])
        k = lerp(k_raw, cx_scr[b, 0:1, B_DIM:2 * B_DIM], mu_ref[:, B_DIM:2 * B_DIM])
        v = lerp(v_raw, cx_scr[b, 0:1, 2 * B_DIM:], mu_ref[:, 2 * B_DIM:3 * B_DIM])
        lo = lerp(lo_raw, cl_scr[b, 0:1, :], mu_ref[:, 3 * B_DIM:])
        cx_scr[b, :, 0:B_DIM] = jnp.broadcast_to(r_raw[tc - 1:tc, :], (SUBLANES, B_DIM))
        cx_scr[b, :, B_DIM:2 * B_DIM] = jnp.broadcast_to(k_raw[tc - 1:tc, :], (SUBLANES, B_DIM))
        cx_scr[b, :, 2 * B_DIM:] = jnp.broadcast_to(v_raw[tc - 1:tc, :], (SUBLANES, B_DIM))
        cl_scr[b] = jnp.broadcast_to(lo_raw[tc - 1:tc, :], (SUBLANES, LORA_COLS))

        z = w0 + _dot_hi(jnp.tanh(lo), w2_ref[...])
        sp = jnp.maximum(-z, 0.0) + jnp.log(1.0 + jnp.exp(-jnp.abs(z)))
        decay = jnp.exp(-jnp.exp(-sp - 0.5))
        a = _sigmoid(a0 + _dot_hi(lo, a2_ref[...]))
        g = _dot_hi(_sigmoid(lo), g2_ref[...])
        kkr = k * k_k
        sq = kkr * kkr
        ss = jnp.concatenate([_seg_sum(sq[:, j * LANES:(j + 1) * LANES], bd) for j in range(n_pairs)], axis=-1)
        kk = kkr / jnp.maximum(jnp.sqrt(ss), 1e-12)
        k = k * (1.0 + (a - 1.0) * k_a)
        bb = kk * a

        def head_dot(x):
            return jnp.concatenate([_seg_sum(x[:, j * LANES:(j + 1) * LANES], bd) for j in range(n_pairs)], axis=-1)

        r_scr[b] = r
        d_scr[b] = decay
        k_scr[b] = k
        v_scr[b] = v
        kk_scr[b] = kk
        b_scr[b] = bb
        g_scr[b] = g
        dr_scr[b] = decay * r
        br_scr[b] = head_dot(bb * r)
        kr_scr[b] = head_dot(k * r)

    sub = lax.broadcasted_iota(jnp.int32, (HEAD, LANES), 0)
    lane = lax.broadcasted_iota(jnp.int32, (HEAD, LANES), 1)
    diag = (lane % HEAD) == sub
    sub8 = lax.broadcasted_iota(jnp.int32, (SUBLANES, LANES), 0)
    li2 = lax.broadcasted_iota(jnp.int32, (2 * LANES, 2 * LANES), 0) // HEAD
    lj2 = lax.broadcasted_iota(jnp.int32, (2 * LANES, 2 * LANES), 1) // HEAD
    bd2 = (li2 == lj2).astype(BF16)
    n_all = nb * n_pairs
    grp = min(n_all, RWKV_MATMUL_PAIRS)
    groups = [list(range(g0, g0 + grp)) for g0 in range(0, n_all, grp)]

    def step8(t8, carry):
        t0 = pl.multiple_of(t8 * SUBLANES, SUBLANES)

        def rows_of(scr):
            return [scr[i // n_pairs, pl.ds(t0, SUBLANES), (i % n_pairs) * LANES:(i % n_pairs + 1) * LANES]
                    for i in range(n_all)]

        kk8, v8, d8, b8, k8, dr8, br8, kr8 = [
            rows_of(scr) for scr in (kk_scr, v_scr, d_scr, b_scr, k_scr, dr_scr, br_scr, kr_scr)]
        st = [s_scr[i] for i in range(n_all)]
        yb = [jnp.zeros((SUBLANES, LANES), f32) for _ in range(n_all)]

        vb = {}
        for grp_ids in groups:
            lhs = jnp.concatenate(
                [jnp.concatenate([jnp.where(diag, v8[i][2 * m:2 * m + 1, :], 0.0),
                                  jnp.where(diag, v8[i][2 * m + 1:2 * m + 2, :], 0.0)], axis=1).astype(BF16)
                 for m in range(SUBLANES // 2) for i in grp_ids], axis=0)
            res = _dot(lhs, bd2)
            for m in range(SUBLANES // 2):
                for q, i in enumerate(grp_ids):
                    blk = res[(m * len(grp_ids) + q) * HEAD:(m * len(grp_ids) + q + 1) * HEAD, :]
                    vb[(i, 2 * m)] = blk[:, 0:LANES]
                    vb[(i, 2 * m + 1)] = blk[:, LANES:]

        for j in range(SUBLANES):
            for grp_ids in groups:
                lhs = jnp.concatenate(
                    [jnp.concatenate([st[i] * kk8[i][j:j + 1, :], st[i] * dr8[i][j:j + 1, :]], axis=1).astype(BF16)
                     for i in grp_ids], axis=0)
                res = _dot(lhs, bd2)
                for q, i in enumerate(grp_ids):
                    sa = res[q * HEAD:(q + 1) * HEAD, 0:LANES]
                    u = res[q * HEAD:(q + 1) * HEAD, LANES:]
                    v_b = vb[(i, j)]
                    y_b = u - sa * br8[i][j:j + 1, :] + v_b * kr8[i][j:j + 1, :]
                    y_row = jnp.sum(jnp.where(diag, y_b, 0.0), axis=0, keepdims=True)
                    yb[i] = jnp.where(sub8 == j, y_row, yb[i])
                    st[i] = st[i] * d8[i][j:j + 1, :] - sa * b8[i][j:j + 1, :] + v_b * k8[i][j:j + 1, :]

        for i in range(n_all):
            s_scr[i] = st[i]
            y_scr[i // n_pairs, pl.ds(t0, SUBLANES), (i % n_pairs) * LANES:(i % n_pairs + 1) * LANES] = yb[i]
        return carry

    lax.fori_loop(0, tc // SUBLANES, step8, 0)

    for b in range(nb):
        outs = []
        for j in range(n_pairs):
            sl = slice(j * LANES, (j + 1) * LANES)
            y = y_scr[b, :, sl]
            mean = _seg_sum(y, bd) * (1.0 / HEAD)
            yc = y - mean
            var = _seg_sum(yc * yc, bd) * (1.0 / HEAD)
            yn = yc * lax.rsqrt(var + GN_EPS) * lnx_w[:, sl] + lnx_b[:, sl]
            bonus = _seg_sum(r_scr[b, :, sl] * k_scr[b, :, sl] * r_k[:, sl], bd) * v_scr[b, :, sl]
            outs.append((yn + bonus) * g_scr[b, :, sl])
        o_ref[b] = jnp.concatenate(outs, axis=-1)

    @pl.when(c == pl.num_programs(1) - 1)
    def _():
        for b in range(nb):
            for p in range(n_pairs):
                s = s_scr[b * n_pairs + p]
                sout_ref[b, 2 * p] = s[:, 0:HEAD]
                sout_ref[b, 2 * p + 1] = s[:, HEAD:]


def _rwkv(p3d, shift0, s0, mu, vecs, w2p, a2p, g2p, nb, tc):
    batch, seq, _ = p3d.shape
    groups = batch // nb
    chunks = seq // tc
    x_blk = (nb, tc, B_DIM)

    def xmap(col):
        return lambda g, c: (g, c, col)

    const2 = lambda g, c: (0, 0)
    scr = lambda *shape: pltpu.VMEM(shape, F32)
    o, s_out = pl.pallas_call(
        functools.partial(_rwkv_kernel, nb=nb, tc=tc),
        name="rwkv7",
        grid=(groups, chunks),
        in_specs=[
            pl.BlockSpec(x_blk, xmap(OFF_PB // B_DIM)),
            pl.BlockSpec(x_blk, xmap(OFF_PB // B_DIM + 1)),
            pl.BlockSpec(x_blk, xmap(OFF_PB // B_DIM + 2)),
            pl.BlockSpec((nb, tc, LORA_COLS), xmap(OFF_LORA // LORA_COLS)),
            pl.BlockSpec((1, nb, B_COLS), lambda g, c: (g, 0, 0)),
            pl.BlockSpec((nb, B_HEADS, HEAD, HEAD), lambda g, c: (g, 0, 0, 0)),
            pl.BlockSpec((1, B_COLS), const2),
            pl.BlockSpec((SUBLANES, B_DIM), const2),
            pl.BlockSpec((LORA_COLS, B_DIM), const2),
            pl.BlockSpec((LORA_COLS, B_DIM), const2),
            pl.BlockSpec((LORA_COLS, B_DIM), const2),
        ],
        out_specs=[
            pl.BlockSpec(x_blk, lambda g, c: (g, c, 0)),
            pl.BlockSpec((nb, B_HEADS, HEAD, HEAD), lambda g, c: (g, 0, 0, 0)),
        ],
        out_shape=[jax.ShapeDtypeStruct((batch, seq, B_DIM), F32),
                   jax.ShapeDtypeStruct((batch, B_HEADS, HEAD, HEAD), F32)],
        scratch_shapes=[scr(nb * B_HEADS // 2, HEAD, LANES)] + [scr(nb, tc, B_DIM)] * 11
        + [scr(nb, SUBLANES, 3 * B_DIM), scr(nb, SUBLANES, LORA_COLS)],
        compiler_params=_cparams(2),
    )(p3d, p3d, p3d, p3d, shift0.reshape(groups, nb, B_COLS), s0, mu, vecs, w2p, a2p, g2p)
    return o, s_out


def _rope_tables(pos):
    half = ROPE_DIM // 2
    inv = jnp.exp(-math.log(ROPE_THETA) * jnp.arange(half, dtype=F32) * 2.0 / ROPE_DIM)
    ang = pos.astype(F32)[:, None] * inv[None, :]
    cos, sin = jnp.cos(ang), jnp.sin(ang)
    lm = np.arange(LANES) % HEAD
    first = jnp.asarray(lm < half)[None, :]
    second = jnp.asarray((lm >= half) & (lm < ROPE_DIM))[None, :]
    freq = np.where(lm < half, lm, np.where(lm < ROPE_DIM, lm - half, 0))
    cos_l, sin_l = cos[:, freq], sin[:, freq]
    c = jnp.where(first | second, cos_l, 1.0)
    s1 = jnp.where(first, -sin_l, 0.0)
    s2 = jnp.where(second, sin_l, 0.0)
    return c, s1, s2


def _rope_flag():
    col = np.arange(IN_COLS)
    rope = (col < OFF_VA) | ((col >= OFF_QC) & (col < OFF_VC))
    return jnp.asarray(rope.astype(np.float32))[None, :]


def _pad_rows(w, start):
    return jnp.zeros((LORA_COLS, B_DIM), F32).at[start:start + w.shape[0]].set(w)


def _mixers(p2d, batch, seq, layer, is_prompt, cache, rwkv_w, sink):
    p3d = p2d.reshape(batch, seq, IN_COLS)
    mu, vecs, w2p, a2p, g2p = rwkv_w
    if is_prompt:
        o, l = None, None
        for window, dil in A_BRANCHES:
            o, l = _attn_a_branch(p2d, o, l, batch, seq, window, dil)
        oa = o
        oc = _attn_c_prompt(p2d, sink, batch, seq)
        shift0 = jnp.zeros((batch, B_COLS), F32)
        s0 = jnp.zeros((batch, B_HEADS, HEAD, HEAD), F32)
        ob, wkv = _rwkv(p3d, shift0, s0, mu, vecs, w2p, a2p, g2p, nb=batch, tc=min(seq, 128))
    else:
        a_k, a_v, c_k, c_v, wkv0, shift0 = cache
        oa = _attn_a_sample(p2d, a_k, a_v, layer, batch, seq)
        sink_col = jnp.repeat(sink, seq)[:, None]
        oc = _attn_c_sample(p2d, c_k, c_v, layer, sink_col, batch, seq)
        ob, wkv = _rwkv(p3d, shift0[layer], wkv0[layer], mu, vecs, w2p, a2p, g2p, nb=4, tc=seq)
    return oa, ob.reshape(batch * seq, B_DIM), oc, wkv


def kernel(x_prompt, x_sample, cache_a_k, cache_a_v, cache_c_k, cache_c_v, state_b_wkv, state_b_shift, g_mix, w_in, w_out, b_mu, b_w0, b_w2, b_a0, b_a2, b_g2, b_k_k, b_k_a, b_r_k, b_lnx_w, b_lnx_b, c_sink, g_ffn, w_gate, w_up, w_down, g_final):
    depth = w_in.shape[0]
    bp, lp, d = x_prompt.shape
    bs, ls, _ = x_sample.shape
    a_win = cache_a_k.shape[2]
    c_win = cache_c_k.shape[2]
    assert lp % (16 * BLOCK) == 0 and a_win >= A_BRANCHES[-1][0] and c_win == C_WINDOW and bs % 4 == 0

    flag = _rope_flag()
    tabs_p = _rope_tables(jnp.arange(lp, dtype=jnp.int32))
    tabs_s = _rope_tables(jnp.tile(PAST_LEN + jnp.arange(ls, dtype=jnp.int32), bs))
    cak = jnp.transpose(cache_a_k, (0, 1, 3, 4, 2))
    cav = jnp.transpose(cache_a_v, (0, 1, 3, 4, 2))
    cck = jnp.transpose(cache_c_k, (0, 1, 3, 4, 2)).reshape(depth, bs, C_KV_DIM, c_win)
    ccv = jnp.transpose(cache_c_v, (0, 1, 3, 4, 2)).reshape(depth, bs, C_KV_DIM, c_win)
    cache = (cak, cav, cck, ccv, state_b_wkv, state_b_shift)

    xp = x_prompt.reshape(bp * lp, d)
    xs = x_sample.reshape(bs * ls, d)
    tm_p, tm_s = 1024, bs * ls
    new_p = [[] for _ in range(6)]
    new_s = [[] for _ in range(6)]
    for l in range(depth):
        vecs = jnp.stack([b_w0[l], b_a0[l], b_k_k[l], b_k_a[l], b_r_k[l], b_lnx_w[l], b_lnx_b[l],
                          jnp.zeros((B_DIM,), F32)], axis=0)
        rwkv_w = (b_mu[l][None, :], vecs, _pad_rows(b_w2[l], 0), _pad_rows(b_a2[l], 96), _pad_rows(b_g2[l], 192))
        sink = c_sink[l].reshape(C_Q_HEADS)
        streams = []
        for x, batch, seq, tm, tabs, is_prompt in ((xp, bp, lp, tm_p, tabs_p, True), (xs, bs, ls, tm_s, tabs_s, False)):
            h = _rmsnorm(x, g_mix[l], BF16)
            p2d = _inproj(h, w_in, l, flag, tabs, tm)
            oa, ob, oc, wkv = _mixers(p2d, batch, seq, l, is_prompt, cache, rwkv_w, sink)
            x = _outproj(x, oa, ob, oc, w_out, l, tm)
            h = _rmsnorm(x, g_ffn[l], BF16)
            act = _ffn_up(h, w_gate, w_up, l, tm)
            x = _ffn_down(x, act, w_down, l, min(tm, 256))
            p3d = p2d.reshape(batch, seq, IN_COLS)
            a_keep = min(a_win, seq) if is_prompt else seq
            c_keep = min(c_win, seq) if is_prompt else seq
            new = (
                p3d[:, seq - a_keep:, OFF_KA:OFF_VA].reshape(batch, a_keep, A_HEADS, HEAD),
                p3d[:, seq - a_keep:, OFF_VA:OFF_PB].reshape(batch, a_keep, A_HEADS, HEAD),
                p3d[:, seq - c_keep:, OFF_KC:OFF_VC].reshape(batch, c_keep, C_KV_DIM // HEAD, HEAD),
                p3d[:, seq - c_keep:, OFF_VC:].reshape(batch, c_keep, C_KV_DIM // HEAD, HEAD),
                wkv,
                p3d[:, -1, OFF_PB:OFF_QC],
            )
            streams.append((x, new))
        (xp, st_p), (xs, st_s) = streams
        for i in range(6):
            new_p[i].append(st_p[i])
            new_s[i].append(st_s[i])
    y_prompt = _rmsnorm(xp, g_final, F32).reshape(bp, lp, d)
    y_sample = _rmsnorm(xs, g_final, F32).reshape(bs, ls, d)
    outs_p = [jnp.stack(t, axis=0) for t in new_p]
    outs_s = [jnp.stack(t, axis=0) for t in new_s]
    return (y_prompt, y_sample, *outs_p, *outs_s)
```

```python
import functools
import math

import numpy as np
import jax
import jax.numpy as jnp
from jax import lax
from jax.experimental import pallas as pl
from jax.experimental.pallas import tpu as pltpu

F32 = jnp.float32
BF16 = jnp.bfloat16

LANES = 128
SUBLANES = 8
VMEM_LIMIT = 52 * 1024 * 1024

D_MODEL = 2048
HEAD = 64
A_DIM = 512
B_DIM = 512
C_DIM = 1024
C_KV_DIM = 128
A_HEADS = 8
B_HEADS = 8
C_Q_HEADS = 16
C_GROUP = 8
LORA_COLS = 256
B_COLS = 3 * B_DIM + LORA_COLS
IN_COLS = 3 * A_DIM + B_COLS + C_DIM + 2 * C_KV_DIM
D_FF = 5632
OFF_QA, OFF_KA, OFF_VA = 0, A_DIM, 2 * A_DIM
OFF_PB = 3 * A_DIM
OFF_LORA = OFF_PB + 3 * B_DIM
OFF_QC = OFF_PB + B_COLS
OFF_KC = OFF_QC + C_DIM
OFF_VC = OFF_KC + C_KV_DIM
A_BRANCHES = ((128, 1), (512, 4), (2048, 16))
C_WINDOW = 128
BLOCK = 128
QC_BLK = 256
IN_TN = 512
PROJ_TM = 1024
DOWN_TM = 512
STEP_PAIRS = 16
ATTN_GROUP = 4
RWKV_MATMUL_PAIRS = 8
PAST_LEN = 16384
ROPE_THETA = 500000.0
ROPE_DIM = 16
RMS_EPS = 1e-6
GN_EPS = 64e-5
ATTN_SCALE = HEAD ** -0.5
NEG = -1e30

_NT = (((1,), (1,)), ((), ()))


def _cparams(n_grid):
    return pltpu.CompilerParams(dimension_semantics=("arbitrary",) * n_grid, vmem_limit_bytes=VMEM_LIMIT)


def _dot(a, b):
    return jnp.dot(a, b, preferred_element_type=F32)


def _split_bf16(x):
    hi = x.astype(BF16)
    lo = (x - hi.astype(F32)).astype(BF16)
    return hi, lo


def _dot_hi(a, b):
    a_hi, a_lo = _split_bf16(a)
    b_hi, b_lo = _split_bf16(b)
    return _dot(a_hi, b_hi) + (_dot(a_lo, b_hi) + _dot(a_hi, b_lo))


def _seg_sum(x, bd):
    hi, lo = _split_bf16(x)
    return _dot(hi, bd) + _dot(lo, bd)


def _sigmoid(x):
    return 1.0 / (1.0 + jnp.exp(-x))


def _rmsnorm_kernel(x_ref, g_ref, o_ref):
    x = x_ref[...]
    ms = jnp.mean(x * x, axis=-1, keepdims=True)
    o_ref[...] = (x * lax.rsqrt(ms + RMS_EPS) * g_ref[...]).astype(o_ref.dtype)


def _rmsnorm(x, g, out_dtype):
    m, d = x.shape
    tm = min(m, 256)
    return pl.pallas_call(
        _rmsnorm_kernel,
        name="rmsnorm",
        grid=(m // tm,),
        in_specs=[pl.BlockSpec((tm, d), lambda i: (i, 0)), pl.BlockSpec((1, d), lambda i: (0, 0))],
        out_specs=pl.BlockSpec((tm, d), lambda i: (i, 0)),
        out_shape=jax.ShapeDtypeStruct((m, d), out_dtype),
        compiler_params=_cparams(1),
    )(x, g.reshape(1, d))


def _row_specs(tm_p, tm_s, n_p, width, tiled, cycle=None):
    col = (lambda j: j) if tiled else (lambda j: 0)
    row = (lambda i: jnp.minimum(i, n_p - 1)) if cycle is None else (lambda i: jnp.minimum(i, n_p - 1) % cycle)
    return (pl.BlockSpec((tm_p, width), lambda j, i: (row(i), col(j))),
            pl.BlockSpec((tm_s, width), lambda j, i: (0, col(j))))


def _dual(n_p, tile_fn, prompt_refs, sample_refs):
    i = pl.program_id(1)

    @pl.when(i < n_p)
    def _():
        tile_fn(*prompt_refs)

    @pl.when(i == n_p)
    def _():
        tile_fn(*sample_refs)


def _inproj_kernel(hp, cp, s1p, s2p, hs, cs, s1s, s2s, w_ref, flag_ref, op_ref, os_ref, wbf_ref, *, rope_tiles, n_p):
    @pl.when(pl.program_id(1) == 0)
    def _():
        wbf_ref[...] = w_ref[...].astype(BF16)

    tile = pl.program_id(0)
    has_rope = functools.reduce(jnp.logical_or, [tile == t for t in rope_tiles])

    def rows(h_ref, c_ref, s1_ref, s2_ref, o_ref):
        @pl.when(jnp.logical_not(has_rope))
        def _():
            o_ref[...] = _dot(h_ref[...], wbf_ref[...])

        @pl.when(has_rope)
        def _():
            c, s1, s2 = c_ref[...], s1_ref[...], s2_ref[...]
            half = 2 * LANES
            for h0 in range(0, o_ref.shape[1], half):
                acc = _dot(h_ref[...], wbf_ref[:, h0:h0 + half])
                for j in range(half // LANES):
                    sl = slice(h0 + j * LANES, h0 + (j + 1) * LANES)
                    x = acc[:, j * LANES:(j + 1) * LANES]
                    rot = x * c + pltpu.roll(x, LANES - 8, 1) * s1 + pltpu.roll(x, 8, 1) * s2
                    o_ref[:, sl] = jnp.where(flag_ref[:, sl] > 0.0, rot, x)

    _dual(n_p, rows, (hp, cp, s1p, s2p, op_ref), (hs, cs, s1s, s2s, os_ref))


def _inproj(h_p, h_s, w_all, layer, flag, rope_tiles, tabs_p, tabs_s, tm, tn=IN_TN):
    (m_p, k), m_s = h_p.shape, h_s.shape[0]
    n = w_all.shape[2]
    n_p = m_p // tm
    h_specs = _row_specs(tm, m_s, n_p, k, False)
    tab_specs = _row_specs(tm, m_s, n_p, LANES, False, cycle=tabs_p[0].shape[0] // tm)
    out_specs = _row_specs(tm, m_s, n_p, tn, True)
    return pl.pallas_call(
        functools.partial(_inproj_kernel, rope_tiles=rope_tiles, n_p=n_p),
        name="inproj",
        grid=(n // tn, n_p + 1),
        in_specs=[h_specs[0]] + [tab_specs[0]] * 3 + [h_specs[1]] + [tab_specs[1]] * 3 + [
            pl.BlockSpec((None, k, tn), lambda j, i: (layer, 0, j)),
            pl.BlockSpec((1, tn), lambda j, i: (0, j)),
        ],
        out_specs=list(out_specs),
        out_shape=[jax.ShapeDtypeStruct((m_p, n), F32), jax.ShapeDtypeStruct((m_s, n), F32)],
        scratch_shapes=[pltpu.VMEM((k, tn), BF16)],
        compiler_params=_cparams(2),
    )(h_p, *tabs_p, h_s, *tabs_s, w_all, flag)


def _outproj_kernel(xp, oap, obp, ocp, xs, oas, obs, ocs, w_ref, op_ref, os_ref, wbf_ref, *, n_p):
    @pl.when(pl.program_id(1) == 0)
    def _():
        wbf_ref[...] = w_ref[...].astype(BF16)

    def rows(x_ref, oa_ref, ob_ref, oc_ref, o_ref):
        acc = _dot(oa_ref[...].astype(BF16), wbf_ref[0:A_DIM, :])
        acc += _dot(ob_ref[...].astype(BF16), wbf_ref[A_DIM:A_DIM + B_DIM, :])
        acc += _dot(oc_ref[...].astype(BF16), wbf_ref[A_DIM + B_DIM:, :])
        o_ref[...] = x_ref[...] + acc

    _dual(n_p, rows, (xp, oap, obp, ocp, op_ref), (xs, oas, obs, ocs, os_ref))


def _outproj(x, oa, ob, oc, w_all, layer, tm, tn=512):
    (m_p, d), m_s = x[0].shape, x[1].shape[0]
    k = w_all.shape[1]
    n_p = m_p // tm
    specs = [_row_specs(tm, m_s, n_p, tn, True)] + [_row_specs(tm, m_s, n_p, w, False) for w in (A_DIM, B_DIM, C_DIM)]
    return pl.pallas_call(
        functools.partial(_outproj_kernel, n_p=n_p),
        name="outproj",
        grid=(d // tn, n_p + 1),
        in_specs=[s[0] for s in specs] + [s[1] for s in specs] + [
            pl.BlockSpec((None, k, tn), lambda j, i: (layer, 0, j))],
        out_specs=list(specs[0]),
        out_shape=[jax.ShapeDtypeStruct((m_p, d), F32), jax.ShapeDtypeStruct((m_s, d), F32)],
        scratch_shapes=[pltpu.VMEM((k, tn), BF16)],
        compiler_params=_cparams(2),
    )(x[0], oa[0], ob[0], oc[0], x[1], oa[1], ob[1], oc[1], w_all)


def _ffn_up_kernel(hp, hs, wg_ref, wu_ref, op_ref, os_ref, wgbf_ref, wubf_ref, *, n_p):
    @pl.when(pl.program_id(1) == 0)
    def _():
        wgbf_ref[...] = wg_ref[...].astype(BF16)
        wubf_ref[...] = wu_ref[...].astype(BF16)

    def rows(h_ref, o_ref):
        h = h_ref[...]
        gate = _dot(h, wgbf_ref[...])
        up = _dot(h, wubf_ref[...])
        o_ref[...] = (gate * _sigmoid(gate) * up).astype(o_ref.dtype)

    _dual(n_p, rows, (hp, op_ref), (hs, os_ref))


def _ffn_up(h_p, h_s, wg_all, wu_all, layer, tm, tn=512):
    (m_p, k), m_s = h_p.shape, h_s.shape[0]
    n = wg_all.shape[2]
    n_p = m_p // tm
    w_spec = pl.BlockSpec((None, k, tn), lambda j, i: (layer, 0, j))
    return pl.pallas_call(
        functools.partial(_ffn_up_kernel, n_p=n_p),
        name="ffn_up",
        grid=(n // tn, n_p + 1),
        in_specs=list(_row_specs(tm, m_s, n_p, k, False)) + [w_spec, w_spec],
        out_specs=list(_row_specs(tm, m_s, n_p, tn, True)),
        out_shape=[jax.ShapeDtypeStruct((m_p, n), BF16), jax.ShapeDtypeStruct((m_s, n), BF16)],
        scratch_shapes=[pltpu.VMEM((k, tn), BF16), pltpu.VMEM((k, tn), BF16)],
        compiler_params=_cparams(2),
    )(h_p, h_s, wg_all, wu_all)


def _ffn_down_kernel(xp, ap, xs, as_, w_ref, op_ref, os_ref, wbf_ref, *, n_p):
    @pl.when(pl.program_id(1) == 0)
    def _():
        wbf_ref[...] = w_ref[...].astype(BF16)

    def rows(x_ref, a_ref, o_ref):
        o_ref[...] = x_ref[...] + _dot(a_ref[...], wbf_ref[...])

    _dual(n_p, rows, (xp, ap, op_ref), (xs, as_, os_ref))


def _ffn_down(x_p, x_s, act_p, act_s, w_all, layer, tm, tn=512):
    (m_p, d), m_s = x_p.shape, x_s.shape[0]
    k = w_all.shape[1]
    n_p = m_p // tm
    x_specs = _row_specs(tm, m_s, n_p, tn, True)
    a_specs = _row_specs(tm, m_s, n_p, k, False)
    return pl.pallas_call(
        functools.partial(_ffn_down_kernel, n_p=n_p),
        name="ffn_down",
        grid=(d // tn, n_p + 1),
        in_specs=[x_specs[0], a_specs[0], x_specs[1], a_specs[1],
                  pl.BlockSpec((None, k, tn), lambda j, i: (layer, 0, j))],
        out_specs=list(x_specs),
        out_shape=[jax.ShapeDtypeStruct((m_p, d), F32), jax.ShapeDtypeStruct((m_s, d), F32)],
        scratch_shapes=[pltpu.VMEM((k, tn), BF16)],
        compiler_params=_cparams(2),
    )(x_p, act_p, x_s, act_s, w_all)


def _band_mask(window, n_keys, prev_valid):
    qi = lax.broadcasted_iota(jnp.int32, (BLOCK, n_keys), 0) + (n_keys - BLOCK)
    kj = lax.broadcasted_iota(jnp.int32, (BLOCK, n_keys), 1)
    dist = qi - kj
    band = (dist >= 0) & (dist <= window)
    if n_keys > BLOCK and prev_valid is not None:
        band = band & ((kj >= n_keys - BLOCK) | prev_valid)
    return band


def _attend_pairs(tasks, lane_lo):
    scores = []
    for q2, k2, _, mask in tasks:
        for hh in range(2):
            qm = jnp.where(lane_lo if hh == 0 else ~lane_lo, q2, 0.0).astype(BF16)
            s = lax.dot_general(qm, k2, _NT, preferred_element_type=F32) * ATTN_SCALE
            scores.append(jnp.where(mask, s, NEG))
    probs = []
    for s in scores:
        m = jnp.max(s, axis=-1, keepdims=True)
        p = jnp.exp(s - m)
        probs.append((p.astype(BF16), m, jnp.sum(p, axis=-1, keepdims=True)))
    out = []
    for t, (_, _, v2, _) in enumerate(tasks):
        out.append([(_dot(probs[2 * t + hh][0], v2),) + probs[2 * t + hh][1:] for hh in range(2)])
    return out


def _attn_a_kernel(*refs, window, dil, qb, pairs, with_prev, chained):
    refs = list(refs)
    q_ref = refs.pop(0)
    kp_ref = refs.pop(0) if with_prev else None
    kc_ref = refs.pop(0)
    vp_ref = refs.pop(0) if with_prev else None
    vc_ref = refs.pop(0)
    op_ref, lp_ref = (refs.pop(0), refs.pop(0)) if chained else (None, None)
    o_ref, l_ref = refs
    n_keys = 2 * BLOCK if with_prev else BLOCK
    mask_first = _band_mask(window, n_keys, pl.program_id(1) > 0)
    mask_rest = _band_mask(window, n_keys, None)
    lane = lax.broadcasted_iota(jnp.int32, (BLOCK, LANES), 1)
    lane_lo = lane < HEAD

    def rows(q, r):
        if dil == 1:
            return slice(q * BLOCK, (q + 1) * BLOCK)
        return pl.ds(q * dil * BLOCK + r, BLOCK, stride=dil)

    jobs = [(q, r, pp) for q in range(qb) for r in range(dil) for pp in range(pairs)]
    for g0 in range(0, len(jobs), ATTN_GROUP):
        grp = jobs[g0:g0 + ATTN_GROUP]
        tasks = []
        for q, r, pp in grp:
            sl = slice(pp * LANES, (pp + 1) * LANES)
            cur = rows(q, r)
            if with_prev:
                kprev = kp_ref[rows(0, r), sl] if q == 0 else kc_ref[rows(q - 1, r), sl]
                vprev = vp_ref[rows(0, r), sl] if q == 0 else vc_ref[rows(q - 1, r), sl]
                k2 = jnp.concatenate([kprev, kc_ref[cur, sl]], axis=0).astype(BF16)
                v2 = jnp.concatenate([vprev, vc_ref[cur, sl]], axis=0).astype(BF16)
            else:
                k2 = kc_ref[cur, sl].astype(BF16)
                v2 = vc_ref[cur, sl].astype(BF16)
            tasks.append((q_ref[cur, sl], k2, v2, mask_first if q == 0 else mask_rest))
        for (q, r, pp), heads in zip(grp, _attend_pairs(tasks, lane_lo)):
            sl = slice(pp * LANES, (pp + 1) * LANES)
            cur = rows(q, r)
            if chained:
                o_prev = op_ref[cur, sl]
                l_prev = lp_ref[cur, sl]
            halves = []
            lse_blk = jnp.zeros((BLOCK, LANES), F32)
            for hh, (o, m, l) in enumerate(heads):
                o = o / l
                lse = m + jnp.log(l)
                if chained:
                    lse_p = jnp.sum(jnp.where(lane == hh, l_prev, 0.0), axis=-1, keepdims=True)
                    mx = jnp.maximum(lse_p, lse)
                    wp = jnp.exp(lse_p - mx)
                    wi = jnp.exp(lse - mx)
                    den = wp + wi
                    o = (o_prev * wp + o * wi) / den
                    lse = mx + jnp.log(den)
                halves.append(o)
                lse_blk = jnp.where(lane == hh, lse, lse_blk)
            o_ref[cur, sl] = jnp.where(lane_lo, halves[0], halves[1]).astype(o_ref.dtype)
            l_ref[cur, sl] = lse_blk


def _attn_a_branch(p2d, o_prev, l_prev, batch, seq, window, dil, out_dtype):
    rows = batch * seq
    unit = dil * BLOCK
    pairs = 1 if dil > 1 else A_HEADS // 2
    qb = max(1, min(seq // unit, STEP_PAIRS // (dil * pairs)))
    nblk = seq // (unit * qb)
    with_prev = seq > unit
    width = pairs * LANES
    n_col = A_DIM // width
    chained = o_prev is not None

    def cur(off):
        return lambda b, n, c: (b * nblk + n, off // width + c)

    def prev(off):
        return lambda b, n, c: (b * nblk * qb + jnp.maximum(n * qb - 1, 0), off // width + c)

    blk = (unit * qb, width)
    pblk = (unit, width)
    in_specs = [pl.BlockSpec(blk, cur(OFF_QA))]
    in_specs += [pl.BlockSpec(pblk, prev(OFF_KA))] if with_prev else []
    in_specs += [pl.BlockSpec(blk, cur(OFF_KA))]
    in_specs += [pl.BlockSpec(pblk, prev(OFF_VA))] if with_prev else []
    in_specs += [pl.BlockSpec(blk, cur(OFF_VA))]
    args = [p2d] * len(in_specs)
    o_spec = pl.BlockSpec(blk, cur(0))
    if chained:
        in_specs += [o_spec, o_spec]
        args += [o_prev, l_prev]
    o, l = pl.pallas_call(
        functools.partial(_attn_a_kernel, window=window // dil, dil=dil, qb=qb, pairs=pairs, with_prev=with_prev,
                          chained=chained),
        grid=(batch, nblk, n_col),
        in_specs=in_specs,
        out_specs=[o_spec, o_spec],
        out_shape=[jax.ShapeDtypeStruct((rows, A_DIM), out_dtype), jax.ShapeDtypeStruct((rows, A_DIM), F32)],
        compiler_params=_cparams(3),
        name=f"attn_a_dil{dil}",
    )(*args)
    return o, l


def _attn_c_kernel(sink_ref, q0_ref, q1_ref, q2_ref, q3_ref, kp_ref, kc_ref, vp_ref, vc_ref, o_ref, *, qb):
    q_refs = (q0_ref, q1_ref, q2_ref, q3_ref)
    mask_first = _band_mask(C_WINDOW, 2 * BLOCK, pl.program_id(1) > 0)
    mask_rest = _band_mask(C_WINDOW, 2 * BLOCK, None)
    lane_lo = lax.broadcasted_iota(jnp.int32, (BLOCK, LANES), 1) < HEAD
    lane_lo2 = lax.broadcasted_iota(jnp.int32, (2 * BLOCK, LANES), 1) < HEAD

    def blk_rows(q):
        return slice(q * BLOCK, (q + 1) * BLOCK)

    jobs = []
    for q in range(qb):
        k2 = jnp.concatenate([kp_ref[...] if q == 0 else kc_ref[blk_rows(q - 1), :], kc_ref[blk_rows(q), :]], axis=0)
        v2 = jnp.concatenate([vp_ref[...] if q == 0 else vc_ref[blk_rows(q - 1), :], vc_ref[blk_rows(q), :]], axis=0)
        k2r = pltpu.roll(k2, HEAD, 1)
        v2r = pltpu.roll(v2, HEAD, 1)
        kdup = [jnp.where(lane_lo2, k2, k2r).astype(BF16), jnp.where(lane_lo2, k2r, k2).astype(BF16)]
        vdup = [jnp.where(lane_lo2, v2, v2r).astype(BF16), jnp.where(lane_lo2, v2r, v2).astype(BF16)]
        for pr in range(C_Q_HEADS // 2):
            g = (2 * pr) // C_GROUP
            q2 = q_refs[pr // 2][blk_rows(q), (pr % 2) * LANES:(pr % 2 + 1) * LANES]
            jobs.append((q, pr, (q2, kdup[g], vdup[g], mask_first if q == 0 else mask_rest)))
    for g0 in range(0, len(jobs), ATTN_GROUP):
        grp = jobs[g0:g0 + ATTN_GROUP]
        for (q, pr, _), heads in zip(grp, _attend_pairs([t for _, _, t in grp], lane_lo)):
            halves = []
            for hh, (o, m, l) in enumerate(heads):
                lse = m + jnp.log(l)
                halves.append(o * (_sigmoid(lse - sink_ref[2 * pr + hh]) / l))
            o_ref[blk_rows(q), pr * LANES:(pr + 1) * LANES] = jnp.where(lane_lo, halves[0], halves[1]).astype(o_ref.dtype)


def _attn_c_prompt(p2d, sink, batch, seq):
    rows = batch * seq
    qb = max(1, min(seq // BLOCK, STEP_PAIRS // (C_Q_HEADS // 2)))
    nb = seq // (BLOCK * qb)

    def cur(col):
        return lambda b, n: (b * nb + n, col)

    def prev(col):
        return lambda b, n: (b * nb * qb + jnp.maximum(n * qb - 1, 0), col)

    kv_blk = (BLOCK * qb, C_KV_DIM)
    kv_prev = (BLOCK, C_KV_DIM)
    return pl.pallas_call(
        functools.partial(_attn_c_kernel, qb=qb),
        name="attn_c",
        grid=(batch, nb),
        in_specs=[
            pl.BlockSpec(memory_space=pltpu.SMEM),
            *[pl.BlockSpec((BLOCK * qb, QC_BLK), cur(OFF_QC // QC_BLK + i)) for i in range(C_DIM // QC_BLK)],
            pl.BlockSpec(kv_prev, prev(OFF_KC // C_KV_DIM)),
            pl.BlockSpec(kv_blk, cur(OFF_KC // C_KV_DIM)),
            pl.BlockSpec(kv_prev, prev(OFF_VC // C_KV_DIM)),
            pl.BlockSpec(kv_blk, cur(OFF_VC // C_KV_DIM)),
        ],
        out_specs=pl.BlockSpec((BLOCK * qb, C_DIM), lambda b, n: (b * nb + n, 0)),
        out_shape=jax.ShapeDtypeStruct((rows, C_DIM), BF16),
        compiler_params=_cparams(2),
    )(sink, *([p2d] * (C_DIM // QC_BLK)), p2d, p2d, p2d, p2d)


def _attn_a_sample_kernel(q_ref, kn_ref, vn_ref, kc_ref, vc_ref, cc_ref, cn_ref, o_ref):
    t = q_ref.shape[0]
    pad = jnp.zeros((BLOCK - t, A_DIM), F32)
    k_new = jnp.concatenate([kn_ref[...], pad], axis=0)
    v_new = jnp.concatenate([vn_ref[...], pad], axis=0)
    cnt_c, cnt_n = cc_ref[...], cn_ref[...]
    outs = []
    for h in range(A_HEADS):
        lanes = slice(h * HEAD, (h + 1) * HEAD)
        q = q_ref[:, lanes].astype(BF16)
        s_c = _dot(q, kc_ref[h].astype(BF16)) * ATTN_SCALE
        s_n = lax.dot_general(q, k_new[:, lanes].astype(BF16), _NT, preferred_element_type=F32) * ATTN_SCALE
        s_c = jnp.where(cnt_c > 0.0, s_c, NEG)
        s_n = jnp.where(cnt_n > 0.0, s_n, NEG)
        m = jnp.maximum(jnp.max(s_c, axis=-1, keepdims=True), jnp.max(s_n, axis=-1, keepdims=True))
        p_c = cnt_c * jnp.exp(s_c - m)
        p_n = cnt_n * jnp.exp(s_n - m)
        l = jnp.sum(p_c, axis=-1, keepdims=True) + jnp.sum(p_n, axis=-1, keepdims=True)
        o = lax.dot_general(p_c.astype(BF16), vc_ref[h].astype(BF16), _NT, preferred_element_type=F32)
        o += _dot(p_n.astype(BF16), v_new[:, lanes].astype(BF16))
        outs.append(o / l)
    o_ref[...] = jnp.concatenate(outs, axis=-1)


def _a_sample_counts(t, n_buf):
    qi = n_buf + np.arange(t)[:, None]

    def count(rows):
        delta = qi - rows[None, :]
        c = np.zeros(delta.shape, np.float32)
        for window, dil in A_BRANCHES:
            c += ((delta >= 0) & (delta <= window) & (delta % dil == 0)).astype(np.float32)
        return c

    return count(np.arange(n_buf)), count(n_buf + np.arange(BLOCK))


def _attn_a_sample(ps, cache_k, cache_v, layer, batch, t):
    n_buf = cache_k.shape[4]
    cnt_c, cnt_n = _a_sample_counts(t, n_buf)
    new_blk = (t, A_DIM)
    cache_spec = pl.BlockSpec((None, None, A_HEADS, HEAD, n_buf), lambda b: (layer, b, 0, 0, 0))
    return pl.pallas_call(
        _attn_a_sample_kernel,
        name="attn_a_sample",
        grid=(batch,),
        in_specs=[
            pl.BlockSpec(new_blk, lambda b: (b, OFF_QA // A_DIM)),
            pl.BlockSpec(new_blk, lambda b: (b, OFF_KA // A_DIM)),
            pl.BlockSpec(new_blk, lambda b: (b, OFF_VA // A_DIM)),
            cache_spec, cache_spec,
            pl.BlockSpec(cnt_c.shape, lambda b: (0, 0)),
            pl.BlockSpec(cnt_n.shape, lambda b: (0, 0)),
        ],
        out_specs=pl.BlockSpec(new_blk, lambda b: (b, 0)),
        out_shape=jax.ShapeDtypeStruct((batch * t, A_DIM), F32),
        compiler_params=_cparams(1),
    )(ps, ps, ps, cache_k, cache_v, jnp.asarray(cnt_c), jnp.asarray(cnt_n))


def _attn_c_sample_kernel(q0_ref, q1_ref, q2_ref, q3_ref, kn_ref, vn_ref, kc_ref, vc_ref, sink_ref, o_ref):
    q_refs = (q0_ref, q1_ref, q2_ref, q3_ref)
    t = q0_ref.shape[0]
    n_buf = kc_ref.shape[1]
    rows = C_Q_HEADS * t
    lane_lo = lax.broadcasted_iota(jnp.int32, (t, LANES), 1) < HEAD
    blocks = []
    for j in range(C_Q_HEADS // 2):
        chunk = q_refs[j // 2][:, (j % 2) * LANES:(j % 2 + 1) * LANES]
        rolled = pltpu.roll(chunk, HEAD, 1)
        if (2 * j) // C_GROUP == 0:
            blocks += [jnp.where(lane_lo, chunk, 0.0), jnp.where(lane_lo, rolled, 0.0)]
        else:
            blocks += [jnp.where(lane_lo, 0.0, rolled), jnp.where(lane_lo, 0.0, chunk)]
    qbd = jnp.concatenate(blocks, axis=0).astype(BF16)
    pad = jnp.zeros((BLOCK - t, C_KV_DIM), F32)
    k_new = jnp.concatenate([kn_ref[...], pad], axis=0).astype(BF16)
    v_new = jnp.concatenate([vn_ref[...], pad], axis=0).astype(BF16)
    s_c = _dot(qbd, kc_ref[...].astype(BF16)) * ATTN_SCALE
    s_n = lax.dot_general(qbd, k_new, _NT, preferred_element_type=F32) * ATTN_SCALE
    qt = lax.broadcasted_iota(jnp.int32, (rows, BLOCK), 0) % t
    kj = lax.broadcasted_iota(jnp.int32, (rows, BLOCK), 1)
    dist_c = n_buf + qt - kj
    s_c = jnp.where((dist_c >= 0) & (dist_c <= C_WINDOW), s_c, NEG)
    s_n = jnp.where(kj <= qt, s_n, NEG)
    m = jnp.maximum(jnp.max(s_c, axis=-1, keepdims=True), jnp.max(s_n, axis=-1, keepdims=True))
    p_c = jnp.exp(s_c - m)
    p_n = jnp.exp(s_n - m)
    l = jnp.sum(p_c, axis=-1, keepdims=True) + jnp.sum(p_n, axis=-1, keepdims=True)
    o = lax.dot_general(p_c.astype(BF16), vc_ref[...].astype(BF16), _NT, preferred_element_type=F32)
    o += _dot(p_n.astype(BF16), v_new)
    lse = m + jnp.log(l)
    o = o * (_sigmoid(lse - sink_ref[...]) / l)
    for j in range(C_Q_HEADS // 2):
        blk_a = o[2 * j * t:(2 * j + 1) * t, :]
        blk_b = o[(2 * j + 1) * t:(2 * j + 2) * t, :]
        if (2 * j) // C_GROUP == 0:
            out = jnp.where(lane_lo, blk_a, pltpu.roll(blk_b, HEAD, 1))
        else:
            out = jnp.where(lane_lo, pltpu.roll(blk_a, HEAD, 1), blk_b)
        o_ref[:, j * LANES:(j + 1) * LANES] = out


def _attn_c_sample(ps, cache_k, cache_v, layer, sink_col, batch, t):
    n_buf = cache_k.shape[3]
    assert n_buf == BLOCK
    kv_blk = (t, C_KV_DIM)
    cache_spec = pl.BlockSpec((None, None, C_KV_DIM, n_buf), lambda b: (layer, b, 0, 0))
    return pl.pallas_call(
        _attn_c_sample_kernel,
        name="attn_c_sample",
        grid=(batch,),
        in_specs=[
            *[pl.BlockSpec((t, QC_BLK), functools.partial(lambda b, i: (b, OFF_QC // QC_BLK + i), i=i))
              for i in range(C_DIM // QC_BLK)],
            pl.BlockSpec(kv_blk, lambda b: (b, OFF_KC // C_KV_DIM)),
            pl.BlockSpec(kv_blk, lambda b: (b, OFF_VC // C_KV_DIM)),
            cache_spec, cache_spec,
            pl.BlockSpec(sink_col.shape, lambda b: (0, 0)),
        ],
        out_specs=pl.BlockSpec((t, C_DIM), lambda b: (b, 0)),
        out_shape=jax.ShapeDtypeStruct((batch * t, C_DIM), F32),
        compiler_params=_cparams(1),
    )(*([ps] * (C_DIM // QC_BLK)), ps, ps, cache_k, cache_v, sink_col)


def _rwkv_kernel(r_ref, k_ref, v_ref, lo_ref, shift_ref, s0_ref, mu_ref, vec_ref, w2_ref, a2_ref, g2_ref,
                 o_ref, sout_ref,
                 s_scr, r_scr, d_scr, k_scr, v_scr, kk_scr, b_scr, g_scr, y_scr, wr_scr, kr_scr,
                 cx_scr, cl_scr, *, nb, tc):
    c = pl.program_id(1)
    n_pairs = B_HEADS // 2
    f32 = F32

    li2 = lax.broadcasted_iota(jnp.int32, (2 * LANES, 2 * LANES), 0) // HEAD
    lj2 = lax.broadcasted_iota(jnp.int32, (2 * LANES, 2 * LANES), 1) // HEAD
    bd2 = (li2 == lj2).astype(BF16)

    def head_sum(x):
        return jnp.concatenate([_seg_sum(x[:, j:j + 2 * LANES], bd2) for j in range(0, B_DIM, 2 * LANES)], axis=-1)

    @pl.when(c == 0)
    def _():
        for b in range(nb):
            for p in range(n_pairs):
                s_scr[b * n_pairs + p] = jnp.concatenate([s0_ref[b, 2 * p], s0_ref[b, 2 * p + 1]], axis=-1)
            cx_scr[b] = jnp.broadcast_to(shift_ref[0, b:b + 1, 0:3 * B_DIM], (SUBLANES, 3 * B_DIM))
            cl_scr[b] = jnp.broadcast_to(shift_ref[0, b:b + 1, 3 * B_DIM:], (SUBLANES, LORA_COLS))

    n_rows = nb * tc
    row = lax.broadcasted_iota(jnp.int32, (n_rows, 1), 0)

    def merged(ref):
        return ref[...].reshape(n_rows, ref.shape[-1])

    def lerp(cur, carry_scr, lo_col, hi_col):
        sh = cur if n_rows == 1 else pltpu.roll(cur, 1, 0)
        for b in range(nb):
            sh = jnp.where(row == b * tc, carry_scr[b, 0:1, lo_col:hi_col], sh)
        return sh

    def keep_last(carry_scr, lo_col, hi_col, raw):
        for b in range(nb):
            carry_scr[b, :, lo_col:hi_col] = jnp.broadcast_to(raw[(b + 1) * tc - 1:(b + 1) * tc, :],
                                                             (SUBLANES, hi_col - lo_col))

    w0, a0, k_k, k_a = vec_ref[0:1, :], vec_ref[1:2, :], vec_ref[2:3, :], vec_ref[3:4, :]
    r_k, lnx_w, lnx_b = vec_ref[4:5, :], vec_ref[5:6, :], vec_ref[6:7, :]

    mixed = []
    for i, (ref, scr) in enumerate(((r_ref, cx_scr), (k_ref, cx_scr), (v_ref, cx_scr), (lo_ref, cl_scr))):
        raw = merged(ref)
        lo_col = i * B_DIM if scr is cx_scr else 0
        hi_col = lo_col + raw.shape[1]
        sh = lerp(raw, scr, lo_col, hi_col)
        keep_last(scr, lo_col, hi_col, raw)
        mixed.append(raw + (sh - raw) * mu_ref[:, i * B_DIM:i * B_DIM + raw.shape[1]])
    r, k, v, lo = mixed

    z = w0 + _dot_hi(jnp.tanh(lo), w2_ref[...])
    sp = jnp.maximum(-z, 0.0) + jnp.log(1.0 + jnp.exp(-jnp.abs(z)))
    decay = jnp.exp(-jnp.exp(-sp - 0.5))
    a = _sigmoid(a0 + _dot_hi(lo, a2_ref[...]))
    g = _dot_hi(_sigmoid(lo), g2_ref[...])
    kkr = k * k_k
    kk = kkr / jnp.maximum(jnp.sqrt(head_sum(kkr * kkr)), 1e-12)
    k = k * (1.0 + (a - 1.0) * k_a)
    bb = kk * a
    wr = decay * r - kk * head_sum(bb * r)
    kr = head_sum(k * r)
    for scr, val in ((r_scr, r), (d_scr, decay), (k_scr, k), (v_scr, v), (kk_scr, kk), (b_scr, bb), (g_scr, g),
                     (wr_scr, wr), (kr_scr, kr)):
        scr[...] = val.reshape(nb, tc, B_DIM)

    sub = lax.broadcasted_iota(jnp.int32, (HEAD, LANES), 0)
    lane = lax.broadcasted_iota(jnp.int32, (HEAD, LANES), 1)
    diag = (lane % HEAD) == sub
    sub8 = lax.broadcasted_iota(jnp.int32, (SUBLANES, LANES), 0)
    n_all = nb * n_pairs
    grp = min(n_all, RWKV_MATMUL_PAIRS)
    groups = [list(range(g0, g0 + grp)) for g0 in range(0, n_all, grp)]

    def step8(t8, carry):
        t0 = pl.multiple_of(t8 * SUBLANES, SUBLANES)

        def rows_of(scr):
            return [scr[i // n_pairs, pl.ds(t0, SUBLANES), (i % n_pairs) * LANES:(i % n_pairs + 1) * LANES]
                    for i in range(n_all)]

        kk8, v8, d8, b8, k8, wr8, kr8 = [
            rows_of(scr) for scr in (kk_scr, v_scr, d_scr, b_scr, k_scr, wr_scr, kr_scr)]
        st = [s_scr[i] for i in range(n_all)]
        yb = [jnp.zeros((SUBLANES, LANES), f32) for _ in range(n_all)]

        vb = {}
        for grp_ids in groups:
            lhs = jnp.concatenate(
                [jnp.concatenate([jnp.where(diag, v8[i][2 * m:2 * m + 1, :], 0.0).astype(BF16),
                                  jnp.where(diag, v8[i][2 * m + 1:2 * m + 2, :], 0.0).astype(BF16)], axis=1)
                 for m in range(SUBLANES // 2) for i in grp_ids], axis=0)
            res = _dot(lhs, bd2)
            for m in range(SUBLANES // 2):
                for q, i in enumerate(grp_ids):
                    blk = res[(m * len(grp_ids) + q) * HEAD:(m * len(grp_ids) + q + 1) * HEAD, :]
                    vb[(i, 2 * m)] = blk[:, 0:LANES]
                    vb[(i, 2 * m + 1)] = blk[:, LANES:]

        for j in range(SUBLANES):
            for grp_ids in groups:
                lhs = jnp.concatenate(
                    [jnp.concatenate([st[i] * kk8[i][j:j + 1, :], st[i] * wr8[i][j:j + 1, :]], axis=1).astype(BF16)
                     for i in grp_ids], axis=0)
                res = _dot(lhs, bd2)
                for q, i in enumerate(grp_ids):
                    sa = res[q * HEAD:(q + 1) * HEAD, 0:LANES]
                    u = res[q * HEAD:(q + 1) * HEAD, LANES:]
                    u_row = jnp.sum(jnp.where(diag, u, 0.0), axis=0, keepdims=True)
                    y_row = u_row + v8[i][j:j + 1, :] * kr8[i][j:j + 1, :]
                    yb[i] = jnp.where(sub8 == j, y_row, yb[i])
                    st[i] = st[i] * d8[i][j:j + 1, :] - sa * b8[i][j:j + 1, :] + vb[(i, j)] * k8[i][j:j + 1, :]

        for i in range(n_all):
            s_scr[i] = st[i]
            y_scr[i // n_pairs, pl.ds(t0, SUBLANES), (i % n_pairs) * LANES:(i % n_pairs + 1) * LANES] = yb[i]
        return carry

    lax.fori_loop(0, tc // SUBLANES, step8, 0)

    y = merged(y_scr)
    yc = y - head_sum(y) * (1.0 / HEAD)
    var = head_sum(yc * yc) * (1.0 / HEAD)
    yn = yc * lax.rsqrt(var + GN_EPS) * lnx_w + lnx_b
    bonus = head_sum(merged(r_scr) * merged(k_scr) * r_k) * merged(v_scr)
    o_ref[...] = ((yn + bonus) * merged(g_scr)).reshape(nb, tc, B_DIM).astype(o_ref.dtype)

    @pl.when(c == pl.num_programs(1) - 1)
    def _():
        for b in range(nb):
            for p in range(n_pairs):
                s = s_scr[b * n_pairs + p]
                sout_ref[b, 2 * p] = s[:, 0:HEAD]
                sout_ref[b, 2 * p + 1] = s[:, HEAD:]


def _rwkv(p3d, shift0, s0, mu, vecs, w2p, a2p, g2p, nb, tc, out_dtype=F32):
    batch, seq, _ = p3d.shape
    groups = batch // nb
    chunks = seq // tc
    x_blk = (nb, tc, B_DIM)

    def xmap(col):
        return lambda g, c: (g, c, col)

    const2 = lambda g, c: (0, 0)
    scr = lambda *shape: pltpu.VMEM(shape, F32)
    o, s_out = pl.pallas_call(
        functools.partial(_rwkv_kernel, nb=nb, tc=tc),
        name="rwkv7",
        grid=(groups, chunks),
        in_specs=[
            pl.BlockSpec(x_blk, xmap(OFF_PB // B_DIM)),
            pl.BlockSpec(x_blk, xmap(OFF_PB // B_DIM + 1)),
            pl.BlockSpec(x_blk, xmap(OFF_PB // B_DIM + 2)),
            pl.BlockSpec((nb, tc, LORA_COLS), xmap(OFF_LORA // LORA_COLS)),
            pl.BlockSpec((1, nb, B_COLS), lambda g, c: (g, 0, 0)),
            pl.BlockSpec((nb, B_HEADS, HEAD, HEAD), lambda g, c: (g, 0, 0, 0)),
            pl.BlockSpec((1, B_COLS), const2),
            pl.BlockSpec((SUBLANES, B_DIM), const2),
            pl.BlockSpec((LORA_COLS, B_DIM), const2),
            pl.BlockSpec((LORA_COLS, B_DIM), const2),
            pl.BlockSpec((LORA_COLS, B_DIM), const2),
        ],
        out_specs=[
            pl.BlockSpec(x_blk, lambda g, c: (g, c, 0)),
            pl.BlockSpec((nb, B_HEADS, HEAD, HEAD), lambda g, c: (g, 0, 0, 0)),
        ],
        out_shape=[jax.ShapeDtypeStruct((batch, seq, B_DIM), out_dtype),
                   jax.ShapeDtypeStruct((batch, B_HEADS, HEAD, HEAD), F32)],
        scratch_shapes=[scr(nb * B_HEADS // 2, HEAD, LANES)] + [scr(nb, tc, B_DIM)] * 10
        + [scr(nb, SUBLANES, 3 * B_DIM), scr(nb, SUBLANES, LORA_COLS)],
        compiler_params=_cparams(2),
    )(p3d, p3d, p3d, p3d, shift0.reshape(groups, nb, B_COLS), s0, mu, vecs, w2p, a2p, g2p)
    return o, s_out


def _rope_tables(pos):
    half = ROPE_DIM // 2
    inv = jnp.exp(-math.log(ROPE_THETA) * jnp.arange(half, dtype=F32) * 2.0 / ROPE_DIM)
    ang = pos.astype(F32)[:, None] * inv[None, :]
    cos, sin = jnp.cos(ang), jnp.sin(ang)
    lm = np.arange(LANES) % HEAD
    first = jnp.asarray(lm < half)[None, :]
    second = jnp.asarray((lm >= half) & (lm < ROPE_DIM))[None, :]
    freq = np.where(lm < half, lm, np.where(lm < ROPE_DIM, lm - half, 0))
    cos_l, sin_l = cos[:, freq], sin[:, freq]
    c = jnp.where(first | second, cos_l, 1.0)
    s1 = jnp.where(first, -sin_l, 0.0)
    s2 = jnp.where(second, sin_l, 0.0)
    return c, s1, s2


def _rope_flag():
    col = np.arange(IN_COLS)
    rope = (col < OFF_VA) | ((col >= OFF_QC) & (col < OFF_VC))
    tiles = tuple(int(t) for t in np.nonzero(rope.reshape(-1, IN_TN).any(axis=1))[0])
    return jnp.asarray(rope.astype(np.float32))[None, :], tiles


def _pad_rows(w, start):
    return jnp.zeros((LORA_COLS, B_DIM), F32).at[start:start + w.shape[0]].set(w)


def _mixers(p2d, batch, seq, layer, is_prompt, cache, rwkv_w, sink):
    p3d = p2d.reshape(batch, seq, IN_COLS)
    mu, vecs, w2p, a2p, g2p = rwkv_w
    if is_prompt:
        o, l = None, None
        order = sorted(A_BRANCHES, key=lambda wd: wd[1] == 1)
        for i, (window, dil) in enumerate(order):
            last = i == len(order) - 1
            o, l = _attn_a_branch(p2d, o, l, batch, seq, window, dil, BF16 if last else F32)
        oa = o
        oc = _attn_c_prompt(p2d, sink, batch, seq)
        shift0 = jnp.zeros((batch, B_COLS), F32)
        s0 = jnp.zeros((batch, B_HEADS, HEAD, HEAD), F32)
        ob, wkv = _rwkv(p3d, shift0, s0, mu, vecs, w2p, a2p, g2p, nb=batch, tc=min(seq, 128), out_dtype=BF16)
    else:
        a_k, a_v, c_k, c_v, wkv0, shift0 = cache
        oa = _attn_a_sample(p2d, a_k, a_v, layer, batch, seq)
        sink_col = jnp.repeat(sink, seq)[:, None]
        oc = _attn_c_sample(p2d, c_k, c_v, layer, sink_col, batch, seq)
        ob, wkv = _rwkv(p3d, shift0[layer], wkv0[layer], mu, vecs, w2p, a2p, g2p, nb=4, tc=seq)
    return oa, ob.reshape(batch * seq, B_DIM), oc, wkv


def kernel(x_prompt, x_sample, cache_a_k, cache_a_v, cache_c_k, cache_c_v, state_b_wkv, state_b_shift, g_mix, w_in, w_out, b_mu, b_w0, b_w2, b_a0, b_a2, b_g2, b_k_k, b_k_a, b_r_k, b_lnx_w, b_lnx_b, c_sink, g_ffn, w_gate, w_up, w_down, g_final):
    depth = w_in.shape[0]
    bp, lp, d = x_prompt.shape
    bs, ls, _ = x_sample.shape
    a_win = cache_a_k.shape[2]
    c_win = cache_c_k.shape[2]
    assert (bp * lp) % PROJ_TM == 0 and lp % PROJ_TM == 0 and lp % (16 * BLOCK) == 0 and a_win >= A_BRANCHES[-1][0] and c_win == C_WINDOW and bs % 4 == 0

    flag, rope_tiles = _rope_flag()
    tabs_p = _rope_tables(jnp.arange(lp, dtype=jnp.int32))
    tabs_s = _rope_tables(jnp.tile(PAST_LEN + jnp.arange(ls, dtype=jnp.int32), bs))
    cak = jnp.transpose(cache_a_k, (0, 1, 3, 4, 2))
    cav = jnp.transpose(cache_a_v, (0, 1, 3, 4, 2))
    cck = jnp.transpose(cache_c_k, (0, 1, 3, 4, 2)).reshape(depth, bs, C_KV_DIM, c_win)
    ccv = jnp.transpose(cache_c_v, (0, 1, 3, 4, 2)).reshape(depth, bs, C_KV_DIM, c_win)
    cache = (cak, cav, cck, ccv, state_b_wkv, state_b_shift)

    xp = x_prompt.reshape(bp * lp, d)
    xs = x_sample.reshape(bs * ls, d)
    new_p = [[] for _ in range(6)]
    new_s = [[] for _ in range(6)]
    for l in range(depth):
        vecs = jnp.stack([b_w0[l], b_a0[l], b_k_k[l], b_k_a[l], b_r_k[l], b_lnx_w[l], b_lnx_b[l],
                          jnp.zeros((B_DIM,), F32)], axis=0)
        rwkv_w = (b_mu[l][None, :], vecs, _pad_rows(b_w2[l], 0), _pad_rows(b_a2[l], 96), _pad_rows(b_g2[l], 192))
        sink = c_sink[l].reshape(C_Q_HEADS)
        hp, hs = _rmsnorm(xp, g_mix[l], BF16), _rmsnorm(xs, g_mix[l], BF16)
        pp, ps = _inproj(hp, hs, w_in, l, flag, rope_tiles, tabs_p, tabs_s, PROJ_TM)
        oa_p, ob_p, oc_p, wkv_p = _mixers(pp, bp, lp, l, True, cache, rwkv_w, sink)
        oa_s, ob_s, oc_s, wkv_s = _mixers(ps, bs, ls, l, False, cache, rwkv_w, sink)
        xp, xs = _outproj((xp, xs), (oa_p, oa_s), (ob_p, ob_s), (oc_p, oc_s), w_out, l, PROJ_TM)
        hp, hs = _rmsnorm(xp, g_ffn[l], BF16), _rmsnorm(xs, g_ffn[l], BF16)
        act_p, act_s = _ffn_up(hp, hs, w_gate, w_up, l, PROJ_TM)
        xp, xs = _ffn_down(xp, xs, act_p, act_s, w_down, l, DOWN_TM)
        for p2d, batch, seq, wkv, new, is_prompt in ((pp, bp, lp, wkv_p, new_p, True), (ps, bs, ls, wkv_s, new_s, False)):
            p3d = p2d.reshape(batch, seq, IN_COLS)
            a_keep = min(a_win, seq) if is_prompt else seq
            c_keep = min(c_win, seq) if is_prompt else seq
            state = (
                p3d[:, seq - a_keep:, OFF_KA:OFF_VA].reshape(batch, a_keep, A_HEADS, HEAD),
                p3d[:, seq - a_keep:, OFF_VA:OFF_PB].reshape(batch, a_keep, A_HEADS, HEAD),
                p3d[:, seq - c_keep:, OFF_KC:OFF_VC].reshape(batch, c_keep, C_KV_DIM // HEAD, HEAD),
                p3d[:, seq - c_keep:, OFF_VC:].reshape(batch, c_keep, C_KV_DIM // HEAD, HEAD),
                wkv,
                p3d[:, -1, OFF_PB:OFF_QC],
            )
            for i in range(6):
                new[i].append(state[i])
    y_prompt = _rmsnorm(xp, g_final, F32).reshape(bp, lp, d)
    y_sample = _rmsnorm(xs, g_final, F32).reshape(bs, ls, d)
    outs_p = [jnp.stack(t, axis=0) for t in new_p]
    outs_s = [jnp.stack(t, axis=0) for t in new_s]
    return (y_prompt, y_sample, *outs_p, *outs_s)
```

```python
import functools
import math

import numpy as np
import jax
import jax.numpy as jnp
from jax import lax
from jax.experimental import pallas as pl
from jax.experimental.pallas import tpu as pltpu

F32 = jnp.float32
BF16 = jnp.bfloat16

LANES = 128
SUBLANES = 8
VMEM_LIMIT = 52 * 1024 * 1024

D_MODEL = 2048
HEAD = 64
A_DIM = 512
B_DIM = 512
C_DIM = 1024
C_KV_DIM = 128
A_HEADS = 8
B_HEADS = 8
C_Q_HEADS = 16
C_GROUP = 8
LORA_COLS = 256
B_COLS = 3 * B_DIM + LORA_COLS
IN_COLS = 3 * A_DIM + B_COLS + C_DIM + 2 * C_KV_DIM
D_FF = 5632
OFF_QA, OFF_KA, OFF_VA = 0, A_DIM, 2 * A_DIM
OFF_PB = 3 * A_DIM
OFF_LORA = OFF_PB + 3 * B_DIM
OFF_QC = OFF_PB + B_COLS
OFF_KC = OFF_QC + C_DIM
OFF_VC = OFF_KC + C_KV_DIM
A_BRANCHES = ((128, 1), (512, 4), (2048, 16))
C_WINDOW = 128
BLOCK = 128
QC_BLK = 256
IN_TN = 512
PROJ_TM = 1024
DOWN_TM = 512
STEP_PAIRS = 16
ATTN_GROUP = 4
RWKV_MATMUL_PAIRS = 16
PAST_LEN = 16384
ROPE_THETA = 500000.0
ROPE_DIM = 16
RMS_EPS = 1e-6
GN_EPS = 64e-5
ATTN_SCALE = HEAD ** -0.5
NEG = -1e30
LOG2E = 1.4426950408889634
LN2 = 0.6931471805599453

_NT = (((1,), (1,)), ((), ()))


def _cparams(n_grid):
    return pltpu.CompilerParams(dimension_semantics=("arbitrary",) * n_grid, vmem_limit_bytes=VMEM_LIMIT)


def _dot(a, b):
    return jnp.dot(a, b, preferred_element_type=F32)


def _split_bf16(x):
    hi = x.astype(BF16)
    lo = (x - hi.astype(F32)).astype(BF16)
    return hi, lo


def _dot_hi(a, b):
    a_hi, a_lo = _split_bf16(a)
    b_hi, b_lo = _split_bf16(b)
    return _dot(a_hi, b_hi) + (_dot(a_lo, b_hi) + _dot(a_hi, b_lo))


def _seg_sum(x, bd):
    hi, lo = _split_bf16(x)
    return _dot(hi, bd) + _dot(lo, bd)


def _sigmoid(x):
    return 1.0 / (1.0 + jnp.exp(-x))


def _rmsnorm_kernel(x_ref, g_ref, o_ref):
    x = x_ref[...]
    ms = jnp.mean(x * x, axis=-1, keepdims=True)
    o_ref[...] = (x * lax.rsqrt(ms + RMS_EPS) * g_ref[...]).astype(o_ref.dtype)


def _rmsnorm(x, g, out_dtype):
    m, d = x.shape
    tm = min(m, 512)
    return pl.pallas_call(
        _rmsnorm_kernel,
        name="rmsnorm",
        grid=(m // tm,),
        in_specs=[pl.BlockSpec((tm, d), lambda i: (i, 0)), pl.BlockSpec((1, d), lambda i: (0, 0))],
        out_specs=pl.BlockSpec((tm, d), lambda i: (i, 0)),
        out_shape=jax.ShapeDtypeStruct((m, d), out_dtype),
        compiler_params=_cparams(1),
    )(x, g.reshape(1, d))


def _row_specs(tm_p, tm_s, n_p, width, tiled, cycle=None):
    col = (lambda j: j) if tiled else (lambda j: 0)
    row = (lambda i: jnp.minimum(i, n_p - 1)) if cycle is None else (lambda i: jnp.minimum(i, n_p - 1) % cycle)
    return (pl.BlockSpec((tm_p, width), lambda j, i: (row(i), col(j))),
            pl.BlockSpec((tm_s, width), lambda j, i: (0, col(j))))


def _dual(n_p, tile_fn, prompt_refs, sample_refs):
    i = pl.program_id(1)

    @pl.when(i < n_p)
    def _():
        tile_fn(*prompt_refs)

    @pl.when(i == n_p)
    def _():
        tile_fn(*sample_refs)


def _inproj_kernel(hp, cp, s1p, s2p, hs, cs, s1s, s2s, w_ref, flag_ref, op_ref, os_ref, wbf_ref, *, rope_tiles, n_p):
    @pl.when(pl.program_id(1) == 0)
    def _():
        wbf_ref[...] = w_ref[...].astype(BF16)

    tile = pl.program_id(0)
    has_rope = functools.reduce(jnp.logical_or, [tile == t for t in rope_tiles])

    def rows(h_ref, c_ref, s1_ref, s2_ref, o_ref):
        @pl.when(jnp.logical_not(has_rope))
        def _():
            o_ref[...] = _dot(h_ref[...], wbf_ref[...])

        @pl.when(has_rope)
        def _():
            c, s1, s2 = c_ref[...], s1_ref[...], s2_ref[...]
            half = 2 * LANES
            for h0 in range(0, o_ref.shape[1], half):
                acc = _dot(h_ref[...], wbf_ref[:, h0:h0 + half])
                for j in range(half // LANES):
                    sl = slice(h0 + j * LANES, h0 + (j + 1) * LANES)
                    x = acc[:, j * LANES:(j + 1) * LANES]
                    rot = x * c + pltpu.roll(x, LANES - 8, 1) * s1 + pltpu.roll(x, 8, 1) * s2
                    o_ref[:, sl] = jnp.where(flag_ref[:, sl] > 0.0, rot, x)

    _dual(n_p, rows, (hp, cp, s1p, s2p, op_ref), (hs, cs, s1s, s2s, os_ref))


def _inproj(h_p, h_s, w_all, layer, flag, rope_tiles, tabs_p, tabs_s, tm, tn=IN_TN):
    (m_p, k), m_s = h_p.shape, h_s.shape[0]
    n = w_all.shape[2]
    n_p = m_p // tm
    h_specs = _row_specs(tm, m_s, n_p, k, False)
    tab_specs = _row_specs(tm, m_s, n_p, LANES, False, cycle=tabs_p[0].shape[0] // tm)
    out_specs = _row_specs(tm, m_s, n_p, tn, True)
    return pl.pallas_call(
        functools.partial(_inproj_kernel, rope_tiles=rope_tiles, n_p=n_p),
        name="inproj",
        grid=(n // tn, n_p + 1),
        in_specs=[h_specs[0]] + [tab_specs[0]] * 3 + [h_specs[1]] + [tab_specs[1]] * 3 + [
            pl.BlockSpec((None, k, tn), lambda j, i: (layer, 0, j)),
            pl.BlockSpec((1, tn), lambda j, i: (0, j)),
        ],
        out_specs=list(out_specs),
        out_shape=[jax.ShapeDtypeStruct((m_p, n), F32), jax.ShapeDtypeStruct((m_s, n), F32)],
        scratch_shapes=[pltpu.VMEM((k, tn), BF16)],
        compiler_params=_cparams(2),
    )(h_p, *tabs_p, h_s, *tabs_s, w_all, flag)


def _outproj_kernel(xp, oap, obp, ocp, xs, oas, obs, ocs, w_ref, op_ref, os_ref, wbf_ref, *, n_p):
    @pl.when(pl.program_id(1) == 0)
    def _():
        wbf_ref[...] = w_ref[...].astype(BF16)

    def rows(x_ref, oa_ref, ob_ref, oc_ref, o_ref):
        acc = _dot(oa_ref[...].astype(BF16), wbf_ref[0:A_DIM, :])
        acc += _dot(ob_ref[...].astype(BF16), wbf_ref[A_DIM:A_DIM + B_DIM, :])
        acc += _dot(oc_ref[...].astype(BF16), wbf_ref[A_DIM + B_DIM:, :])
        o_ref[...] = x_ref[...] + acc

    _dual(n_p, rows, (xp, oap, obp, ocp, op_ref), (xs, oas, obs, ocs, os_ref))


def _outproj(x, oa, ob, oc, w_all, layer, tm, tn=512):
    (m_p, d), m_s = x[0].shape, x[1].shape[0]
    k = w_all.shape[1]
    n_p = m_p // tm
    specs = [_row_specs(tm, m_s, n_p, tn, True)] + [_row_specs(tm, m_s, n_p, w, False) for w in (A_DIM, B_DIM, C_DIM)]
    return pl.pallas_call(
        functools.partial(_outproj_kernel, n_p=n_p),
        name="outproj",
        grid=(d // tn, n_p + 1),
        in_specs=[s[0] for s in specs] + [s[1] for s in specs] + [
            pl.BlockSpec((None, k, tn), lambda j, i: (layer, 0, j))],
        out_specs=list(specs[0]),
        out_shape=[jax.ShapeDtypeStruct((m_p, d), F32), jax.ShapeDtypeStruct((m_s, d), F32)],
        scratch_shapes=[pltpu.VMEM((k, tn), BF16)],
        compiler_params=_cparams(2),
    )(x[0], oa[0], ob[0], oc[0], x[1], oa[1], ob[1], oc[1], w_all)


def _ffn_up_kernel(hp, hs, wg_ref, wu_ref, op_ref, os_ref, wgbf_ref, wubf_ref, *, n_p):
    @pl.when(pl.program_id(1) == 0)
    def _():
        wgbf_ref[...] = wg_ref[...].astype(BF16)
        wubf_ref[...] = wu_ref[...].astype(BF16)

    def rows(h_ref, o_ref):
        h = h_ref[...]
        gate = _dot(h, wgbf_ref[...])
        up = _dot(h, wubf_ref[...])
        o_ref[...] = (gate * _sigmoid(gate) * up).astype(o_ref.dtype)

    _dual(n_p, rows, (hp, op_ref), (hs, os_ref))


def _ffn_up(h_p, h_s, wg_all, wu_all, layer, tm, tn=512):
    (m_p, k), m_s = h_p.shape, h_s.shape[0]
    n = wg_all.shape[2]
    n_p = m_p // tm
    w_spec = pl.BlockSpec((None, k, tn), lambda j, i: (layer, 0, j))
    return pl.pallas_call(
        functools.partial(_ffn_up_kernel, n_p=n_p),
        name="ffn_up",
        grid=(n // tn, n_p + 1),
        in_specs=list(_row_specs(tm, m_s, n_p, k, False)) + [w_spec, w_spec],
        out_specs=list(_row_specs(tm, m_s, n_p, tn, True)),
        out_shape=[jax.ShapeDtypeStruct((m_p, n), BF16), jax.ShapeDtypeStruct((m_s, n), BF16)],
        scratch_shapes=[pltpu.VMEM((k, tn), BF16), pltpu.VMEM((k, tn), BF16)],
        compiler_params=_cparams(2),
    )(h_p, h_s, wg_all, wu_all)


def _ffn_down_kernel(xp, ap, xs, as_, w_ref, op_ref, os_ref, wbf_ref, *, n_p):
    @pl.when(pl.program_id(1) == 0)
    def _():
        wbf_ref[...] = w_ref[...].astype(BF16)

    def rows(x_ref, a_ref, o_ref):
        o_ref[...] = x_ref[...] + _dot(a_ref[...], wbf_ref[...])

    _dual(n_p, rows, (xp, ap, op_ref), (xs, as_, os_ref))


def _ffn_down(x_p, x_s, act_p, act_s, w_all, layer, tm, tn=512):
    (m_p, d), m_s = x_p.shape, x_s.shape[0]
    k = w_all.shape[1]
    n_p = m_p // tm
    x_specs = _row_specs(tm, m_s, n_p, tn, True)
    a_specs = _row_specs(tm, m_s, n_p, k, False)
    return pl.pallas_call(
        functools.partial(_ffn_down_kernel, n_p=n_p),
        name="ffn_down",
        grid=(d // tn, n_p + 1),
        in_specs=[x_specs[0], a_specs[0], x_specs[1], a_specs[1],
                  pl.BlockSpec((None, k, tn), lambda j, i: (layer, 0, j))],
        out_specs=list(x_specs),
        out_shape=[jax.ShapeDtypeStruct((m_p, d), F32), jax.ShapeDtypeStruct((m_s, d), F32)],
        scratch_shapes=[pltpu.VMEM((k, tn), BF16)],
        compiler_params=_cparams(2),
    )(x_p, act_p, x_s, act_s, w_all)


def _band_mask(window, n_keys, prev_valid):
    qi = lax.broadcasted_iota(jnp.int32, (BLOCK, n_keys), 0) + (n_keys - BLOCK)
    kj = lax.broadcasted_iota(jnp.int32, (BLOCK, n_keys), 1)
    dist = qi - kj
    band = (dist >= 0) & (dist <= window)
    if n_keys > BLOCK and prev_valid is not None:
        band = band & ((kj >= n_keys - BLOCK) | prev_valid)
    return band


def _attend_pairs(tasks, lane_lo):
    scores = []
    for q2, k2, _, mask in tasks:
        q2 = q2 * (ATTN_SCALE * LOG2E)
        for hh in range(2):
            qm = jnp.where(lane_lo if hh == 0 else ~lane_lo, q2, 0.0).astype(BF16)
            s = lax.dot_general(qm, k2, _NT, preferred_element_type=F32)
            scores.append(jnp.where(mask, s, NEG))
    probs = []
    for s in scores:
        m = jnp.max(s, axis=-1, keepdims=True)
        p = jnp.exp2(s - m)
        probs.append((p.astype(BF16), m * LN2, jnp.sum(p, axis=-1, keepdims=True)))
    out = []
    for t, (_, _, v2, _) in enumerate(tasks):
        out.append([(_dot(probs[2 * t + hh][0], v2),) + probs[2 * t + hh][1:] for hh in range(2)])
    return out


def _attn_a_kernel(*refs, window, dil, qb, pairs, with_prev, chained):
    refs = list(refs)
    q_ref = refs.pop(0)
    kp_ref = refs.pop(0) if with_prev else None
    kc_ref = refs.pop(0)
    vp_ref = refs.pop(0) if with_prev else None
    vc_ref = refs.pop(0)
    op_ref, lp_ref = (refs.pop(0), refs.pop(0)) if chained else (None, None)
    o_ref, l_ref = refs
    n_keys = 2 * BLOCK if with_prev else BLOCK
    mask_first = _band_mask(window, n_keys, pl.program_id(1) > 0)
    mask_rest = _band_mask(window, n_keys, None)
    lane = lax.broadcasted_iota(jnp.int32, (BLOCK, LANES), 1)
    lane_lo = lane < HEAD

    def rows(q, r):
        if dil == 1:
            return slice(q * BLOCK, (q + 1) * BLOCK)
        return pl.ds(q * dil * BLOCK + r, BLOCK, stride=dil)

    jobs = [(q, r, pp) for q in range(qb) for r in range(dil) for pp in range(pairs)]
    for g0 in range(0, len(jobs), ATTN_GROUP):
        grp = jobs[g0:g0 + ATTN_GROUP]
        tasks = []
        for q, r, pp in grp:
            sl = slice(pp * LANES, (pp + 1) * LANES)
            cur = rows(q, r)
            if with_prev:
                kprev = kp_ref[rows(0, r), sl] if q == 0 else kc_ref[rows(q - 1, r), sl]
                vprev = vp_ref[rows(0, r), sl] if q == 0 else vc_ref[rows(q - 1, r), sl]
                k2 = jnp.concatenate([kprev, kc_ref[cur, sl]], axis=0).astype(BF16)
                v2 = jnp.concatenate([vprev, vc_ref[cur, sl]], axis=0).astype(BF16)
            else:
                k2 = kc_ref[cur, sl].astype(BF16)
                v2 = vc_ref[cur, sl].astype(BF16)
            tasks.append((q_ref[cur, sl], k2, v2, mask_first if q == 0 else mask_rest))
        for (q, r, pp), heads in zip(grp, _attend_pairs(tasks, lane_lo)):
            sl = slice(pp * LANES, (pp + 1) * LANES)
            cur = rows(q, r)
            if chained:
                o_prev = op_ref[cur, sl]
                l_prev = lp_ref[cur, sl]
            halves = []
            lse_blk = jnp.zeros((BLOCK, LANES), F32)
            for hh, (o, m, l) in enumerate(heads):
                o = o / l
                lse = m + jnp.log(l)
                if chained:
                    lse_p = jnp.sum(jnp.where(lane == hh, l_prev, 0.0), axis=-1, keepdims=True)
                    mx = jnp.maximum(lse_p, lse)
                    wp = jnp.exp(lse_p - mx)
                    wi = jnp.exp(lse - mx)
                    den = wp + wi
                    o = (o_prev * wp + o * wi) / den
                    lse = mx + jnp.log(den)
                halves.append(o)
                lse_blk = jnp.where(lane == hh, lse, lse_blk)
            o_ref[cur, sl] = jnp.where(lane_lo, halves[0], halves[1]).astype(o_ref.dtype)
            l_ref[cur, sl] = lse_blk


def _attn_a_branch(p2d, o_prev, l_prev, batch, seq, window, dil, out_dtype):
    rows = batch * seq
    unit = dil * BLOCK
    pairs = 1 if dil > 1 else A_HEADS // 2
    qb = max(1, min(seq // unit, STEP_PAIRS // (dil * pairs)))
    nblk = seq // (unit * qb)
    with_prev = seq > unit
    width = pairs * LANES
    n_col = A_DIM // width
    chained = o_prev is not None

    def cur(off):
        return lambda b, n, c: (b * nblk + n, off // width + c)

    def prev(off):
        return lambda b, n, c: (b * nblk * qb + jnp.maximum(n * qb - 1, 0), off // width + c)

    blk = (unit * qb, width)
    pblk = (unit, width)
    in_specs = [pl.BlockSpec(blk, cur(OFF_QA))]
    in_specs += [pl.BlockSpec(pblk, prev(OFF_KA))] if with_prev else []
    in_specs += [pl.BlockSpec(blk, cur(OFF_KA))]
    in_specs += [pl.BlockSpec(pblk, prev(OFF_VA))] if with_prev else []
    in_specs += [pl.BlockSpec(blk, cur(OFF_VA))]
    args = [p2d] * len(in_specs)
    o_spec = pl.BlockSpec(blk, cur(0))
    if chained:
        in_specs += [o_spec, o_spec]
        args += [o_prev, l_prev]
    o, l = pl.pallas_call(
        functools.partial(_attn_a_kernel, window=window // dil, dil=dil, qb=qb, pairs=pairs, with_prev=with_prev,
                          chained=chained),
        grid=(batch, nblk, n_col),
        in_specs=in_specs,
        out_specs=[o_spec, o_spec],
        out_shape=[jax.ShapeDtypeStruct((rows, A_DIM), out_dtype), jax.ShapeDtypeStruct((rows, A_DIM), F32)],
        compiler_params=_cparams(3),
        name=f"attn_a_dil{dil}",
    )(*args)
    return o, l


def _attn_c_kernel(sink_ref, q0_ref, q1_ref, q2_ref, q3_ref, kp_ref, kc_ref, vp_ref, vc_ref, o_ref, *, qb):
    q_refs = (q0_ref, q1_ref, q2_ref, q3_ref)
    mask_first = _band_mask(C_WINDOW, 2 * BLOCK, pl.program_id(1) > 0)
    mask_rest = _band_mask(C_WINDOW, 2 * BLOCK, None)
    lane_lo = lax.broadcasted_iota(jnp.int32, (BLOCK, LANES), 1) < HEAD
    lane_lo2 = lax.broadcasted_iota(jnp.int32, (2 * BLOCK, LANES), 1) < HEAD

    def blk_rows(q):
        return slice(q * BLOCK, (q + 1) * BLOCK)

    jobs = []
    for q in range(qb):
        k2 = jnp.concatenate([kp_ref[...] if q == 0 else kc_ref[blk_rows(q - 1), :], kc_ref[blk_rows(q), :]], axis=0)
        v2 = jnp.concatenate([vp_ref[...] if q == 0 else vc_ref[blk_rows(q - 1), :], vc_ref[blk_rows(q), :]], axis=0)
        k2r = pltpu.roll(k2, HEAD, 1)
        v2r = pltpu.roll(v2, HEAD, 1)
        kdup = [jnp.where(lane_lo2, k2, k2r).astype(BF16), jnp.where(lane_lo2, k2r, k2).astype(BF16)]
        vdup = [jnp.where(lane_lo2, v2, v2r).astype(BF16), jnp.where(lane_lo2, v2r, v2).astype(BF16)]
        for pr in range(C_Q_HEADS // 2):
            g = (2 * pr) // C_GROUP
            q2 = q_refs[pr // 2][blk_rows(q), (pr % 2) * LANES:(pr % 2 + 1) * LANES]
            jobs.append((q, pr, (q2, kdup[g], vdup[g], mask_first if q == 0 else mask_rest)))
    for g0 in range(0, len(jobs), ATTN_GROUP):
        grp = jobs[g0:g0 + ATTN_GROUP]
        for (q, pr, _), heads in zip(grp, _attend_pairs([t for _, _, t in grp], lane_lo)):
            halves = []
            for hh, (o, m, l) in enumerate(heads):
                lse = m + jnp.log(l)
                halves.append(o * (_sigmoid(lse - sink_ref[2 * pr + hh]) / l))
            o_ref[blk_rows(q), pr * LANES:(pr + 1) * LANES] = jnp.where(lane_lo, halves[0], halves[1]).astype(o_ref.dtype)


def _attn_c_prompt(p2d, sink, batch, seq):
    rows = batch * seq
    qb = max(1, min(seq // BLOCK, STEP_PAIRS // (C_Q_HEADS // 2)))
    nb = seq // (BLOCK * qb)

    def cur(col):
        return lambda b, n: (b * nb + n, col)

    def prev(col):
        return lambda b, n: (b * nb * qb + jnp.maximum(n * qb - 1, 0), col)

    kv_blk = (BLOCK * qb, C_KV_DIM)
    kv_prev = (BLOCK, C_KV_DIM)
    return pl.pallas_call(
        functools.partial(_attn_c_kernel, qb=qb),
        name="attn_c",
        grid=(batch, nb),
        in_specs=[
            pl.BlockSpec(memory_space=pltpu.SMEM),
            *[pl.BlockSpec((BLOCK * qb, QC_BLK), cur(OFF_QC // QC_BLK + i)) for i in range(C_DIM // QC_BLK)],
            pl.BlockSpec(kv_prev, prev(OFF_KC // C_KV_DIM)),
            pl.BlockSpec(kv_blk, cur(OFF_KC // C_KV_DIM)),
            pl.BlockSpec(kv_prev, prev(OFF_VC // C_KV_DIM)),
            pl.BlockSpec(kv_blk, cur(OFF_VC // C_KV_DIM)),
        ],
        out_specs=pl.BlockSpec((BLOCK * qb, C_DIM), lambda b, n: (b * nb + n, 0)),
        out_shape=jax.ShapeDtypeStruct((rows, C_DIM), BF16),
        compiler_params=_cparams(2),
    )(sink, *([p2d] * (C_DIM // QC_BLK)), p2d, p2d, p2d, p2d)


def _attn_a_sample_kernel(q_ref, kn_ref, vn_ref, kc_ref, vc_ref, cc_ref, cn_ref, o_ref):
    t = q_ref.shape[0]
    pad = jnp.zeros((BLOCK - t, A_DIM), F32)
    k_new = jnp.concatenate([kn_ref[...], pad], axis=0)
    v_new = jnp.concatenate([vn_ref[...], pad], axis=0)
    cnt_c, cnt_n = cc_ref[...], cn_ref[...]
    outs = []
    for h in range(A_HEADS):
        lanes = slice(h * HEAD, (h + 1) * HEAD)
        q = q_ref[:, lanes].astype(BF16)
        s_c = _dot(q, kc_ref[h].astype(BF16)) * ATTN_SCALE
        s_n = lax.dot_general(q, k_new[:, lanes].astype(BF16), _NT, preferred_element_type=F32) * ATTN_SCALE
        s_c = jnp.where(cnt_c > 0.0, s_c, NEG)
        s_n = jnp.where(cnt_n > 0.0, s_n, NEG)
        m = jnp.maximum(jnp.max(s_c, axis=-1, keepdims=True), jnp.max(s_n, axis=-1, keepdims=True))
        p_c = cnt_c * jnp.exp(s_c - m)
        p_n = cnt_n * jnp.exp(s_n - m)
        l = jnp.sum(p_c, axis=-1, keepdims=True) + jnp.sum(p_n, axis=-1, keepdims=True)
        o = lax.dot_general(p_c.astype(BF16), vc_ref[h].astype(BF16), _NT, preferred_element_type=F32)
        o += _dot(p_n.astype(BF16), v_new[:, lanes].astype(BF16))
        outs.append(o / l)
    o_ref[...] = jnp.concatenate(outs, axis=-1)


def _a_sample_counts(t, n_buf):
    qi = n_buf + np.arange(t)[:, None]

    def count(rows):
        delta = qi - rows[None, :]
        c = np.zeros(delta.shape, np.float32)
        for window, dil in A_BRANCHES:
            c += ((delta >= 0) & (delta <= window) & (delta % dil == 0)).astype(np.float32)
        return c

    return count(np.arange(n_buf)), count(n_buf + np.arange(BLOCK))


def _attn_a_sample(ps, cache_k, cache_v, layer, batch, t):
    n_buf = cache_k.shape[4]
    cnt_c, cnt_n = _a_sample_counts(t, n_buf)
    new_blk = (t, A_DIM)
    cache_spec = pl.BlockSpec((None, None, A_HEADS, HEAD, n_buf), lambda b: (layer, b, 0, 0, 0))
    return pl.pallas_call(
        _attn_a_sample_kernel,
        name="attn_a_sample",
        grid=(batch,),
        in_specs=[
            pl.BlockSpec(new_blk, lambda b: (b, OFF_QA // A_DIM)),
            pl.BlockSpec(new_blk, lambda b: (b, OFF_KA // A_DIM)),
            pl.BlockSpec(new_blk, lambda b: (b, OFF_VA // A_DIM)),
            cache_spec, cache_spec,
            pl.BlockSpec(cnt_c.shape, lambda b: (0, 0)),
            pl.BlockSpec(cnt_n.shape, lambda b: (0, 0)),
        ],
        out_specs=pl.BlockSpec(new_blk, lambda b: (b, 0)),
        out_shape=jax.ShapeDtypeStruct((batch * t, A_DIM), F32),
        compiler_params=_cparams(1),
    )(ps, ps, ps, cache_k, cache_v, jnp.asarray(cnt_c), jnp.asarray(cnt_n))


def _attn_c_sample_kernel(q0_ref, q1_ref, q2_ref, q3_ref, kn_ref, vn_ref, kc_ref, vc_ref, sink_ref, o_ref):
    q_refs = (q0_ref, q1_ref, q2_ref, q3_ref)
    t = q0_ref.shape[0]
    n_buf = kc_ref.shape[1]
    rows = C_Q_HEADS * t
    lane_lo = lax.broadcasted_iota(jnp.int32, (t, LANES), 1) < HEAD
    blocks = []
    for j in range(C_Q_HEADS // 2):
        chunk = q_refs[j // 2][:, (j % 2) * LANES:(j % 2 + 1) * LANES]
        rolled = pltpu.roll(chunk, HEAD, 1)
        if (2 * j) // C_GROUP == 0:
            blocks += [jnp.where(lane_lo, chunk, 0.0), jnp.where(lane_lo, rolled, 0.0)]
        else:
            blocks += [jnp.where(lane_lo, 0.0, rolled), jnp.where(lane_lo, 0.0, chunk)]
    qbd = jnp.concatenate(blocks, axis=0).astype(BF16)
    pad = jnp.zeros((BLOCK - t, C_KV_DIM), F32)
    k_new = jnp.concatenate([kn_ref[...], pad], axis=0).astype(BF16)
    v_new = jnp.concatenate([vn_ref[...], pad], axis=0).astype(BF16)
    s_c = _dot(qbd, kc_ref[...].astype(BF16)) * ATTN_SCALE
    s_n = lax.dot_general(qbd, k_new, _NT, preferred_element_type=F32) * ATTN_SCALE
    qt = lax.broadcasted_iota(jnp.int32, (rows, BLOCK), 0) % t
    kj = lax.broadcasted_iota(jnp.int32, (rows, BLOCK), 1)
    dist_c = n_buf + qt - kj
    s_c = jnp.where((dist_c >= 0) & (dist_c <= C_WINDOW), s_c, NEG)
    s_n = jnp.where(kj <= qt, s_n, NEG)
    m = jnp.maximum(jnp.max(s_c, axis=-1, keepdims=True), jnp.max(s_n, axis=-1, keepdims=True))
    p_c = jnp.exp(s_c - m)
    p_n = jnp.exp(s_n - m)
    l = jnp.sum(p_c, axis=-1, keepdims=True) + jnp.sum(p_n, axis=-1, keepdims=True)
    o = lax.dot_general(p_c.astype(BF16), vc_ref[...].astype(BF16), _NT, preferred_element_type=F32)
    o += _dot(p_n.astype(BF16), v_new)
    lse = m + jnp.log(l)
    o = o * (_sigmoid(lse - sink_ref[...]) / l)
    for j in range(C_Q_HEADS // 2):
        blk_a = o[2 * j * t:(2 * j + 1) * t, :]
        blk_b = o[(2 * j + 1) * t:(2 * j + 2) * t, :]
        if (2 * j) // C_GROUP == 0:
            out = jnp.where(lane_lo, blk_a, pltpu.roll(blk_b, HEAD, 1))
        else:
            out = jnp.where(lane_lo, pltpu.roll(blk_a, HEAD, 1), blk_b)
        o_ref[:, j * LANES:(j + 1) * LANES] = out


def _attn_c_sample(ps, cache_k, cache_v, layer, sink_col, batch, t):
    n_buf = cache_k.shape[3]
    assert n_buf == BLOCK
    kv_blk = (t, C_KV_DIM)
    cache_spec = pl.BlockSpec((None, None, C_KV_DIM, n_buf), lambda b: (layer, b, 0, 0))
    return pl.pallas_call(
        _attn_c_sample_kernel,
        name="attn_c_sample",
        grid=(batch,),
        in_specs=[
            *[pl.BlockSpec((t, QC_BLK), functools.partial(lambda b, i: (b, OFF_QC // QC_BLK + i), i=i))
              for i in range(C_DIM // QC_BLK)],
            pl.BlockSpec(kv_blk, lambda b: (b, OFF_KC // C_KV_DIM)),
            pl.BlockSpec(kv_blk, lambda b: (b, OFF_VC // C_KV_DIM)),
            cache_spec, cache_spec,
            pl.BlockSpec(sink_col.shape, lambda b: (0, 0)),
        ],
        out_specs=pl.BlockSpec((t, C_DIM), lambda b: (b, 0)),
        out_shape=jax.ShapeDtypeStruct((batch * t, C_DIM), F32),
        compiler_params=_cparams(1),
    )(*([ps] * (C_DIM // QC_BLK)), ps, ps, cache_k, cache_v, sink_col)


def _rwkv_kernel(r_ref, k_ref, v_ref, lo_ref, shift_ref, s0_ref, mu_ref, vec_ref, w2_ref, a2_ref, g2_ref,
                 o_ref, sout_ref,
                 s_scr, r_scr, d_scr, k_scr, v_scr, kk_scr, b_scr, g_scr, y_scr, wr_scr, kr_scr,
                 cx_scr, cl_scr, *, nb, tc):
    c = pl.program_id(1)
    n_pairs = B_HEADS // 2
    f32 = F32

    li2 = lax.broadcasted_iota(jnp.int32, (2 * LANES, 2 * LANES), 0) // HEAD
    lj2 = lax.broadcasted_iota(jnp.int32, (2 * LANES, 2 * LANES), 1) // HEAD
    bd2 = (li2 == lj2).astype(BF16)

    def head_sum(x):
        return jnp.concatenate([_seg_sum(x[:, j:j + 2 * LANES], bd2) for j in range(0, B_DIM, 2 * LANES)], axis=-1)

    @pl.when(c == 0)
    def _():
        for b in range(nb):
            for p in range(n_pairs):
                s_scr[b * n_pairs + p] = jnp.concatenate([s0_ref[b, 2 * p], s0_ref[b, 2 * p + 1]], axis=-1)
            cx_scr[b] = jnp.broadcast_to(shift_ref[0, b:b + 1, 0:3 * B_DIM], (SUBLANES, 3 * B_DIM))
            cl_scr[b] = jnp.broadcast_to(shift_ref[0, b:b + 1, 3 * B_DIM:], (SUBLANES, LORA_COLS))

    n_rows = nb * tc
    first_row = lax.broadcasted_iota(jnp.int32, (SUBLANES, 1), 0) == 0

    def merged(ref):
        return ref[...].reshape(n_rows, ref.shape[-1])

    def lerp(cur, carry_scr, lo_col, hi_col):
        sh = pltpu.roll(cur, 1, 0)
        parts = []
        for b in range(nb):
            head = jnp.where(first_row, carry_scr[b, 0:1, lo_col:hi_col], sh[b * tc:b * tc + SUBLANES, :])
            parts += [head] + ([sh[b * tc + SUBLANES:(b + 1) * tc, :]] if tc > SUBLANES else [])
        return jnp.concatenate(parts, axis=0)

    def keep_last(carry_scr, lo_col, hi_col, raw):
        for b in range(nb):
            carry_scr[b, :, lo_col:hi_col] = jnp.broadcast_to(raw[(b + 1) * tc - 1:(b + 1) * tc, :],
                                                             (SUBLANES, hi_col - lo_col))

    w0, a0, k_k, k_a = vec_ref[0:1, :], vec_ref[1:2, :], vec_ref[2:3, :], vec_ref[3:4, :]
    r_k, lnx_w, lnx_b = vec_ref[4:5, :], vec_ref[5:6, :], vec_ref[6:7, :]

    mixed = []
    for i, (ref, scr) in enumerate(((r_ref, cx_scr), (k_ref, cx_scr), (v_ref, cx_scr), (lo_ref, cl_scr))):
        raw = merged(ref)
        lo_col = i * B_DIM if scr is cx_scr else 0
        hi_col = lo_col + raw.shape[1]
        sh = lerp(raw, scr, lo_col, hi_col)
        keep_last(scr, lo_col, hi_col, raw)
        mixed.append(raw + (sh - raw) * mu_ref[:, i * B_DIM:i * B_DIM + raw.shape[1]])
    r, k, v, lo = mixed

    z = w0 + _dot_hi(jnp.tanh(lo), w2_ref[...])
    sp = jnp.maximum(-z, 0.0) + jnp.log(1.0 + jnp.exp(-jnp.abs(z)))
    decay = jnp.exp(-jnp.exp(-sp - 0.5))
    a = _sigmoid(a0 + _dot_hi(lo, a2_ref[...]))
    g = _dot_hi(_sigmoid(lo), g2_ref[...])
    kkr = k * k_k
    kk = kkr * lax.rsqrt(jnp.maximum(head_sum(kkr * kkr), 1e-24))
    k = k * (1.0 + (a - 1.0) * k_a)
    bb = kk * a
    wr = decay * r - kk * head_sum(bb * r)
    kr = head_sum(k * r)
    for scr, val in ((r_scr, r), (d_scr, decay), (k_scr, k), (v_scr, v), (kk_scr, kk), (b_scr, bb), (g_scr, g),
                     (wr_scr, wr), (kr_scr, kr)):
        scr[...] = val.reshape(nb, tc, B_DIM)

    sub = lax.broadcasted_iota(jnp.int32, (HEAD, LANES), 0)
    lane = lax.broadcasted_iota(jnp.int32, (HEAD, LANES), 1)
    diag = (lane % HEAD) == sub
    tok_lane = lane % HEAD
    n_all = nb * n_pairs
    grp = min(n_all, RWKV_MATMUL_PAIRS)
    groups = [list(range(g0, g0 + grp)) for g0 in range(0, n_all, grp)]

    def step8(t8, carry):
        t0 = pl.multiple_of(t8 * SUBLANES, SUBLANES)

        def rows_of(scr):
            return [scr[i // n_pairs, pl.ds(t0, SUBLANES), (i % n_pairs) * LANES:(i % n_pairs + 1) * LANES]
                    for i in range(n_all)]

        kk8, v8, d8, b8, k8, wr8, kr8 = [
            rows_of(scr) for scr in (kk_scr, v_scr, d_scr, b_scr, k_scr, wr_scr, kr_scr)]
        yb = [jnp.zeros((HEAD, LANES), f32) for _ in range(n_all)]

        vb = {}
        for grp_ids in groups:
            lhs = jnp.concatenate(
                [jnp.concatenate([jnp.where(diag, v8[i][2 * m:2 * m + 1, :], 0.0).astype(BF16),
                                  jnp.where(diag, v8[i][2 * m + 1:2 * m + 2, :], 0.0).astype(BF16)], axis=1)
                 for m in range(SUBLANES // 2) for i in grp_ids], axis=0)
            res = _dot(lhs, bd2)
            for m in range(SUBLANES // 2):
                for q, i in enumerate(grp_ids):
                    blk = res[(m * len(grp_ids) + q) * HEAD:(m * len(grp_ids) + q + 1) * HEAD, :]
                    vb[(i, 2 * m)] = blk[:, 0:LANES]
                    vb[(i, 2 * m + 1)] = blk[:, LANES:]

        for j in range(SUBLANES):
            for grp_ids in groups:
                lhs = jnp.concatenate(
                    [jnp.concatenate([s_scr[i] * kk8[i][j:j + 1, :], s_scr[i] * wr8[i][j:j + 1, :]],
                                     axis=1).astype(BF16) for i in grp_ids], axis=0)
                res = _dot(lhs, bd2)
                for q, i in enumerate(grp_ids):
                    sa = res[q * HEAD:(q + 1) * HEAD, 0:LANES]
                    u = res[q * HEAD:(q + 1) * HEAD, LANES:]
                    yb[i] = jnp.where(tok_lane == j, u, yb[i])
                    s_scr[i] = s_scr[i] * d8[i][j:j + 1, :] - sa * b8[i][j:j + 1, :] + vb[(i, j)] * k8[i][j:j + 1, :]

        for i in range(n_all):
            yt = yb[i].T
            u8 = jnp.concatenate([yt[0:SUBLANES, :], yt[HEAD:HEAD + SUBLANES, :]], axis=1)
            y_scr[i // n_pairs, pl.ds(t0, SUBLANES), (i % n_pairs) * LANES:(i % n_pairs + 1) * LANES] = (
                u8 + v8[i] * kr8[i])
        return carry

    lax.fori_loop(0, tc // SUBLANES, step8, 0)

    y = merged(y_scr)
    yc = y - head_sum(y) * (1.0 / HEAD)
    var = head_sum(yc * yc) * (1.0 / HEAD)
    yn = yc * lax.rsqrt(var + GN_EPS) * lnx_w + lnx_b
    bonus = head_sum(merged(r_scr) * merged(k_scr) * r_k) * merged(v_scr)
    o_ref[...] = ((yn + bonus) * merged(g_scr)).reshape(nb, tc, B_DIM).astype(o_ref.dtype)

    @pl.when(c == pl.num_programs(1) - 1)
    def _():
        for b in range(nb):
            for p in range(n_pairs):
                s = s_scr[b * n_pairs + p]
                sout_ref[b, 2 * p] = s[:, 0:HEAD]
                sout_ref[b, 2 * p + 1] = s[:, HEAD:]


def _rwkv(p3d, shift0, s0, mu, vecs, w2p, a2p, g2p, nb, tc, out_dtype=F32):
    batch, seq, _ = p3d.shape
    groups = batch // nb
    chunks = seq // tc
    x_blk = (nb, tc, B_DIM)

    def xmap(col):
        return lambda g, c: (g, c, col)

    const2 = lambda g, c: (0, 0)
    scr = lambda *shape: pltpu.VMEM(shape, F32)
    o, s_out = pl.pallas_call(
        functools.partial(_rwkv_kernel, nb=nb, tc=tc),
        name="rwkv7",
        grid=(groups, chunks),
        in_specs=[
            pl.BlockSpec(x_blk, xmap(OFF_PB // B_DIM)),
            pl.BlockSpec(x_blk, xmap(OFF_PB // B_DIM + 1)),
            pl.BlockSpec(x_blk, xmap(OFF_PB // B_DIM + 2)),
            pl.BlockSpec((nb, tc, LORA_COLS), xmap(OFF_LORA // LORA_COLS)),
            pl.BlockSpec((1, nb, B_COLS), lambda g, c: (g, 0, 0)),
            pl.BlockSpec((nb, B_HEADS, HEAD, HEAD), lambda g, c: (g, 0, 0, 0)),
            pl.BlockSpec((1, B_COLS), const2),
            pl.BlockSpec((SUBLANES, B_DIM), const2),
            pl.BlockSpec((LORA_COLS, B_DIM), const2),
            pl.BlockSpec((LORA_COLS, B_DIM), const2),
            pl.BlockSpec((LORA_COLS, B_DIM), const2),
        ],
        out_specs=[
            pl.BlockSpec(x_blk, lambda g, c: (g, c, 0)),
            pl.BlockSpec((nb, B_HEADS, HEAD, HEAD), lambda g, c: (g, 0, 0, 0)),
        ],
        out_shape=[jax.ShapeDtypeStruct((batch, seq, B_DIM), out_dtype),
                   jax.ShapeDtypeStruct((batch, B_HEADS, HEAD, HEAD), F32)],
        scratch_shapes=[scr(nb * B_HEADS // 2, HEAD, LANES)] + [scr(nb, tc, B_DIM)] * 10
        + [scr(nb, SUBLANES, 3 * B_DIM), scr(nb, SUBLANES, LORA_COLS)],
        compiler_params=_cparams(2),
    )(p3d, p3d, p3d, p3d, shift0.reshape(groups, nb, B_COLS), s0, mu, vecs, w2p, a2p, g2p)
    return o, s_out


def _rope_tables(pos):
    half = ROPE_DIM // 2
    inv = jnp.exp(-math.log(ROPE_THETA) * jnp.arange(half, dtype=F32) * 2.0 / ROPE_DIM)
    ang = pos.astype(F32)[:, None] * inv[None, :]
    cos, sin = jnp.cos(ang), jnp.sin(ang)
    lm = np.arange(LANES) % HEAD
    first = jnp.asarray(lm < half)[None, :]
    second = jnp.asarray((lm >= half) & (lm < ROPE_DIM))[None, :]
    freq = np.where(lm < half, lm, np.where(lm < ROPE_DIM, lm - half, 0))
    cos_l, sin_l = cos[:, freq], sin[:, freq]
    c = jnp.where(first | second, cos_l, 1.0)
    s1 = jnp.where(first, -sin_l, 0.0)
    s2 = jnp.where(second, sin_l, 0.0)
    return c, s1, s2


def _rope_flag():
    col = np.arange(IN_COLS)
    rope = (col < OFF_VA) | ((col >= OFF_QC) & (col < OFF_VC))
    tiles = tuple(int(t) for t in np.nonzero(rope.reshape(-1, IN_TN).any(axis=1))[0])
    return jnp.asarray(rope.astype(np.float32))[None, :], tiles


def _pad_rows(w, start):
    return jnp.zeros((LORA_COLS, B_DIM), F32).at[start:start + w.shape[0]].set(w)


def _mixers(p2d, batch, seq, layer, is_prompt, cache, rwkv_w, sink):
    p3d = p2d.reshape(batch, seq, IN_COLS)
    mu, vecs, w2p, a2p, g2p = rwkv_w
    if is_prompt:
        o, l = None, None
        order = sorted(A_BRANCHES, key=lambda wd: wd[1] == 1)
        for i, (window, dil) in enumerate(order):
            last = i == len(order) - 1
            o, l = _attn_a_branch(p2d, o, l, batch, seq, window, dil, BF16 if last else F32)
        oa = o
        oc = _attn_c_prompt(p2d, sink, batch, seq)
        shift0 = jnp.zeros((batch, B_COLS), F32)
        s0 = jnp.zeros((batch, B_HEADS, HEAD, HEAD), F32)
        ob, wkv = _rwkv(p3d, shift0, s0, mu, vecs, w2p, a2p, g2p, nb=batch, tc=min(seq, 128), out_dtype=BF16)
    else:
        a_k, a_v, c_k, c_v, wkv0, shift0 = cache
        oa = _attn_a_sample(p2d, a_k, a_v, layer, batch, seq)
        sink_col = jnp.repeat(sink, seq)[:, None]
        oc = _attn_c_sample(p2d, c_k, c_v, layer, sink_col, batch, seq)
        ob, wkv = _rwkv(p3d, shift0[layer], wkv0[layer], mu, vecs, w2p, a2p, g2p, nb=4, tc=seq)
    return oa, ob.reshape(batch * seq, B_DIM), oc, wkv


def kernel(x_prompt, x_sample, cache_a_k, cache_a_v, cache_c_k, cache_c_v, state_b_wkv, state_b_shift, g_mix, w_in, w_out, b_mu, b_w0, b_w2, b_a0, b_a2, b_g2, b_k_k, b_k_a, b_r_k, b_lnx_w, b_lnx_b, c_sink, g_ffn, w_gate, w_up, w_down, g_final):
    depth = w_in.shape[0]
    bp, lp, d = x_prompt.shape
    bs, ls, _ = x_sample.shape
    a_win = cache_a_k.shape[2]
    c_win = cache_c_k.shape[2]
    assert (bp * lp) % PROJ_TM == 0 and lp % PROJ_TM == 0 and lp % (16 * BLOCK) == 0 and a_win >= A_BRANCHES[-1][0] and c_win == C_WINDOW and bs % 4 == 0

    flag, rope_tiles = _rope_flag()
    tabs_p = _rope_tables(jnp.arange(lp, dtype=jnp.int32))
    tabs_s = _rope_tables(jnp.tile(PAST_LEN + jnp.arange(ls, dtype=jnp.int32), bs))
    cak = jnp.transpose(cache_a_k, (0, 1, 3, 4, 2))
    cav = jnp.transpose(cache_a_v, (0, 1, 3, 4, 2))
    cck = jnp.transpose(cache_c_k, (0, 1, 3, 4, 2)).reshape(depth, bs, C_KV_DIM, c_win)
    ccv = jnp.transpose(cache_c_v, (0, 1, 3, 4, 2)).reshape(depth, bs, C_KV_DIM, c_win)
    cache = (cak, cav, cck, ccv, state_b_wkv, state_b_shift)

    xp = x_prompt.reshape(bp * lp, d)
    xs = x_sample.reshape(bs * ls, d)
    new_p = [[] for _ in range(6)]
    new_s = [[] for _ in range(6)]
    for l in range(depth):
        vecs = jnp.stack([b_w0[l], b_a0[l], b_k_k[l], b_k_a[l], b_r_k[l], b_lnx_w[l], b_lnx_b[l],
                          jnp.zeros((B_DIM,), F32)], axis=0)
        rwkv_w = (b_mu[l][None, :], vecs, _pad_rows(b_w2[l], 0), _pad_rows(b_a2[l], 96), _pad_rows(b_g2[l], 192))
        sink = c_sink[l].reshape(C_Q_HEADS)
        hp, hs = _rmsnorm(xp, g_mix[l], BF16), _rmsnorm(xs, g_mix[l], BF16)
        pp, ps = _inproj(hp, hs, w_in, l, flag, rope_tiles, tabs_p, tabs_s, PROJ_TM)
        oa_p, ob_p, oc_p, wkv_p = _mixers(pp, bp, lp, l, True, cache, rwkv_w, sink)
        oa_s, ob_s, oc_s, wkv_s = _mixers(ps, bs, ls, l, False, cache, rwkv_w, sink)
        xp, xs = _outproj((xp, xs), (oa_p, oa_s), (ob_p, ob_s), (oc_p, oc_s), w_out, l, PROJ_TM)
        hp, hs = _rmsnorm(xp, g_ffn[l], BF16), _rmsnorm(xs, g_ffn[l], BF16)
        act_p, act_s = _ffn_up(hp, hs, w_gate, w_up, l, PROJ_TM)
        xp, xs = _ffn_down(xp, xs, act_p, act_s, w_down, l, DOWN_TM)
        for p2d, batch, seq, wkv, new, is_prompt in ((pp, bp, lp, wkv_p, new_p, True), (ps, bs, ls, wkv_s, new_s, False)):
            p3d = p2d.reshape(batch, seq, IN_COLS)
            a_keep = min(a_win, seq) if is_prompt else seq
            c_keep = min(c_win, seq) if is_prompt else seq
            state = (
                p3d[:, seq - a_keep:, OFF_KA:OFF_VA].reshape(batch, a_keep, A_HEADS, HEAD),
                p3d[:, seq - a_keep:, OFF_VA:OFF_PB].reshape(batch, a_keep, A_HEADS, HEAD),
                p3d[:, seq - c_keep:, OFF_KC:OFF_VC].reshape(batch, c_keep, C_KV_DIM // HEAD, HEAD),
                p3d[:, seq - c_keep:, OFF_VC:].reshape(batch, c_keep, C_KV_DIM // HEAD, HEAD),
                wkv,
                p3d[:, -1, OFF_PB:OFF_QC],
            )
            for i in range(6):
                new[i].append(state[i])
    y_prompt = _rmsnorm(xp, g_final, F32).reshape(bp, lp, d)
    y_sample = _rmsnorm(xs, g_final, F32).reshape(bs, ls, d)
    outs_p = [jnp.stack(t, axis=0) for t in new_p]
    outs_s = [jnp.stack(t, axis=0) for t in new_s]
    return (y_prompt, y_sample, *outs_p, *outs_s)
```

```python
import functools
import math

import numpy as np
import jax
import jax.numpy as jnp
from jax import lax
from jax.experimental import pallas as pl
from jax.experimental.pallas import tpu as pltpu

F32 = jnp.float32
BF16 = jnp.bfloat16

LANES = 128
SUBLANES = 8
VMEM_LIMIT = 52 * 1024 * 1024

D_MODEL = 2048
HEAD = 64
A_DIM = 512
B_DIM = 512
C_DIM = 1024
C_KV_DIM = 128
A_HEADS = 8
B_HEADS = 8
C_Q_HEADS = 16
C_GROUP = 8
LORA_COLS = 256
B_COLS = 3 * B_DIM + LORA_COLS
IN_COLS = 3 * A_DIM + B_COLS + C_DIM + 2 * C_KV_DIM
D_FF = 5632
OFF_QA, OFF_KA, OFF_VA = 0, A_DIM, 2 * A_DIM
OFF_PB = 3 * A_DIM
OFF_LORA = OFF_PB + 3 * B_DIM
OFF_QC = OFF_PB + B_COLS
OFF_KC = OFF_QC + C_DIM
OFF_VC = OFF_KC + C_KV_DIM
A_BRANCHES = ((128, 1), (512, 4), (2048, 16))
C_WINDOW = 128
BLOCK = 128
QC_BLK = 256
IN_TN = 512
PROJ_TM = 1024
UP_TN = 512
OUT_TM, OUT_TN = 512, 1024
DOWN_TM, DOWN_TN = 512, 512
STEP_PAIRS = 16
ATTN_GROUP = 4
RWKV_MATMUL_PAIRS = 16
PAST_LEN = 16384
ROPE_THETA = 500000.0
ROPE_DIM = 16
RMS_EPS = 1e-6
GN_EPS = 64e-5
ATTN_SCALE = HEAD ** -0.5
NEG = -1e30
LOG2E = 1.4426950408889634
LN2 = 0.6931471805599453

_NT = (((1,), (1,)), ((), ()))


def _cparams(n_grid):
    return pltpu.CompilerParams(dimension_semantics=("arbitrary",) * n_grid, vmem_limit_bytes=VMEM_LIMIT)


def _dot(a, b):
    return jnp.dot(a, b, preferred_element_type=F32)


def _split_bf16(x):
    hi = x.astype(BF16)
    lo = (x - hi.astype(F32)).astype(BF16)
    return hi, lo


def _dot_hi(a, b):
    a_hi, a_lo = _split_bf16(a)
    b_hi, b_lo = _split_bf16(b)
    return _dot(a_hi, b_hi) + (_dot(a_lo, b_hi) + _dot(a_hi, b_lo))


def _seg_sum(x, bd):
    hi, lo = _split_bf16(x)
    return _dot(hi, bd) + _dot(lo, bd)


def _sigmoid(x):
    return 1.0 / (1.0 + jnp.exp(-x))


def _rmsnorm_kernel(x_ref, g_ref, o_ref):
    x = x_ref[...]
    ms = jnp.mean(x * x, axis=-1, keepdims=True)
    o_ref[...] = (x * lax.rsqrt(ms + RMS_EPS) * g_ref[...]).astype(o_ref.dtype)


def _rmsnorm(x, g, out_dtype):
    m, d = x.shape
    tm = min(m, 512)
    return pl.pallas_call(
        _rmsnorm_kernel,
        name="rmsnorm",
        grid=(m // tm,),
        in_specs=[pl.BlockSpec((tm, d), lambda i: (i, 0)), pl.BlockSpec((1, d), lambda i: (0, 0))],
        out_specs=pl.BlockSpec((tm, d), lambda i: (i, 0)),
        out_shape=jax.ShapeDtypeStruct((m, d), out_dtype),
        compiler_params=_cparams(1),
    )(x, g.reshape(1, d))


def _row_specs(tm_p, tm_s, n_p, width, tiled, cycle=None):
    col = (lambda j: j) if tiled else (lambda j: 0)
    row = (lambda i: jnp.minimum(i, n_p - 1)) if cycle is None else (lambda i: jnp.minimum(i, n_p - 1) % cycle)
    return (pl.BlockSpec((tm_p, width), lambda j, i: (row(i), col(j))),
            pl.BlockSpec((tm_s, width), lambda j, i: (0, col(j))))


def _dual(n_p, tile_fn, prompt_refs, sample_refs):
    i = pl.program_id(1)

    @pl.when(i < n_p)
    def _():
        tile_fn(*prompt_refs)

    @pl.when(i == n_p)
    def _():
        tile_fn(*sample_refs)


def _inproj_kernel(hp, cp, s1p, s2p, hs, cs, s1s, s2s, w_ref, flag_ref, op_ref, os_ref, wbf_ref, *, rope_tiles, n_p):
    @pl.when(pl.program_id(1) == 0)
    def _():
        wbf_ref[...] = w_ref[...].astype(BF16)

    tile = pl.program_id(0)
    has_rope = functools.reduce(jnp.logical_or, [tile == t for t in rope_tiles])

    def rows(h_ref, c_ref, s1_ref, s2_ref, o_ref):
        @pl.when(jnp.logical_not(has_rope))
        def _():
            o_ref[...] = _dot(h_ref[...], wbf_ref[...])

        @pl.when(has_rope)
        def _():
            c, s1, s2 = c_ref[...], s1_ref[...], s2_ref[...]
            half = 2 * LANES
            for h0 in range(0, o_ref.shape[1], half):
                acc = _dot(h_ref[...], wbf_ref[:, h0:h0 + half])
                for j in range(half // LANES):
                    sl = slice(h0 + j * LANES, h0 + (j + 1) * LANES)
                    x = acc[:, j * LANES:(j + 1) * LANES]
                    rot = x * c + pltpu.roll(x, LANES - 8, 1) * s1 + pltpu.roll(x, 8, 1) * s2
                    o_ref[:, sl] = jnp.where(flag_ref[:, sl] > 0.0, rot, x)

    _dual(n_p, rows, (hp, cp, s1p, s2p, op_ref), (hs, cs, s1s, s2s, os_ref))


def _inproj(h_p, h_s, w_all, layer, flag, rope_tiles, tabs_p, tabs_s, tm, tn=IN_TN):
    (m_p, k), m_s = h_p.shape, h_s.shape[0]
    n = w_all.shape[2]
    n_p = m_p // tm
    h_specs = _row_specs(tm, m_s, n_p, k, False)
    tab_specs = _row_specs(tm, m_s, n_p, LANES, False, cycle=tabs_p[0].shape[0] // tm)
    out_specs = _row_specs(tm, m_s, n_p, tn, True)
    return pl.pallas_call(
        functools.partial(_inproj_kernel, rope_tiles=rope_tiles, n_p=n_p),
        name="inproj",
        grid=(n // tn, n_p + 1),
        in_specs=[h_specs[0]] + [tab_specs[0]] * 3 + [h_specs[1]] + [tab_specs[1]] * 3 + [
            pl.BlockSpec((None, k, tn), lambda j, i: (layer, 0, j)),
            pl.BlockSpec((1, tn), lambda j, i: (0, j)),
        ],
        out_specs=list(out_specs),
        out_shape=[jax.ShapeDtypeStruct((m_p, n), F32), jax.ShapeDtypeStruct((m_s, n), F32)],
        scratch_shapes=[pltpu.VMEM((k, tn), BF16)],
        compiler_params=_cparams(2),
    )(h_p, *tabs_p, h_s, *tabs_s, w_all, flag)


def _outproj_kernel(xp, oap, obp, ocp, xs, oas, obs, ocs, w_ref, op_ref, os_ref, wbf_ref, *, n_p):
    @pl.when(pl.program_id(1) == 0)
    def _():
        wbf_ref[...] = w_ref[...].astype(BF16)

    def rows(x_ref, oa_ref, ob_ref, oc_ref, o_ref):
        acc = _dot(oa_ref[...].astype(BF16), wbf_ref[0:A_DIM, :])
        acc += _dot(ob_ref[...].astype(BF16), wbf_ref[A_DIM:A_DIM + B_DIM, :])
        acc += _dot(oc_ref[...].astype(BF16), wbf_ref[A_DIM + B_DIM:, :])
        o_ref[...] = x_ref[...] + acc

    _dual(n_p, rows, (xp, oap, obp, ocp, op_ref), (xs, oas, obs, ocs, os_ref))


def _outproj(x, oa, ob, oc, w_all, layer, tm=OUT_TM, tn=OUT_TN):
    (m_p, d), m_s = x[0].shape, x[1].shape[0]
    k = w_all.shape[1]
    n_p = m_p // tm
    specs = [_row_specs(tm, m_s, n_p, tn, True)] + [_row_specs(tm, m_s, n_p, w, False) for w in (A_DIM, B_DIM, C_DIM)]
    return pl.pallas_call(
        functools.partial(_outproj_kernel, n_p=n_p),
        name="outproj",
        grid=(d // tn, n_p + 1),
        in_specs=[s[0] for s in specs] + [s[1] for s in specs] + [
            pl.BlockSpec((None, k, tn), lambda j, i: (layer, 0, j))],
        out_specs=list(specs[0]),
        out_shape=[jax.ShapeDtypeStruct((m_p, d), F32), jax.ShapeDtypeStruct((m_s, d), F32)],
        scratch_shapes=[pltpu.VMEM((k, tn), BF16)],
        compiler_params=_cparams(2),
    )(x[0], oa[0], ob[0], oc[0], x[1], oa[1], ob[1], oc[1], w_all)


def _ffn_up_kernel(hp, hs, wg_ref, wu_ref, op_ref, os_ref, wgbf_ref, wubf_ref, *, n_p):
    @pl.when(pl.program_id(1) == 0)
    def _():
        wgbf_ref[...] = wg_ref[...].astype(BF16)
        wubf_ref[...] = wu_ref[...].astype(BF16)

    def rows(h_ref, o_ref):
        h = h_ref[...]
        gate = _dot(h, wgbf_ref[...])
        up = _dot(h, wubf_ref[...])
        o_ref[...] = (gate * _sigmoid(gate) * up).astype(o_ref.dtype)

    _dual(n_p, rows, (hp, op_ref), (hs, os_ref))


def _ffn_up(h_p, h_s, wg_all, wu_all, layer, tm=PROJ_TM, tn=UP_TN):
    (m_p, k), m_s = h_p.shape, h_s.shape[0]
    n = wg_all.shape[2]
    n_p = m_p // tm
    w_spec = pl.BlockSpec((None, k, tn), lambda j, i: (layer, 0, j))
    return pl.pallas_call(
        functools.partial(_ffn_up_kernel, n_p=n_p),
        name="ffn_up",
        grid=(n // tn, n_p + 1),
        in_specs=list(_row_specs(tm, m_s, n_p, k, False)) + [w_spec, w_spec],
        out_specs=list(_row_specs(tm, m_s, n_p, tn, True)),
        out_shape=[jax.ShapeDtypeStruct((m_p, n), BF16), jax.ShapeDtypeStruct((m_s, n), BF16)],
        scratch_shapes=[pltpu.VMEM((k, tn), BF16), pltpu.VMEM((k, tn), BF16)],
        compiler_params=_cparams(2),
    )(h_p, h_s, wg_all, wu_all)


def _ffn_down_kernel(xp, ap, xs, as_, w_ref, op_ref, os_ref, wbf_ref, *, n_p):
    @pl.when(pl.program_id(1) == 0)
    def _():
        wbf_ref[...] = w_ref[...].astype(BF16)

    def rows(x_ref, a_ref, o_ref):
        o_ref[...] = x_ref[...] + _dot(a_ref[...], wbf_ref[...])

    _dual(n_p, rows, (xp, ap, op_ref), (xs, as_, os_ref))


def _ffn_down(x_p, x_s, act_p, act_s, w_all, layer, tm=DOWN_TM, tn=DOWN_TN):
    (m_p, d), m_s = x_p.shape, x_s.shape[0]
    k = w_all.shape[1]
    n_p = m_p // tm
    x_specs = _row_specs(tm, m_s, n_p, tn, True)
    a_specs = _row_specs(tm, m_s, n_p, k, False)
    return pl.pallas_call(
        functools.partial(_ffn_down_kernel, n_p=n_p),
        name="ffn_down",
        grid=(d // tn, n_p + 1),
        in_specs=[x_specs[0], a_specs[0], x_specs[1], a_specs[1],
                  pl.BlockSpec((None, k, tn), lambda j, i: (layer, 0, j))],
        out_specs=list(x_specs),
        out_shape=[jax.ShapeDtypeStruct((m_p, d), F32), jax.ShapeDtypeStruct((m_s, d), F32)],
        scratch_shapes=[pltpu.VMEM((k, tn), BF16)],
        compiler_params=_cparams(2),
    )(x_p, act_p, x_s, act_s, w_all)


def _band_mask(window, n_keys, prev_valid):
    qi = lax.broadcasted_iota(jnp.int32, (BLOCK, n_keys), 0) + (n_keys - BLOCK)
    kj = lax.broadcasted_iota(jnp.int32, (BLOCK, n_keys), 1)
    dist = qi - kj
    band = (dist >= 0) & (dist <= window)
    if n_keys > BLOCK and prev_valid is not None:
        band = band & ((kj >= n_keys - BLOCK) | prev_valid)
    return band


def _attend_pairs(tasks, lane_lo):
    scores = []
    for q2, k2, _, mask in tasks:
        q2 = q2 * (ATTN_SCALE * LOG2E)
        for hh in range(2):
            qm = jnp.where(lane_lo if hh == 0 else ~lane_lo, q2, 0.0).astype(BF16)
            s = lax.dot_general(qm, k2, _NT, preferred_element_type=F32)
            scores.append(jnp.where(mask, s, NEG))
    probs = []
    for s in scores:
        m = jnp.max(s, axis=-1, keepdims=True)
        p = jnp.exp2(s - m)
        probs.append((p.astype(BF16), m * LN2, jnp.sum(p, axis=-1, keepdims=True)))
    out = []
    for t, (_, _, v2, _) in enumerate(tasks):
        out.append([(_dot(probs[2 * t + hh][0], v2),) + probs[2 * t + hh][1:] for hh in range(2)])
    return out


def _attn_a_kernel(*refs, window, dil, qb, pairs, with_prev, chained):
    refs = list(refs)
    q_ref = refs.pop(0)
    kp_ref = refs.pop(0) if with_prev else None
    kc_ref = refs.pop(0)
    vp_ref = refs.pop(0) if with_prev else None
    vc_ref = refs.pop(0)
    op_ref, lp_ref = (refs.pop(0), refs.pop(0)) if chained else (None, None)
    o_ref, l_ref = refs
    n_keys = 2 * BLOCK if with_prev else BLOCK
    mask_first = _band_mask(window, n_keys, pl.program_id(1) > 0)
    mask_rest = _band_mask(window, n_keys, None)
    lane = lax.broadcasted_iota(jnp.int32, (BLOCK, LANES), 1)
    lane_lo = lane < HEAD

    def rows(q, r):
        if dil == 1:
            return slice(q * BLOCK, (q + 1) * BLOCK)
        return pl.ds(q * dil * BLOCK + r, BLOCK, stride=dil)

    jobs = [(q, r, pp) for q in range(qb) for r in range(dil) for pp in range(pairs)]
    for g0 in range(0, len(jobs), ATTN_GROUP):
        grp = jobs[g0:g0 + ATTN_GROUP]
        tasks = []
        for q, r, pp in grp:
            sl = slice(pp * LANES, (pp + 1) * LANES)
            cur = rows(q, r)
            if with_prev:
                kprev = kp_ref[rows(0, r), sl] if q == 0 else kc_ref[rows(q - 1, r), sl]
                vprev = vp_ref[rows(0, r), sl] if q == 0 else vc_ref[rows(q - 1, r), sl]
                k2 = jnp.concatenate([kprev, kc_ref[cur, sl]], axis=0).astype(BF16)
                v2 = jnp.concatenate([vprev, vc_ref[cur, sl]], axis=0).astype(BF16)
            else:
                k2 = kc_ref[cur, sl].astype(BF16)
                v2 = vc_ref[cur, sl].astype(BF16)
            tasks.append((q_ref[cur, sl], k2, v2, mask_first if q == 0 else mask_rest))
        for (q, r, pp), heads in zip(grp, _attend_pairs(tasks, lane_lo)):
            sl = slice(pp * LANES, (pp + 1) * LANES)
            cur = rows(q, r)
            if chained:
                o_prev = op_ref[cur, sl]
                l_prev = lp_ref[cur, sl]
            halves = []
            lse_blk = jnp.zeros((BLOCK, LANES), F32)
            for hh, (o, m, l) in enumerate(heads):
                o = o / l
                lse = m + jnp.log(l)
                if chained:
                    lse_p = jnp.sum(jnp.where(lane == hh, l_prev, 0.0), axis=-1, keepdims=True)
                    mx = jnp.maximum(lse_p, lse)
                    wp = jnp.exp(lse_p - mx)
                    wi = jnp.exp(lse - mx)
                    den = wp + wi
                    o = (o_prev * wp + o * wi) / den
                    lse = mx + jnp.log(den)
                halves.append(o)
                lse_blk = jnp.where(lane == hh, lse, lse_blk)
            o_ref[cur, sl] = jnp.where(lane_lo, halves[0], halves[1]).astype(o_ref.dtype)
            l_ref[cur, sl] = lse_blk


def _attn_a_branch(p2d, o_prev, l_prev, batch, seq, window, dil, out_dtype):
    rows = batch * seq
    unit = dil * BLOCK
    pairs = 1 if dil > 1 else A_HEADS // 2
    qb = max(1, min(seq // unit, STEP_PAIRS // (dil * pairs)))
    nblk = seq // (unit * qb)
    with_prev = seq > unit
    width = pairs * LANES
    n_col = A_DIM // width
    chained = o_prev is not None

    def cur(off):
        return lambda b, n, c: (b * nblk + n, off // width + c)

    def prev(off):
        return lambda b, n, c: (b * nblk * qb + jnp.maximum(n * qb - 1, 0), off // width + c)

    blk = (unit * qb, width)
    pblk = (unit, width)
    in_specs = [pl.BlockSpec(blk, cur(OFF_QA))]
    in_specs += [pl.BlockSpec(pblk, prev(OFF_KA))] if with_prev else []
    in_specs += [pl.BlockSpec(blk, cur(OFF_KA))]
    in_specs += [pl.BlockSpec(pblk, prev(OFF_VA))] if with_prev else []
    in_specs += [pl.BlockSpec(blk, cur(OFF_VA))]
    args = [p2d] * len(in_specs)
    o_spec = pl.BlockSpec(blk, cur(0))
    if chained:
        in_specs += [o_spec, o_spec]
        args += [o_prev, l_prev]
    o, l = pl.pallas_call(
        functools.partial(_attn_a_kernel, window=window // dil, dil=dil, qb=qb, pairs=pairs, with_prev=with_prev,
                          chained=chained),
        grid=(batch, nblk, n_col),
        in_specs=in_specs,
        out_specs=[o_spec, o_spec],
        out_shape=[jax.ShapeDtypeStruct((rows, A_DIM), out_dtype), jax.ShapeDtypeStruct((rows, A_DIM), F32)],
        compiler_params=_cparams(3),
        name=f"attn_a_dil{dil}",
    )(*args)
    return o, l


def _attn_c_kernel(sink_ref, q0_ref, q1_ref, q2_ref, q3_ref, kp_ref, kc_ref, vp_ref, vc_ref, o_ref, *, qb):
    q_refs = (q0_ref, q1_ref, q2_ref, q3_ref)
    mask_first = _band_mask(C_WINDOW, 2 * BLOCK, pl.program_id(1) > 0)
    mask_rest = _band_mask(C_WINDOW, 2 * BLOCK, None)
    lane_lo = lax.broadcasted_iota(jnp.int32, (BLOCK, LANES), 1) < HEAD
    lane_lo2 = lax.broadcasted_iota(jnp.int32, (2 * BLOCK, LANES), 1) < HEAD

    def blk_rows(q):
        return slice(q * BLOCK, (q + 1) * BLOCK)

    jobs = []
    for q in range(qb):
        k2 = jnp.concatenate([kp_ref[...] if q == 0 else kc_ref[blk_rows(q - 1), :], kc_ref[blk_rows(q), :]], axis=0)
        v2 = jnp.concatenate([vp_ref[...] if q == 0 else vc_ref[blk_rows(q - 1), :], vc_ref[blk_rows(q), :]], axis=0)
        k2r = pltpu.roll(k2, HEAD, 1)
        v2r = pltpu.roll(v2, HEAD, 1)
        kdup = [jnp.where(lane_lo2, k2, k2r).astype(BF16), jnp.where(lane_lo2, k2r, k2).astype(BF16)]
        vdup = [jnp.where(lane_lo2, v2, v2r).astype(BF16), jnp.where(lane_lo2, v2r, v2).astype(BF16)]
        for pr in range(C_Q_HEADS // 2):
            g = (2 * pr) // C_GROUP
            q2 = q_refs[pr // 2][blk_rows(q), (pr % 2) * LANES:(pr % 2 + 1) * LANES]
            jobs.append((q, pr, (q2, kdup[g], vdup[g], mask_first if q == 0 else mask_rest)))
    for g0 in range(0, len(jobs), ATTN_GROUP):
        grp = jobs[g0:g0 + ATTN_GROUP]
        for (q, pr, _), heads in zip(grp, _attend_pairs([t for _, _, t in grp], lane_lo)):
            halves = []
            for hh, (o, m, l) in enumerate(heads):
                lse = m + jnp.log(l)
                halves.append(o * (_sigmoid(lse - sink_ref[2 * pr + hh]) / l))
            o_ref[blk_rows(q), pr * LANES:(pr + 1) * LANES] = jnp.where(lane_lo, halves[0], halves[1]).astype(o_ref.dtype)


def _attn_c_prompt(p2d, sink, batch, seq):
    rows = batch * seq
    qb = max(1, min(seq // BLOCK, STEP_PAIRS // (C_Q_HEADS // 2)))
    nb = seq // (BLOCK * qb)

    def cur(col):
        return lambda b, n: (b * nb + n, col)

    def prev(col):
        return lambda b, n: (b * nb * qb + jnp.maximum(n * qb - 1, 0), col)

    kv_blk = (BLOCK * qb, C_KV_DIM)
    kv_prev = (BLOCK, C_KV_DIM)
    return pl.pallas_call(
        functools.partial(_attn_c_kernel, qb=qb),
        name="attn_c",
        grid=(batch, nb),
        in_specs=[
            pl.BlockSpec(memory_space=pltpu.SMEM),
            *[pl.BlockSpec((BLOCK * qb, QC_BLK), cur(OFF_QC // QC_BLK + i)) for i in range(C_DIM // QC_BLK)],
            pl.BlockSpec(kv_prev, prev(OFF_KC // C_KV_DIM)),
            pl.BlockSpec(kv_blk, cur(OFF_KC // C_KV_DIM)),
            pl.BlockSpec(kv_prev, prev(OFF_VC // C_KV_DIM)),
            pl.BlockSpec(kv_blk, cur(OFF_VC // C_KV_DIM)),
        ],
        out_specs=pl.BlockSpec((BLOCK * qb, C_DIM), lambda b, n: (b * nb + n, 0)),
        out_shape=jax.ShapeDtypeStruct((rows, C_DIM), BF16),
        compiler_params=_cparams(2),
    )(sink, *([p2d] * (C_DIM // QC_BLK)), p2d, p2d, p2d, p2d)


def _attn_a_sample_kernel(q_ref, kn_ref, vn_ref, kc_ref, vc_ref, cc_ref, cn_ref, o_ref):
    t = q_ref.shape[0]
    pad = jnp.zeros((BLOCK - t, A_DIM), F32)
    k_new = jnp.concatenate([kn_ref[...], pad], axis=0)
    v_new = jnp.concatenate([vn_ref[...], pad], axis=0)
    cnt_c, cnt_n = cc_ref[...], cn_ref[...]
    outs = []
    for h in range(A_HEADS):
        lanes = slice(h * HEAD, (h + 1) * HEAD)
        q = q_ref[:, lanes].astype(BF16)
        s_c = _dot(q, kc_ref[h].astype(BF16)) * ATTN_SCALE
        s_n = lax.dot_general(q, k_new[:, lanes].astype(BF16), _NT, preferred_element_type=F32) * ATTN_SCALE
        s_c = jnp.where(cnt_c > 0.0, s_c, NEG)
        s_n = jnp.where(cnt_n > 0.0, s_n, NEG)
        m = jnp.maximum(jnp.max(s_c, axis=-1, keepdims=True), jnp.max(s_n, axis=-1, keepdims=True))
        p_c = cnt_c * jnp.exp(s_c - m)
        p_n = cnt_n * jnp.exp(s_n - m)
        l = jnp.sum(p_c, axis=-1, keepdims=True) + jnp.sum(p_n, axis=-1, keepdims=True)
        o = lax.dot_general(p_c.astype(BF16), vc_ref[h].astype(BF16), _NT, preferred_element_type=F32)
        o += _dot(p_n.astype(BF16), v_new[:, lanes].astype(BF16))
        outs.append(o / l)
    o_ref[...] = jnp.concatenate(outs, axis=-1)


def _a_sample_counts(t, n_buf):
    qi = n_buf + np.arange(t)[:, None]

    def count(rows):
        delta = qi - rows[None, :]
        c = np.zeros(delta.shape, np.float32)
        for window, dil in A_BRANCHES:
            c += ((delta >= 0) & (delta <= window) & (delta % dil == 0)).astype(np.float32)
        return c

    return count(np.arange(n_buf)), count(n_buf + np.arange(BLOCK))


def _attn_a_sample(ps, cache_k, cache_v, layer, batch, t):
    n_buf = cache_k.shape[4]
    cnt_c, cnt_n = _a_sample_counts(t, n_buf)
    new_blk = (t, A_DIM)
    cache_spec = pl.BlockSpec((None, None, A_HEADS, HEAD, n_buf), lambda b: (layer, b, 0, 0, 0))
    return pl.pallas_call(
        _attn_a_sample_kernel,
        name="attn_a_sample",
        grid=(batch,),
        in_specs=[
            pl.BlockSpec(new_blk, lambda b: (b, OFF_QA // A_DIM)),
            pl.BlockSpec(new_blk, lambda b: (b, OFF_KA // A_DIM)),
            pl.BlockSpec(new_blk, lambda b: (b, OFF_VA // A_DIM)),
            cache_spec, cache_spec,
            pl.BlockSpec(cnt_c.shape, lambda b: (0, 0)),
            pl.BlockSpec(cnt_n.shape, lambda b: (0, 0)),
        ],
        out_specs=pl.BlockSpec(new_blk, lambda b: (b, 0)),
        out_shape=jax.ShapeDtypeStruct((batch * t, A_DIM), F32),
        compiler_params=_cparams(1),
    )(ps, ps, ps, cache_k, cache_v, jnp.asarray(cnt_c), jnp.asarray(cnt_n))


def _attn_c_sample_kernel(q0_ref, q1_ref, q2_ref, q3_ref, kn_ref, vn_ref, kc_ref, vc_ref, sink_ref, o_ref):
    q_refs = (q0_ref, q1_ref, q2_ref, q3_ref)
    t = q0_ref.shape[0]
    n_buf = kc_ref.shape[1]
    rows = C_Q_HEADS * t
    lane_lo = lax.broadcasted_iota(jnp.int32, (t, LANES), 1) < HEAD
    blocks = []
    for j in range(C_Q_HEADS // 2):
        chunk = q_refs[j // 2][:, (j % 2) * LANES:(j % 2 + 1) * LANES]
        rolled = pltpu.roll(chunk, HEAD, 1)
        if (2 * j) // C_GROUP == 0:
            blocks += [jnp.where(lane_lo, chunk, 0.0), jnp.where(lane_lo, rolled, 0.0)]
        else:
            blocks += [jnp.where(lane_lo, 0.0, rolled), jnp.where(lane_lo, 0.0, chunk)]
    qbd = jnp.concatenate(blocks, axis=0).astype(BF16)
    pad = jnp.zeros((BLOCK - t, C_KV_DIM), F32)
    k_new = jnp.concatenate([kn_ref[...], pad], axis=0).astype(BF16)
    v_new = jnp.concatenate([vn_ref[...], pad], axis=0).astype(BF16)
    s_c = _dot(qbd, kc_ref[...].astype(BF16)) * ATTN_SCALE
    s_n = lax.dot_general(qbd, k_new, _NT, preferred_element_type=F32) * ATTN_SCALE
    qt = lax.broadcasted_iota(jnp.int32, (rows, BLOCK), 0) % t
    kj = lax.broadcasted_iota(jnp.int32, (rows, BLOCK), 1)
    dist_c = n_buf + qt - kj
    s_c = jnp.where((dist_c >= 0) & (dist_c <= C_WINDOW), s_c, NEG)
    s_n = jnp.where(kj <= qt, s_n, NEG)
    m = jnp.maximum(jnp.max(s_c, axis=-1, keepdims=True), jnp.max(s_n, axis=-1, keepdims=True))
    p_c = jnp.exp(s_c - m)
    p_n = jnp.exp(s_n - m)
    l = jnp.sum(p_c, axis=-1, keepdims=True) + jnp.sum(p_n, axis=-1, keepdims=True)
    o = lax.dot_general(p_c.astype(BF16), vc_ref[...].astype(BF16), _NT, preferred_element_type=F32)
    o += _dot(p_n.astype(BF16), v_new)
    lse = m + jnp.log(l)
    o = o * (_sigmoid(lse - sink_ref[...]) / l)
    for j in range(C_Q_HEADS // 2):
        blk_a = o[2 * j * t:(2 * j + 1) * t, :]
        blk_b = o[(2 * j + 1) * t:(2 * j + 2) * t, :]
        if (2 * j) // C_GROUP == 0:
            out = jnp.where(lane_lo, blk_a, pltpu.roll(blk_b, HEAD, 1))
        else:
            out = jnp.where(lane_lo, pltpu.roll(blk_a, HEAD, 1), blk_b)
        o_ref[:, j * LANES:(j + 1) * LANES] = out


def _attn_c_sample(ps, cache_k, cache_v, layer, sink_col, batch, t):
    n_buf = cache_k.shape[3]
    assert n_buf == BLOCK
    kv_blk = (t, C_KV_DIM)
    cache_spec = pl.BlockSpec((None, None, C_KV_DIM, n_buf), lambda b: (layer, b, 0, 0))
    return pl.pallas_call(
        _attn_c_sample_kernel,
        name="attn_c_sample",
        grid=(batch,),
        in_specs=[
            *[pl.BlockSpec((t, QC_BLK), functools.partial(lambda b, i: (b, OFF_QC // QC_BLK + i), i=i))
              for i in range(C_DIM // QC_BLK)],
            pl.BlockSpec(kv_blk, lambda b: (b, OFF_KC // C_KV_DIM)),
            pl.BlockSpec(kv_blk, lambda b: (b, OFF_VC // C_KV_DIM)),
            cache_spec, cache_spec,
            pl.BlockSpec(sink_col.shape, lambda b: (0, 0)),
        ],
        out_specs=pl.BlockSpec((t, C_DIM), lambda b: (b, 0)),
        out_shape=jax.ShapeDtypeStruct((batch * t, C_DIM), F32),
        compiler_params=_cparams(1),
    )(*([ps] * (C_DIM // QC_BLK)), ps, ps, cache_k, cache_v, sink_col)


def _rwkv_kernel(r_ref, k_ref, v_ref, lo_ref, shift_ref, s0_ref, mu_ref, vec_ref, w2_ref, a2_ref, g2_ref,
                 o_ref, sout_ref,
                 s_scr, r_scr, d_scr, k_scr, v_scr, kk_scr, b_scr, g_scr, y_scr, wr_scr, kr_scr,
                 cx_scr, cl_scr, *, nb, tc):
    c = pl.program_id(1)
    n_pairs = B_HEADS // 2
    f32 = F32

    li2 = lax.broadcasted_iota(jnp.int32, (2 * LANES, 2 * LANES), 0) // HEAD
    lj2 = lax.broadcasted_iota(jnp.int32, (2 * LANES, 2 * LANES), 1) // HEAD
    bd2 = (li2 == lj2).astype(BF16)

    def head_sum(x):
        return jnp.concatenate([_seg_sum(x[:, j:j + 2 * LANES], bd2) for j in range(0, B_DIM, 2 * LANES)], axis=-1)

    @pl.when(c == 0)
    def _():
        for b in range(nb):
            for p in range(n_pairs):
                s_scr[b * n_pairs + p] = jnp.concatenate([s0_ref[b, 2 * p], s0_ref[b, 2 * p + 1]], axis=-1)
            cx_scr[b] = jnp.broadcast_to(shift_ref[0, b:b + 1, 0:3 * B_DIM], (SUBLANES, 3 * B_DIM))
            cl_scr[b] = jnp.broadcast_to(shift_ref[0, b:b + 1, 3 * B_DIM:], (SUBLANES, LORA_COLS))

    n_rows = nb * tc
    first_row = lax.broadcasted_iota(jnp.int32, (SUBLANES, 1), 0) == 0

    def merged(ref):
        return ref[...].reshape(n_rows, ref.shape[-1])

    def lerp(cur, carry_scr, lo_col, hi_col):
        sh = pltpu.roll(cur, 1, 0)
        parts = []
        for b in range(nb):
            head = jnp.where(first_row, carry_scr[b, 0:1, lo_col:hi_col], sh[b * tc:b * tc + SUBLANES, :])
            parts += [head] + ([sh[b * tc + SUBLANES:(b + 1) * tc, :]] if tc > SUBLANES else [])
        return jnp.concatenate(parts, axis=0)

    def keep_last(carry_scr, lo_col, hi_col, raw):
        for b in range(nb):
            carry_scr[b, :, lo_col:hi_col] = jnp.broadcast_to(raw[(b + 1) * tc - 1:(b + 1) * tc, :],
                                                             (SUBLANES, hi_col - lo_col))

    w0, a0, k_k, k_a = vec_ref[0:1, :], vec_ref[1:2, :], vec_ref[2:3, :], vec_ref[3:4, :]
    r_k, lnx_w, lnx_b = vec_ref[4:5, :], vec_ref[5:6, :], vec_ref[6:7, :]

    mixed = []
    for i, (ref, scr) in enumerate(((r_ref, cx_scr), (k_ref, cx_scr), (v_ref, cx_scr), (lo_ref, cl_scr))):
        raw = merged(ref)
        lo_col = i * B_DIM if scr is cx_scr else 0
        hi_col = lo_col + raw.shape[1]
        sh = lerp(raw, scr, lo_col, hi_col)
        keep_last(scr, lo_col, hi_col, raw)
        mixed.append(raw + (sh - raw) * mu_ref[:, i * B_DIM:i * B_DIM + raw.shape[1]])
    r, k, v, lo = mixed

    z = w0 + _dot_hi(jnp.tanh(lo), w2_ref[...])
    sp = jnp.maximum(-z, 0.0) + jnp.log(1.0 + jnp.exp(-jnp.abs(z)))
    decay = jnp.exp(-jnp.exp(-sp - 0.5))
    a = _sigmoid(a0 + _dot_hi(lo, a2_ref[...]))
    g = _dot_hi(_sigmoid(lo), g2_ref[...])
    kkr = k * k_k
    kk = kkr * lax.rsqrt(jnp.maximum(head_sum(kkr * kkr), 1e-24))
    k = k * (1.0 + (a - 1.0) * k_a)
    bb = kk * a
    wr = decay * r - kk * head_sum(bb * r)
    kr = head_sum(k * r)
    for scr, val in ((r_scr, r), (d_scr, decay), (k_scr, k), (v_scr, v), (kk_scr, kk), (b_scr, bb), (g_scr, g),
                     (wr_scr, wr), (kr_scr, kr)):
        scr[...] = val.reshape(nb, tc, B_DIM)

    sub = lax.broadcasted_iota(jnp.int32, (HEAD, LANES), 0)
    lane = lax.broadcasted_iota(jnp.int32, (HEAD, LANES), 1)
    diag = (lane % HEAD) == sub
    tok_lane = lane % HEAD
    n_all = nb * n_pairs
    grp = min(n_all, RWKV_MATMUL_PAIRS)
    groups = [list(range(g0, g0 + grp)) for g0 in range(0, n_all, grp)]

    def step8(t8, carry):
        t0 = pl.multiple_of(t8 * SUBLANES, SUBLANES)

        def rows_of(scr):
            return [scr[i // n_pairs, pl.ds(t0, SUBLANES), (i % n_pairs) * LANES:(i % n_pairs + 1) * LANES]
                    for i in range(n_all)]

        kk8, v8, d8, b8, k8, wr8, kr8 = [
            rows_of(scr) for scr in (kk_scr, v_scr, d_scr, b_scr, k_scr, wr_scr, kr_scr)]
        yb = [jnp.zeros((HEAD, LANES), f32) for _ in range(n_all)]

        vb = {}
        pairs_mm = range(SUBLANES // 2)
        for grp_ids in groups:
            lhs = jnp.concatenate(
                [jnp.concatenate([jnp.where(diag, v8[i][2 * m:2 * m + 1, :], 0.0).astype(BF16),
                                  jnp.where(diag, v8[i][2 * m + 1:2 * m + 2, :], 0.0).astype(BF16)], axis=1)
                 for m in pairs_mm for i in grp_ids], axis=0)
            res = _dot(lhs, bd2)
            for mi, m in enumerate(pairs_mm):
                for q, i in enumerate(grp_ids):
                    blk = res[(mi * len(grp_ids) + q) * HEAD:(mi * len(grp_ids) + q + 1) * HEAD, :]
                    vb[(i, 2 * m)] = blk[:, 0:LANES]
                    vb[(i, 2 * m + 1)] = blk[:, LANES:]

        for j in range(SUBLANES):
            for grp_ids in groups:
                lhs = jnp.concatenate(
                    [jnp.concatenate([s_scr[i] * kk8[i][j:j + 1, :], s_scr[i] * wr8[i][j:j + 1, :]],
                                     axis=1).astype(BF16) for i in grp_ids], axis=0)
                res = _dot(lhs, bd2)
                for q, i in enumerate(grp_ids):
                    sa = res[q * HEAD:(q + 1) * HEAD, 0:LANES]
                    u = res[q * HEAD:(q + 1) * HEAD, LANES:]
                    yb[i] = jnp.where(tok_lane == j, u, yb[i])
                    s_scr[i] = s_scr[i] * d8[i][j:j + 1, :] - sa * b8[i][j:j + 1, :] + vb[(i, j)] * k8[i][j:j + 1, :]

        for i in range(n_all):
            yt = yb[i].T
            u8 = jnp.concatenate([yt[0:SUBLANES, :], yt[HEAD:HEAD + SUBLANES, :]], axis=1)
            y_scr[i // n_pairs, pl.ds(t0, SUBLANES), (i % n_pairs) * LANES:(i % n_pairs + 1) * LANES] = (
                u8 + v8[i] * kr8[i])
        return carry

    lax.fori_loop(0, tc // SUBLANES, step8, 0, unroll=4 if tc // SUBLANES >= 4 else 1)

    y = merged(y_scr)
    yc = y - head_sum(y) * (1.0 / HEAD)
    var = head_sum(yc * yc) * (1.0 / HEAD)
    yn = yc * lax.rsqrt(var + GN_EPS) * lnx_w + lnx_b
    bonus = head_sum(merged(r_scr) * merged(k_scr) * r_k) * merged(v_scr)
    o_ref[...] = ((yn + bonus) * merged(g_scr)).reshape(nb, tc, B_DIM).astype(o_ref.dtype)

    @pl.when(c == pl.num_programs(1) - 1)
    def _():
        for b in range(nb):
            for p in range(n_pairs):
                s = s_scr[b * n_pairs + p]
                sout_ref[b, 2 * p] = s[:, 0:HEAD]
                sout_ref[b, 2 * p + 1] = s[:, HEAD:]


def _rwkv(p3d, shift0, s0, mu, vecs, w2p, a2p, g2p, nb, tc, out_dtype=F32):
    batch, seq, _ = p3d.shape
    groups = batch // nb
    chunks = seq // tc
    x_blk = (nb, tc, B_DIM)

    def xmap(col):
        return lambda g, c: (g, c, col)

    const2 = lambda g, c: (0, 0)
    scr = lambda *shape: pltpu.VMEM(shape, F32)
    o, s_out = pl.pallas_call(
        functools.partial(_rwkv_kernel, nb=nb, tc=tc),
        name="rwkv7",
        grid=(groups, chunks),
        in_specs=[
            pl.BlockSpec(x_blk, xmap(OFF_PB // B_DIM)),
            pl.BlockSpec(x_blk, xmap(OFF_PB // B_DIM + 1)),
            pl.BlockSpec(x_blk, xmap(OFF_PB // B_DIM + 2)),
            pl.BlockSpec((nb, tc, LORA_COLS), xmap(OFF_LORA // LORA_COLS)),
            pl.BlockSpec((1, nb, B_COLS), lambda g, c: (g, 0, 0)),
            pl.BlockSpec((nb, B_HEADS, HEAD, HEAD), lambda g, c: (g, 0, 0, 0)),
            pl.BlockSpec((1, B_COLS), const2),
            pl.BlockSpec((SUBLANES, B_DIM), const2),
            pl.BlockSpec((LORA_COLS, B_DIM), const2),
            pl.BlockSpec((LORA_COLS, B_DIM), const2),
            pl.BlockSpec((LORA_COLS, B_DIM), const2),
        ],
        out_specs=[
            pl.BlockSpec(x_blk, lambda g, c: (g, c, 0)),
            pl.BlockSpec((nb, B_HEADS, HEAD, HEAD), lambda g, c: (g, 0, 0, 0)),
        ],
        out_shape=[jax.ShapeDtypeStruct((batch, seq, B_DIM), out_dtype),
                   jax.ShapeDtypeStruct((batch, B_HEADS, HEAD, HEAD), F32)],
        scratch_shapes=[scr(nb * B_HEADS // 2, HEAD, LANES)] + [scr(nb, tc, B_DIM)] * 10
        + [scr(nb, SUBLANES, 3 * B_DIM), scr(nb, SUBLANES, LORA_COLS)],
        compiler_params=_cparams(2),
    )(p3d, p3d, p3d, p3d, shift0.reshape(groups, nb, B_COLS), s0, mu, vecs, w2p, a2p, g2p)
    return o, s_out


def _rope_tables(pos):
    half = ROPE_DIM // 2
    inv = jnp.exp(-math.log(ROPE_THETA) * jnp.arange(half, dtype=F32) * 2.0 / ROPE_DIM)
    ang = pos.astype(F32)[:, None] * inv[None, :]
    cos, sin = jnp.cos(ang), jnp.sin(ang)
    lm = np.arange(LANES) % HEAD
    first = jnp.asarray(lm < half)[None, :]
    second = jnp.asarray((lm >= half) & (lm < ROPE_DIM))[None, :]
    freq = np.where(lm < half, lm, np.where(lm < ROPE_DIM, lm - half, 0))
    cos_l, sin_l = cos[:, freq], sin[:, freq]
    c = jnp.where(first | second, cos_l, 1.0)
    s1 = jnp.where(first, -sin_l, 0.0)
    s2 = jnp.where(second, sin_l, 0.0)
    return c, s1, s2


def _rope_flag():
    col = np.arange(IN_COLS)
    rope = (col < OFF_VA) | ((col >= OFF_QC) & (col < OFF_VC))
    tiles = tuple(int(t) for t in np.nonzero(rope.reshape(-1, IN_TN).any(axis=1))[0])
    return jnp.asarray(rope.astype(np.float32))[None, :], tiles


def _pad_rows(w, start):
    return jnp.zeros((LORA_COLS, B_DIM), F32).at[start:start + w.shape[0]].set(w)


def _mixers(p2d, batch, seq, layer, is_prompt, cache, rwkv_w, sink):
    p3d = p2d.reshape(batch, seq, IN_COLS)
    mu, vecs, w2p, a2p, g2p = rwkv_w
    if is_prompt:
        o, l = None, None
        order = sorted(A_BRANCHES, key=lambda wd: wd[1] == 1)
        for i, (window, dil) in enumerate(order):
            last = i == len(order) - 1
            o, l = _attn_a_branch(p2d, o, l, batch, seq, window, dil, BF16 if last else F32)
        oa = o
        oc = _attn_c_prompt(p2d, sink, batch, seq)
        shift0 = jnp.zeros((batch, B_COLS), F32)
        s0 = jnp.zeros((batch, B_HEADS, HEAD, HEAD), F32)
        ob, wkv = _rwkv(p3d, shift0, s0, mu, vecs, w2p, a2p, g2p, nb=batch, tc=min(seq, 128), out_dtype=BF16)
    else:
        a_k, a_v, c_k, c_v, wkv0, shift0 = cache
        oa = _attn_a_sample(p2d, a_k, a_v, layer, batch, seq)
        sink_col = jnp.repeat(sink, seq)[:, None]
        oc = _attn_c_sample(p2d, c_k, c_v, layer, sink_col, batch, seq)
        ob, wkv = _rwkv(p3d, shift0[layer], wkv0[layer], mu, vecs, w2p, a2p, g2p, nb=4, tc=seq)
    return oa, ob.reshape(batch * seq, B_DIM), oc, wkv


def kernel(x_prompt, x_sample, cache_a_k, cache_a_v, cache_c_k, cache_c_v, state_b_wkv, state_b_shift, g_mix, w_in, w_out, b_mu, b_w0, b_w2, b_a0, b_a2, b_g2, b_k_k, b_k_a, b_r_k, b_lnx_w, b_lnx_b, c_sink, g_ffn, w_gate, w_up, w_down, g_final):
    depth = w_in.shape[0]
    bp, lp, d = x_prompt.shape
    bs, ls, _ = x_sample.shape
    a_win = cache_a_k.shape[2]
    c_win = cache_c_k.shape[2]
    assert (bp * lp) % PROJ_TM == 0 and lp % PROJ_TM == 0 and lp % (16 * BLOCK) == 0 and a_win >= A_BRANCHES[-1][0] and c_win == C_WINDOW and bs % 4 == 0

    flag, rope_tiles = _rope_flag()
    tabs_p = _rope_tables(jnp.arange(lp, dtype=jnp.int32))
    tabs_s = _rope_tables(jnp.tile(PAST_LEN + jnp.arange(ls, dtype=jnp.int32), bs))
    cak = jnp.transpose(cache_a_k, (0, 1, 3, 4, 2))
    cav = jnp.transpose(cache_a_v, (0, 1, 3, 4, 2))
    cck = jnp.transpose(cache_c_k, (0, 1, 3, 4, 2)).reshape(depth, bs, C_KV_DIM, c_win)
    ccv = jnp.transpose(cache_c_v, (0, 1, 3, 4, 2)).reshape(depth, bs, C_KV_DIM, c_win)
    cache = (cak, cav, cck, ccv, state_b_wkv, state_b_shift)

    xp = x_prompt.reshape(bp * lp, d)
    xs = x_sample.reshape(bs * ls, d)
    new_p = [[] for _ in range(6)]
    new_s = [[] for _ in range(6)]
    for l in range(depth):
        vecs = jnp.stack([b_w0[l], b_a0[l], b_k_k[l], b_k_a[l], b_r_k[l], b_lnx_w[l], b_lnx_b[l],
                          jnp.zeros((B_DIM,), F32)], axis=0)
        rwkv_w = (b_mu[l][None, :], vecs, _pad_rows(b_w2[l], 0), _pad_rows(b_a2[l], 96), _pad_rows(b_g2[l], 192))
        sink = c_sink[l].reshape(C_Q_HEADS)
        hp, hs = _rmsnorm(xp, g_mix[l], BF16), _rmsnorm(xs, g_mix[l], BF16)
        pp, ps = _inproj(hp, hs, w_in, l, flag, rope_tiles, tabs_p, tabs_s, PROJ_TM)
        oa_p, ob_p, oc_p, wkv_p = _mixers(pp, bp, lp, l, True, cache, rwkv_w, sink)
        oa_s, ob_s, oc_s, wkv_s = _mixers(ps, bs, ls, l, False, cache, rwkv_w, sink)
        xp, xs = _outproj((xp, xs), (oa_p, oa_s), (ob_p, ob_s), (oc_p, oc_s), w_out, l)
        hp, hs = _rmsnorm(xp, g_ffn[l], BF16), _rmsnorm(xs, g_ffn[l], BF16)
        act_p, act_s = _ffn_up(hp, hs, w_gate, w_up, l)
        xp, xs = _ffn_down(xp, xs, act_p, act_s, w_down, l)
        for p2d, batch, seq, wkv, new, is_prompt in ((pp, bp, lp, wkv_p, new_p, True), (ps, bs, ls, wkv_s, new_s, False)):
            p3d = p2d.reshape(batch, seq, IN_COLS)
            a_keep = min(a_win, seq) if is_prompt else seq
            c_keep = min(c_win, seq) if is_prompt else seq
            state = (
                p3d[:, seq - a_keep:, OFF_KA:OFF_VA].reshape(batch, a_keep, A_HEADS, HEAD),
                p3d[:, seq - a_keep:, OFF_VA:OFF_PB].reshape(batch, a_keep, A_HEADS, HEAD),
                p3d[:, seq - c_keep:, OFF_KC:OFF_VC].reshape(batch, c_keep, C_KV_DIM // HEAD, HEAD),
                p3d[:, seq - c_keep:, OFF_VC:].reshape(batch, c_keep, C_KV_DIM // HEAD, HEAD),
                wkv,
                p3d[:, -1, OFF_PB:OFF_QC],
            )
            for i in range(6):
                new[i].append(state[i])
    y_prompt = _rmsnorm(xp, g_final, F32).reshape(bp, lp, d)
    y_sample = _rmsnorm(xs, g_final, F32).reshape(bs, ls, d)
    outs_p = [jnp.stack(t, axis=0) for t in new_p]
    outs_s = [jnp.stack(t, axis=0) for t in new_s]
    return (y_prompt, y_sample, *outs_p, *outs_s)
```

```python
import functools
import math

import numpy as np
import jax
import jax.numpy as jnp
from jax import lax
from jax.experimental import pallas as pl
from jax.experimental.pallas import tpu as pltpu

F32 = jnp.float32
BF16 = jnp.bfloat16

LANES = 128
SUBLANES = 8
VMEM_LIMIT = 52 * 1024 * 1024

D_MODEL = 2048
HEAD = 64
A_DIM = 512
B_DIM = 512
C_DIM = 1024
C_KV_DIM = 128
A_HEADS = 8
B_HEADS = 8
C_Q_HEADS = 16
C_GROUP = 8
LORA_COLS = 256
B_COLS = 3 * B_DIM + LORA_COLS
IN_COLS = 3 * A_DIM + B_COLS + C_DIM + 2 * C_KV_DIM
D_FF = 5632
OFF_QA, OFF_KA, OFF_VA = 0, A_DIM, 2 * A_DIM
OFF_PB = 3 * A_DIM
OFF_LORA = OFF_PB + 3 * B_DIM
OFF_QC = OFF_PB + B_COLS
OFF_KC = OFF_QC + C_DIM
OFF_VC = OFF_KC + C_KV_DIM
A_BRANCHES = ((128, 1), (512, 4), (2048, 16))
C_WINDOW = 128
BLOCK = 128
QC_BLK = 256
IN_TN = 512
PROJ_TM = 1024
UP_TN = 512
OUT_TM, OUT_TN = 512, 1024
DOWN_TM, DOWN_TN = 512, 512
STEP_PAIRS = 16
ATTN_GROUP = 4
ATTN_GROUP_MERGING = 2
RWKV_MATMUL_PAIRS = 16
PAST_LEN = 16384
ROPE_THETA = 500000.0
ROPE_DIM = 16
RMS_EPS = 1e-6
GN_EPS = 64e-5
ATTN_SCALE = HEAD ** -0.5
NEG = -1e30
LOG2E = 1.4426950408889634
LN2 = 0.6931471805599453

_NT = (((1,), (1,)), ((), ()))


def _cparams(n_grid):
    return pltpu.CompilerParams(dimension_semantics=("arbitrary",) * n_grid, vmem_limit_bytes=VMEM_LIMIT)


def _dot(a, b):
    return jnp.dot(a, b, preferred_element_type=F32)


def _split_bf16(x):
    hi = x.astype(BF16)
    lo = (x - hi.astype(F32)).astype(BF16)
    return hi, lo


def _dot_hi(a, b):
    a_hi, a_lo = _split_bf16(a)
    b_hi, b_lo = _split_bf16(b)
    return _dot(a_hi, b_hi) + (_dot(a_lo, b_hi) + _dot(a_hi, b_lo))


def _seg_sum(x, bd):
    hi, lo = _split_bf16(x)
    return _dot(hi, bd) + _dot(lo, bd)


def _sigmoid(x):
    return 1.0 / (1.0 + jnp.exp(-x))


def _rmsnorm_kernel(x_ref, g_ref, o_ref):
    x = x_ref[...]
    ms = jnp.mean(x * x, axis=-1, keepdims=True)
    o_ref[...] = (x * lax.rsqrt(ms + RMS_EPS) * g_ref[...]).astype(o_ref.dtype)


def _rmsnorm(x, g, out_dtype):
    m, d = x.shape
    tm = min(m, 512)
    return pl.pallas_call(
        _rmsnorm_kernel,
        name="rmsnorm",
        grid=(m // tm,),
        in_specs=[pl.BlockSpec((tm, d), lambda i: (i, 0)), pl.BlockSpec((1, d), lambda i: (0, 0))],
        out_specs=pl.BlockSpec((tm, d), lambda i: (i, 0)),
        out_shape=jax.ShapeDtypeStruct((m, d), out_dtype),
        compiler_params=_cparams(1),
    )(x, g.reshape(1, d))


def _row_specs(tm_p, tm_s, n_p, width, tiled, cycle=None):
    col = (lambda j: j) if tiled else (lambda j: 0)
    row = (lambda i: jnp.minimum(i, n_p - 1)) if cycle is None else (lambda i: jnp.minimum(i, n_p - 1) % cycle)
    return (pl.BlockSpec((tm_p, width), lambda j, i: (row(i), col(j))),
            pl.BlockSpec((tm_s, width), lambda j, i: (0, col(j))))


def _dual(n_p, tile_fn, prompt_refs, sample_refs):
    i = pl.program_id(1)

    @pl.when(i < n_p)
    def _():
        tile_fn(*prompt_refs)

    @pl.when(i == n_p)
    def _():
        tile_fn(*sample_refs)


def _inproj_kernel(hp, cp, s1p, s2p, hs, cs, s1s, s2s, w_ref, flag_ref, op_ref, os_ref, wbf_ref, *, rope_tiles, n_p):
    @pl.when(pl.program_id(1) == 0)
    def _():
        wbf_ref[...] = w_ref[...].astype(BF16)

    tile = pl.program_id(0)
    has_rope = functools.reduce(jnp.logical_or, [tile == t for t in rope_tiles])

    def rows(h_ref, c_ref, s1_ref, s2_ref, o_ref):
        @pl.when(jnp.logical_not(has_rope))
        def _():
            o_ref[...] = _dot(h_ref[...], wbf_ref[...])

        @pl.when(has_rope)
        def _():
            c, s1, s2 = c_ref[...], s1_ref[...], s2_ref[...]
            half = 2 * LANES
            for h0 in range(0, o_ref.shape[1], half):
                acc = _dot(h_ref[...], wbf_ref[:, h0:h0 + half])
                for j in range(half // LANES):
                    sl = slice(h0 + j * LANES, h0 + (j + 1) * LANES)
                    x = acc[:, j * LANES:(j + 1) * LANES]
                    rot = x * c + pltpu.roll(x, LANES - 8, 1) * s1 + pltpu.roll(x, 8, 1) * s2
                    o_ref[:, sl] = jnp.where(flag_ref[:, sl] > 0.0, rot, x)

    _dual(n_p, rows, (hp, cp, s1p, s2p, op_ref), (hs, cs, s1s, s2s, os_ref))


def _inproj(h_p, h_s, w_all, layer, flag, rope_tiles, tabs_p, tabs_s, tm, tn=IN_TN):
    (m_p, k), m_s = h_p.shape, h_s.shape[0]
    n = w_all.shape[2]
    n_p = m_p // tm
    h_specs = _row_specs(tm, m_s, n_p, k, False)
    tab_specs = _row_specs(tm, m_s, n_p, LANES, False, cycle=tabs_p[0].shape[0] // tm)
    out_specs = _row_specs(tm, m_s, n_p, tn, True)
    return pl.pallas_call(
        functools.partial(_inproj_kernel, rope_tiles=rope_tiles, n_p=n_p),
        name="inproj",
        grid=(n // tn, n_p + 1),
        in_specs=[h_specs[0]] + [tab_specs[0]] * 3 + [h_specs[1]] + [tab_specs[1]] * 3 + [
            pl.BlockSpec((None, k, tn), lambda j, i: (layer, 0, j)),
            pl.BlockSpec((1, tn), lambda j, i: (0, j)),
        ],
        out_specs=list(out_specs),
        out_shape=[jax.ShapeDtypeStruct((m_p, n), F32), jax.ShapeDtypeStruct((m_s, n), F32)],
        scratch_shapes=[pltpu.VMEM((k, tn), BF16)],
        compiler_params=_cparams(2),
    )(h_p, *tabs_p, h_s, *tabs_s, w_all, flag)


def _outproj_kernel(xp, oap, obp, ocp, xs, oas, obs, ocs, w_ref, op_ref, os_ref, wbf_ref, *, n_p):
    @pl.when(pl.program_id(1) == 0)
    def _():
        wbf_ref[...] = w_ref[...].astype(BF16)

    def rows(x_ref, oa_ref, ob_ref, oc_ref, o_ref):
        acc = _dot(oa_ref[...].astype(BF16), wbf_ref[0:A_DIM, :])
        acc += _dot(ob_ref[...].astype(BF16), wbf_ref[A_DIM:A_DIM + B_DIM, :])
        acc += _dot(oc_ref[...].astype(BF16), wbf_ref[A_DIM + B_DIM:, :])
        o_ref[...] = x_ref[...] + acc

    _dual(n_p, rows, (xp, oap, obp, ocp, op_ref), (xs, oas, obs, ocs, os_ref))


def _outproj(x, oa, ob, oc, w_all, layer, tm=OUT_TM, tn=OUT_TN):
    (m_p, d), m_s = x[0].shape, x[1].shape[0]
    k = w_all.shape[1]
    n_p = m_p // tm
    specs = [_row_specs(tm, m_s, n_p, tn, True)] + [_row_specs(tm, m_s, n_p, w, False) for w in (A_DIM, B_DIM, C_DIM)]
    return pl.pallas_call(
        functools.partial(_outproj_kernel, n_p=n_p),
        name="outproj",
        grid=(d // tn, n_p + 1),
        in_specs=[s[0] for s in specs] + [s[1] for s in specs] + [
            pl.BlockSpec((None, k, tn), lambda j, i: (layer, 0, j))],
        out_specs=list(specs[0]),
        out_shape=[jax.ShapeDtypeStruct((m_p, d), F32), jax.ShapeDtypeStruct((m_s, d), F32)],
        scratch_shapes=[pltpu.VMEM((k, tn), BF16)],
        compiler_params=_cparams(2),
    )(x[0], oa[0], ob[0], oc[0], x[1], oa[1], ob[1], oc[1], w_all)


def _ffn_up_kernel(hp, hs, wg_ref, wu_ref, op_ref, os_ref, wgbf_ref, wubf_ref, *, n_p):
    @pl.when(pl.program_id(1) == 0)
    def _():
        wgbf_ref[...] = wg_ref[...].astype(BF16)
        wubf_ref[...] = wu_ref[...].astype(BF16)

    def rows(h_ref, o_ref):
        h = h_ref[...]
        gate = _dot(h, wgbf_ref[...])
        up = _dot(h, wubf_ref[...])
        o_ref[...] = (gate * _sigmoid(gate) * up).astype(o_ref.dtype)

    _dual(n_p, rows, (hp, op_ref), (hs, os_ref))


def _ffn_up(h_p, h_s, wg_all, wu_all, layer, tm=PROJ_TM, tn=UP_TN):
    (m_p, k), m_s = h_p.shape, h_s.shape[0]
    n = wg_all.shape[2]
    n_p = m_p // tm
    w_spec = pl.BlockSpec((None, k, tn), lambda j, i: (layer, 0, j))
    return pl.pallas_call(
        functools.partial(_ffn_up_kernel, n_p=n_p),
        name="ffn_up",
        grid=(n // tn, n_p + 1),
        in_specs=list(_row_specs(tm, m_s, n_p, k, False)) + [w_spec, w_spec],
        out_specs=list(_row_specs(tm, m_s, n_p, tn, True)),
        out_shape=[jax.ShapeDtypeStruct((m_p, n), BF16), jax.ShapeDtypeStruct((m_s, n), BF16)],
        scratch_shapes=[pltpu.VMEM((k, tn), BF16), pltpu.VMEM((k, tn), BF16)],
        compiler_params=_cparams(2),
    )(h_p, h_s, wg_all, wu_all)


def _ffn_down_kernel(xp, ap, xs, as_, w_ref, op_ref, os_ref, wbf_ref, *, n_p):
    @pl.when(pl.program_id(1) == 0)
    def _():
        wbf_ref[...] = w_ref[...].astype(BF16)

    def rows(x_ref, a_ref, o_ref):
        o_ref[...] = x_ref[...] + _dot(a_ref[...], wbf_ref[...])

    _dual(n_p, rows, (xp, ap, op_ref), (xs, as_, os_ref))


def _ffn_down(x_p, x_s, act_p, act_s, w_all, layer, tm=DOWN_TM, tn=DOWN_TN):
    (m_p, d), m_s = x_p.shape, x_s.shape[0]
    k = w_all.shape[1]
    n_p = m_p // tm
    x_specs = _row_specs(tm, m_s, n_p, tn, True)
    a_specs = _row_specs(tm, m_s, n_p, k, False)
    return pl.pallas_call(
        functools.partial(_ffn_down_kernel, n_p=n_p),
        name="ffn_down",
        grid=(d // tn, n_p + 1),
        in_specs=[x_specs[0], a_specs[0], x_specs[1], a_specs[1],
                  pl.BlockSpec((None, k, tn), lambda j, i: (layer, 0, j))],
        out_specs=list(x_specs),
        out_shape=[jax.ShapeDtypeStruct((m_p, d), F32), jax.ShapeDtypeStruct((m_s, d), F32)],
        scratch_shapes=[pltpu.VMEM((k, tn), BF16)],
        compiler_params=_cparams(2),
    )(x_p, act_p, x_s, act_s, w_all)


def _band_mask(window, n_keys, prev_valid):
    qi = lax.broadcasted_iota(jnp.int32, (BLOCK, n_keys), 0) + (n_keys - BLOCK)
    kj = lax.broadcasted_iota(jnp.int32, (BLOCK, n_keys), 1)
    dist = qi - kj
    band = (dist >= 0) & (dist <= window)
    if n_keys > BLOCK and prev_valid is not None:
        band = band & ((kj >= n_keys - BLOCK) | prev_valid)
    return band


def _attend_pairs(tasks, lane_lo):
    scores = []
    for q2, k2, _, mask in tasks:
        q2 = q2 * (ATTN_SCALE * LOG2E)
        for hh in range(2):
            qm = jnp.where(lane_lo if hh == 0 else ~lane_lo, q2, 0.0).astype(BF16)
            s = lax.dot_general(qm, k2, _NT, preferred_element_type=F32)
            scores.append(jnp.where(mask, s, NEG))
    probs = []
    for s in scores:
        m = jnp.max(s, axis=-1, keepdims=True)
        p = jnp.exp2(s - m)
        probs.append((p.astype(BF16), m * LN2, jnp.sum(p, axis=-1, keepdims=True)))
    out = []
    for t, (_, _, v2, _) in enumerate(tasks):
        out.append([(_dot(probs[2 * t + hh][0], v2),) + probs[2 * t + hh][1:] for hh in range(2)])
    return out


def _attn_a_kernel(q_ref, k_ref, v_ref, o_ref, oacc_ref, lacc_ref, *, seq, branches):
    lane = lax.broadcasted_iota(jnp.int32, (BLOCK, LANES), 1)
    lane_lo = lane < HEAD
    for bi, (window, dil) in enumerate(branches):
        first, last = bi == 0, bi == len(branches) - 1
        mask_first = _band_mask(window // dil, BLOCK, None)
        mask_rest = _band_mask(window // dil, 2 * BLOCK, None)

        def rows(q, r):
            if dil == 1:
                return slice(q * BLOCK, (q + 1) * BLOCK)
            return pl.ds(q * dil * BLOCK + r, BLOCK, stride=dil)

        jobs = [(q, r) for q in range(seq // (dil * BLOCK)) for r in range(dil)]
        group = ATTN_GROUP if first else ATTN_GROUP_MERGING
        for g0 in range(0, len(jobs), group):
            grp = jobs[g0:g0 + group]
            tasks = []
            for q, r in grp:
                cur = rows(q, r)
                if q == 0:
                    k2, v2 = k_ref[cur, :].astype(BF16), v_ref[cur, :].astype(BF16)
                else:
                    k2 = jnp.concatenate([k_ref[rows(q - 1, r), :], k_ref[cur, :]], axis=0).astype(BF16)
                    v2 = jnp.concatenate([v_ref[rows(q - 1, r), :], v_ref[cur, :]], axis=0).astype(BF16)
                tasks.append((q_ref[cur, :], k2, v2, mask_first if q == 0 else mask_rest))
            for (q, r), heads in zip(grp, _attend_pairs(tasks, lane_lo)):
                cur = rows(q, r)
                if not first:
                    o_prev = oacc_ref[cur, :]
                    l_prev = lacc_ref[cur, :]
                halves = []
                lse_blk = jnp.zeros((BLOCK, LANES), F32)
                for hh, (o, m, l) in enumerate(heads):
                    o = o / l
                    lse = m + jnp.log(l)
                    if not first:
                        lse_p = jnp.sum(jnp.where(lane == hh, l_prev, 0.0), axis=-1, keepdims=True)
                        mx = jnp.maximum(lse_p, lse)
                        wp = jnp.exp(lse_p - mx)
                        wi = jnp.exp(lse - mx)
                        den = wp + wi
                        o = (o_prev * wp + o * wi) / den
                        lse = mx + jnp.log(den)
                    halves.append(o)
                    lse_blk = jnp.where(lane == hh, lse, lse_blk)
                o_pair = jnp.where(lane_lo, halves[0], halves[1])
                if last:
                    o_ref[cur, :] = o_pair.astype(o_ref.dtype)
                else:
                    oacc_ref[cur, :] = o_pair
                    lacc_ref[cur, :] = lse_blk


def _attn_a_prompt(p2d, batch, seq):
    branches = tuple(sorted(A_BRANCHES, key=lambda wd: wd[1] == 1))
    blk = (seq, LANES)

    def col(off):
        return lambda b, c: (b, off // LANES + c)

    return pl.pallas_call(
        functools.partial(_attn_a_kernel, seq=seq, branches=branches),
        name="attn_a",
        grid=(batch, A_DIM // LANES),
        in_specs=[pl.BlockSpec(blk, col(OFF_QA)), pl.BlockSpec(blk, col(OFF_KA)), pl.BlockSpec(blk, col(OFF_VA))],
        out_specs=pl.BlockSpec(blk, col(0)),
        out_shape=jax.ShapeDtypeStruct((batch * seq, A_DIM), BF16),
        scratch_shapes=[pltpu.VMEM(blk, F32), pltpu.VMEM(blk, F32)],
        compiler_params=_cparams(2),
    )(p2d, p2d, p2d)


def _attn_c_kernel(sink_ref, q0_ref, q1_ref, q2_ref, q3_ref, kp_ref, kc_ref, vp_ref, vc_ref, o_ref, *, qb):
    q_refs = (q0_ref, q1_ref, q2_ref, q3_ref)
    mask_first = _band_mask(C_WINDOW, 2 * BLOCK, pl.program_id(1) > 0)
    mask_rest = _band_mask(C_WINDOW, 2 * BLOCK, None)
    lane_lo = lax.broadcasted_iota(jnp.int32, (BLOCK, LANES), 1) < HEAD
    lane_lo2 = lax.broadcasted_iota(jnp.int32, (2 * BLOCK, LANES), 1) < HEAD

    def blk_rows(q):
        return slice(q * BLOCK, (q + 1) * BLOCK)

    jobs = []
    for q in range(qb):
        k2 = jnp.concatenate([kp_ref[...] if q == 0 else kc_ref[blk_rows(q - 1), :], kc_ref[blk_rows(q), :]], axis=0)
        v2 = jnp.concatenate([vp_ref[...] if q == 0 else vc_ref[blk_rows(q - 1), :], vc_ref[blk_rows(q), :]], axis=0)
        k2r = pltpu.roll(k2, HEAD, 1)
        v2r = pltpu.roll(v2, HEAD, 1)
        kdup = [jnp.where(lane_lo2, k2, k2r).astype(BF16), jnp.where(lane_lo2, k2r, k2).astype(BF16)]
        vdup = [jnp.where(lane_lo2, v2, v2r).astype(BF16), jnp.where(lane_lo2, v2r, v2).astype(BF16)]
        for pr in range(C_Q_HEADS // 2):
            g = (2 * pr) // C_GROUP
            q2 = q_refs[pr // 2][blk_rows(q), (pr % 2) * LANES:(pr % 2 + 1) * LANES]
            jobs.append((q, pr, (q2, kdup[g], vdup[g], mask_first if q == 0 else mask_rest)))
    for g0 in range(0, len(jobs), ATTN_GROUP):
        grp = jobs[g0:g0 + ATTN_GROUP]
        for (q, pr, _), heads in zip(grp, _attend_pairs([t for _, _, t in grp], lane_lo)):
            halves = []
            for hh, (o, m, l) in enumerate(heads):
                lse = m + jnp.log(l)
                halves.append(o * (_sigmoid(lse - sink_ref[2 * pr + hh]) / l))
            o_ref[blk_rows(q), pr * LANES:(pr + 1) * LANES] = jnp.where(lane_lo, halves[0], halves[1]).astype(o_ref.dtype)


def _attn_c_prompt(p2d, sink, batch, seq):
    rows = batch * seq
    qb = max(1, min(seq // BLOCK, STEP_PAIRS // (C_Q_HEADS // 2)))
    nb = seq // (BLOCK * qb)

    def cur(col):
        return lambda b, n: (b * nb + n, col)

    def prev(col):
        return lambda b, n: (b * nb * qb + jnp.maximum(n * qb - 1, 0), col)

    kv_blk = (BLOCK * qb, C_KV_DIM)
    kv_prev = (BLOCK, C_KV_DIM)
    return pl.pallas_call(
        functools.partial(_attn_c_kernel, qb=qb),
        name="attn_c",
        grid=(batch, nb),
        in_specs=[
            pl.BlockSpec(memory_space=pltpu.SMEM),
            *[pl.BlockSpec((BLOCK * qb, QC_BLK), cur(OFF_QC // QC_BLK + i)) for i in range(C_DIM // QC_BLK)],
            pl.BlockSpec(kv_prev, prev(OFF_KC // C_KV_DIM)),
            pl.BlockSpec(kv_blk, cur(OFF_KC // C_KV_DIM)),
            pl.BlockSpec(kv_prev, prev(OFF_VC // C_KV_DIM)),
            pl.BlockSpec(kv_blk, cur(OFF_VC // C_KV_DIM)),
        ],
        out_specs=pl.BlockSpec((BLOCK * qb, C_DIM), lambda b, n: (b * nb + n, 0)),
        out_shape=jax.ShapeDtypeStruct((rows, C_DIM), BF16),
        compiler_params=_cparams(2),
    )(sink, *([p2d] * (C_DIM // QC_BLK)), p2d, p2d, p2d, p2d)


def _attn_a_sample_kernel(q_ref, kn_ref, vn_ref, kc_ref, vc_ref, cc_ref, cn_ref, o_ref):
    t = q_ref.shape[0]
    pad = jnp.zeros((BLOCK - t, A_DIM), F32)
    k_new = jnp.concatenate([kn_ref[...], pad], axis=0)
    v_new = jnp.concatenate([vn_ref[...], pad], axis=0)
    cnt_c, cnt_n = cc_ref[...], cn_ref[...]
    outs = []
    for h in range(A_HEADS):
        lanes = slice(h * HEAD, (h + 1) * HEAD)
        q = q_ref[:, lanes].astype(BF16)
        s_c = _dot(q, kc_ref[h].astype(BF16)) * ATTN_SCALE
        s_n = lax.dot_general(q, k_new[:, lanes].astype(BF16), _NT, preferred_element_type=F32) * ATTN_SCALE
        s_c = jnp.where(cnt_c > 0.0, s_c, NEG)
        s_n = jnp.where(cnt_n > 0.0, s_n, NEG)
        m = jnp.maximum(jnp.max(s_c, axis=-1, keepdims=True), jnp.max(s_n, axis=-1, keepdims=True))
        p_c = cnt_c * jnp.exp(s_c - m)
        p_n = cnt_n * jnp.exp(s_n - m)
        l = jnp.sum(p_c, axis=-1, keepdims=True) + jnp.sum(p_n, axis=-1, keepdims=True)
        o = lax.dot_general(p_c.astype(BF16), vc_ref[h].astype(BF16), _NT, preferred_element_type=F32)
        o += _dot(p_n.astype(BF16), v_new[:, lanes].astype(BF16))
        outs.append(o / l)
    o_ref[...] = jnp.concatenate(outs, axis=-1)


def _a_sample_counts(t, n_buf):
    qi = n_buf + np.arange(t)[:, None]

    def count(rows):
        delta = qi - rows[None, :]
        c = np.zeros(delta.shape, np.float32)
        for window, dil in A_BRANCHES:
            c += ((delta >= 0) & (delta <= window) & (delta % dil == 0)).astype(np.float32)
        return c

    return count(np.arange(n_buf)), count(n_buf + np.arange(BLOCK))


def _attn_a_sample(ps, cache_k, cache_v, layer, batch, t):
    n_buf = cache_k.shape[4]
    cnt_c, cnt_n = _a_sample_counts(t, n_buf)
    new_blk = (t, A_DIM)
    cache_spec = pl.BlockSpec((None, None, A_HEADS, HEAD, n_buf), lambda b: (layer, b, 0, 0, 0))
    return pl.pallas_call(
        _attn_a_sample_kernel,
        name="attn_a_sample",
        grid=(batch,),
        in_specs=[
            pl.BlockSpec(new_blk, lambda b: (b, OFF_QA // A_DIM)),
            pl.BlockSpec(new_blk, lambda b: (b, OFF_KA // A_DIM)),
            pl.BlockSpec(new_blk, lambda b: (b, OFF_VA // A_DIM)),
            cache_spec, cache_spec,
            pl.BlockSpec(cnt_c.shape, lambda b: (0, 0)),
            pl.BlockSpec(cnt_n.shape, lambda b: (0, 0)),
        ],
        out_specs=pl.BlockSpec(new_blk, lambda b: (b, 0)),
        out_shape=jax.ShapeDtypeStruct((batch * t, A_DIM), F32),
        compiler_params=_cparams(1),
    )(ps, ps, ps, cache_k, cache_v, jnp.asarray(cnt_c), jnp.asarray(cnt_n))


def _attn_c_sample_kernel(q0_ref, q1_ref, q2_ref, q3_ref, kn_ref, vn_ref, kc_ref, vc_ref, sink_ref, o_ref):
    q_refs = (q0_ref, q1_ref, q2_ref, q3_ref)
    t = q0_ref.shape[0]
    n_buf = kc_ref.shape[1]
    rows = C_Q_HEADS * t
    lane_lo = lax.broadcasted_iota(jnp.int32, (t, LANES), 1) < HEAD
    blocks = []
    for j in range(C_Q_HEADS // 2):
        chunk = q_refs[j // 2][:, (j % 2) * LANES:(j % 2 + 1) * LANES]
        rolled = pltpu.roll(chunk, HEAD, 1)
        if (2 * j) // C_GROUP == 0:
            blocks += [jnp.where(lane_lo, chunk, 0.0), jnp.where(lane_lo, rolled, 0.0)]
        else:
            blocks += [jnp.where(lane_lo, 0.0, rolled), jnp.where(lane_lo, 0.0, chunk)]
    qbd = jnp.concatenate(blocks, axis=0).astype(BF16)
    pad = jnp.zeros((BLOCK - t, C_KV_DIM), F32)
    k_new = jnp.concatenate([kn_ref[...], pad], axis=0).astype(BF16)
    v_new = jnp.concatenate([vn_ref[...], pad], axis=0).astype(BF16)
    s_c = _dot(qbd, kc_ref[...].astype(BF16)) * ATTN_SCALE
    s_n = lax.dot_general(qbd, k_new, _NT, preferred_element_type=F32) * ATTN_SCALE
    qt = lax.broadcasted_iota(jnp.int32, (rows, BLOCK), 0) % t
    kj = lax.broadcasted_iota(jnp.int32, (rows, BLOCK), 1)
    dist_c = n_buf + qt - kj
    s_c = jnp.where((dist_c >= 0) & (dist_c <= C_WINDOW), s_c, NEG)
    s_n = jnp.where(kj <= qt, s_n, NEG)
    m = jnp.maximum(jnp.max(s_c, axis=-1, keepdims=True), jnp.max(s_n, axis=-1, keepdims=True))
    p_c = jnp.exp(s_c - m)
    p_n = jnp.exp(s_n - m)
    l = jnp.sum(p_c, axis=-1, keepdims=True) + jnp.sum(p_n, axis=-1, keepdims=True)
    o = lax.dot_general(p_c.astype(BF16), vc_ref[...].astype(BF16), _NT, preferred_element_type=F32)
    o += _dot(p_n.astype(BF16), v_new)
    lse = m + jnp.log(l)
    o = o * (_sigmoid(lse - sink_ref[...]) / l)
    for j in range(C_Q_HEADS // 2):
        blk_a = o[2 * j * t:(2 * j + 1) * t, :]
        blk_b = o[(2 * j + 1) * t:(2 * j + 2) * t, :]
        if (2 * j) // C_GROUP == 0:
            out = jnp.where(lane_lo, blk_a, pltpu.roll(blk_b, HEAD, 1))
        else:
            out = jnp.where(lane_lo, pltpu.roll(blk_a, HEAD, 1), blk_b)
        o_ref[:, j * LANES:(j + 1) * LANES] = out


def _attn_c_sample(ps, cache_k, cache_v, layer, sink_col, batch, t):
    n_buf = cache_k.shape[3]
    assert n_buf == BLOCK
    kv_blk = (t, C_KV_DIM)
    cache_spec = pl.BlockSpec((None, None, C_KV_DIM, n_buf), lambda b: (layer, b, 0, 0))
    return pl.pallas_call(
        _attn_c_sample_kernel,
        name="attn_c_sample",
        grid=(batch,),
        in_specs=[
            *[pl.BlockSpec((t, QC_BLK), functools.partial(lambda b, i: (b, OFF_QC // QC_BLK + i), i=i))
              for i in range(C_DIM // QC_BLK)],
            pl.BlockSpec(kv_blk, lambda b: (b, OFF_KC // C_KV_DIM)),
            pl.BlockSpec(kv_blk, lambda b: (b, OFF_VC // C_KV_DIM)),
            cache_spec, cache_spec,
            pl.BlockSpec(sink_col.shape, lambda b: (0, 0)),
        ],
        out_specs=pl.BlockSpec((t, C_DIM), lambda b: (b, 0)),
        out_shape=jax.ShapeDtypeStruct((batch * t, C_DIM), F32),
        compiler_params=_cparams(1),
    )(*([ps] * (C_DIM // QC_BLK)), ps, ps, cache_k, cache_v, sink_col)


def _rwkv_kernel(r_ref, k_ref, v_ref, lo_ref, shift_ref, s0_ref, mu_ref, vec_ref, w2_ref, a2_ref, g2_ref,
                 o_ref, sout_ref,
                 s_scr, r_scr, d_scr, k_scr, v_scr, kk_scr, b_scr, g_scr, y_scr, wr_scr, kr_scr,
                 cx_scr, cl_scr, *, nb, tc):
    c = pl.program_id(1)
    n_pairs = B_HEADS // 2
    f32 = F32

    li2 = lax.broadcasted_iota(jnp.int32, (2 * LANES, 2 * LANES), 0) // HEAD
    lj2 = lax.broadcasted_iota(jnp.int32, (2 * LANES, 2 * LANES), 1) // HEAD
    bd2 = (li2 == lj2).astype(BF16)

    def head_sum(x):
        return jnp.concatenate([_seg_sum(x[:, j:j + 2 * LANES], bd2) for j in range(0, B_DIM, 2 * LANES)], axis=-1)

    @pl.when(c == 0)
    def _():
        for b in range(nb):
            for p in range(n_pairs):
                s_scr[b * n_pairs + p] = jnp.concatenate([s0_ref[b, 2 * p], s0_ref[b, 2 * p + 1]], axis=-1)
            cx_scr[b] = jnp.broadcast_to(shift_ref[0, b:b + 1, 0:3 * B_DIM], (SUBLANES, 3 * B_DIM))
            cl_scr[b] = jnp.broadcast_to(shift_ref[0, b:b + 1, 3 * B_DIM:], (SUBLANES, LORA_COLS))

    n_rows = nb * tc
    first_row = lax.broadcasted_iota(jnp.int32, (SUBLANES, 1), 0) == 0

    def merged(ref):
        return ref[...].reshape(n_rows, ref.shape[-1])

    def lerp(cur, carry_scr, lo_col, hi_col):
        sh = pltpu.roll(cur, 1, 0)
        parts = []
        for b in range(nb):
            head = jnp.where(first_row, carry_scr[b, 0:1, lo_col:hi_col], sh[b * tc:b * tc + SUBLANES, :])
            parts += [head] + ([sh[b * tc + SUBLANES:(b + 1) * tc, :]] if tc > SUBLANES else [])
        return jnp.concatenate(parts, axis=0)

    def keep_last(carry_scr, lo_col, hi_col, raw):
        for b in range(nb):
            carry_scr[b, :, lo_col:hi_col] = jnp.broadcast_to(raw[(b + 1) * tc - 1:(b + 1) * tc, :],
                                                             (SUBLANES, hi_col - lo_col))

    w0, a0, k_k, k_a = vec_ref[0:1, :], vec_ref[1:2, :], vec_ref[2:3, :], vec_ref[3:4, :]
    r_k, lnx_w, lnx_b = vec_ref[4:5, :], vec_ref[5:6, :], vec_ref[6:7, :]

    mixed = []
    for i, (ref, scr) in enumerate(((r_ref, cx_scr), (k_ref, cx_scr), (v_ref, cx_scr), (lo_ref, cl_scr))):
        raw = merged(ref)
        lo_col = i * B_DIM if scr is cx_scr else 0
        hi_col = lo_col + raw.shape[1]
        sh = lerp(raw, scr, lo_col, hi_col)
        keep_last(scr, lo_col, hi_col, raw)
        mixed.append(raw + (sh - raw) * mu_ref[:, i * B_DIM:i * B_DIM + raw.shape[1]])
    r, k, v, lo = mixed

    z = w0 + _dot_hi(jnp.tanh(lo), w2_ref[...])
    sp = jnp.maximum(-z, 0.0) + jnp.log(1.0 + jnp.exp(-jnp.abs(z)))
    decay = jnp.exp(-jnp.exp(-sp - 0.5))
    a = _sigmoid(a0 + _dot_hi(lo, a2_ref[...]))
    g = _dot_hi(_sigmoid(lo), g2_ref[...])
    kkr = k * k_k
    kk = kkr * lax.rsqrt(jnp.maximum(head_sum(kkr * kkr), 1e-24))
    k = k * (1.0 + (a - 1.0) * k_a)
    bb = kk * a
    wr = decay * r - kk * head_sum(bb * r)
    kr = head_sum(k * r)
    for scr, val in ((r_scr, r), (d_scr, decay), (k_scr, k), (v_scr, v), (kk_scr, kk), (b_scr, bb), (g_scr, g),
                     (wr_scr, wr), (kr_scr, kr)):
        scr[...] = val.reshape(nb, tc, B_DIM)

    sub = lax.broadcasted_iota(jnp.int32, (HEAD, LANES), 0)
    lane = lax.broadcasted_iota(jnp.int32, (HEAD, LANES), 1)
    diag = (lane % HEAD) == sub
    tok_lane = lane % HEAD
    n_all = nb * n_pairs
    grp = min(n_all, RWKV_MATMUL_PAIRS)
    groups = [list(range(g0, g0 + grp)) for g0 in range(0, n_all, grp)]

    def step8(t8, carry):
        t0 = pl.multiple_of(t8 * SUBLANES, SUBLANES)

        def rows_of(scr):
            return [scr[i // n_pairs, pl.ds(t0, SUBLANES), (i % n_pairs) * LANES:(i % n_pairs + 1) * LANES]
                    for i in range(n_all)]

        kk8, v8, d8, b8, k8, wr8, kr8 = [
            rows_of(scr) for scr in (kk_scr, v_scr, d_scr, b_scr, k_scr, wr_scr, kr_scr)]
        yb = [jnp.zeros((HEAD, LANES), f32) for _ in range(n_all)]

        vb = {}
        pairs_mm = range(SUBLANES // 2)
        for grp_ids in groups:
            lhs = jnp.concatenate(
                [jnp.concatenate([jnp.where(diag, v8[i][2 * m:2 * m + 1, :], 0.0).astype(BF16),
                                  jnp.where(diag, v8[i][2 * m + 1:2 * m + 2, :], 0.0).astype(BF16)], axis=1)
                 for m in pairs_mm for i in grp_ids], axis=0)
            res = _dot(lhs, bd2)
            for mi, m in enumerate(pairs_mm):
                for q, i in enumerate(grp_ids):
                    blk = res[(mi * len(grp_ids) + q) * HEAD:(mi * len(grp_ids) + q + 1) * HEAD, :]
                    vb[(i, 2 * m)] = blk[:, 0:LANES]
                    vb[(i, 2 * m + 1)] = blk[:, LANES:]

        for j in range(SUBLANES):
            for grp_ids in groups:
                lhs = jnp.concatenate(
                    [jnp.concatenate([s_scr[i] * kk8[i][j:j + 1, :], s_scr[i] * wr8[i][j:j + 1, :]],
                                     axis=1).astype(BF16) for i in grp_ids], axis=0)
                res = _dot(lhs, bd2)
                for q, i in enumerate(grp_ids):
                    sa = res[q * HEAD:(q + 1) * HEAD, 0:LANES]
                    u = res[q * HEAD:(q + 1) * HEAD, LANES:]
                    yb[i] = jnp.where(tok_lane == j, u, yb[i])
                    s_scr[i] = s_scr[i] * d8[i][j:j + 1, :] - sa * b8[i][j:j + 1, :] + vb[(i, j)] * k8[i][j:j + 1, :]

        for i in range(n_all):
            yt = yb[i].T
            u8 = jnp.concatenate([yt[0:SUBLANES, :], yt[HEAD:HEAD + SUBLANES, :]], axis=1)
            y_scr[i // n_pairs, pl.ds(t0, SUBLANES), (i % n_pairs) * LANES:(i % n_pairs + 1) * LANES] = (
                u8 + v8[i] * kr8[i])
        return carry

    lax.fori_loop(0, tc // SUBLANES, step8, 0, unroll=4 if tc // SUBLANES >= 4 else 1)

    y = merged(y_scr)
    yc = y - head_sum(y) * (1.0 / HEAD)
    var = head_sum(yc * yc) * (1.0 / HEAD)
    yn = yc * lax.rsqrt(var + GN_EPS) * lnx_w + lnx_b
    bonus = head_sum(merged(r_scr) * merged(k_scr) * r_k) * merged(v_scr)
    o_ref[...] = ((yn + bonus) * merged(g_scr)).reshape(nb, tc, B_DIM).astype(o_ref.dtype)

    @pl.when(c == pl.num_programs(1) - 1)
    def _():
        for b in range(nb):
            for p in range(n_pairs):
                s = s_scr[b * n_pairs + p]
                sout_ref[b, 2 * p] = s[:, 0:HEAD]
                sout_ref[b, 2 * p + 1] = s[:, HEAD:]


def _rwkv(p3d, shift0, s0, mu, vecs, w2p, a2p, g2p, nb, tc, out_dtype=F32):
    batch, seq, _ = p3d.shape
    groups = batch // nb
    chunks = seq // tc
    x_blk = (nb, tc, B_DIM)

    def xmap(col):
        return lambda g, c: (g, c, col)

    const2 = lambda g, c: (0, 0)
    scr = lambda *shape: pltpu.VMEM(shape, F32)
    o, s_out = pl.pallas_call(
        functools.partial(_rwkv_kernel, nb=nb, tc=tc),
        name="rwkv7",
        grid=(groups, chunks),
        in_specs=[
            pl.BlockSpec(x_blk, xmap(OFF_PB // B_DIM)),
            pl.BlockSpec(x_blk, xmap(OFF_PB // B_DIM + 1)),
            pl.BlockSpec(x_blk, xmap(OFF_PB // B_DIM + 2)),
            pl.BlockSpec((nb, tc, LORA_COLS), xmap(OFF_LORA // LORA_COLS)),
            pl.BlockSpec((1, nb, B_COLS), lambda g, c: (g, 0, 0)),
            pl.BlockSpec((nb, B_HEADS, HEAD, HEAD), lambda g, c: (g, 0, 0, 0)),
            pl.BlockSpec((1, B_COLS), const2),
            pl.BlockSpec((SUBLANES, B_DIM), const2),
            pl.BlockSpec((LORA_COLS, B_DIM), const2),
            pl.BlockSpec((LORA_COLS, B_DIM), const2),
            pl.BlockSpec((LORA_COLS, B_DIM), const2),
        ],
        out_specs=[
            pl.BlockSpec(x_blk, lambda g, c: (g, c, 0)),
            pl.BlockSpec((nb, B_HEADS, HEAD, HEAD), lambda g, c: (g, 0, 0, 0)),
        ],
        out_shape=[jax.ShapeDtypeStruct((batch, seq, B_DIM), out_dtype),
                   jax.ShapeDtypeStruct((batch, B_HEADS, HEAD, HEAD), F32)],
        scratch_shapes=[scr(nb * B_HEADS // 2, HEAD, LANES)] + [scr(nb, tc, B_DIM)] * 10
        + [scr(nb, SUBLANES, 3 * B_DIM), scr(nb, SUBLANES, LORA_COLS)],
        compiler_params=_cparams(2),
    )(p3d, p3d, p3d, p3d, shift0.reshape(groups, nb, B_COLS), s0, mu, vecs, w2p, a2p, g2p)
    return o, s_out


def _rope_tables(pos):
    half = ROPE_DIM // 2
    inv = jnp.exp(-math.log(ROPE_THETA) * jnp.arange(half, dtype=F32) * 2.0 / ROPE_DIM)
    ang = pos.astype(F32)[:, None] * inv[None, :]
    cos, sin = jnp.cos(ang), jnp.sin(ang)
    lm = np.arange(LANES) % HEAD
    first = jnp.asarray(lm < half)[None, :]
    second = jnp.asarray((lm >= half) & (lm < ROPE_DIM))[None, :]
    freq = np.where(lm < half, lm, np.where(lm < ROPE_DIM, lm - half, 0))
    cos_l, sin_l = cos[:, freq], sin[:, freq]
    c = jnp.where(first | second, cos_l, 1.0)
    s1 = jnp.where(first, -sin_l, 0.0)
    s2 = jnp.where(second, sin_l, 0.0)
    return c, s1, s2


def _rope_flag():
    col = np.arange(IN_COLS)
    rope = (col < OFF_VA) | ((col >= OFF_QC) & (col < OFF_VC))
    tiles = tuple(int(t) for t in np.nonzero(rope.reshape(-1, IN_TN).any(axis=1))[0])
    return jnp.asarray(rope.astype(np.float32))[None, :], tiles


def _pad_rows(w, start):
    return jnp.zeros((LORA_COLS, B_DIM), F32).at[start:start + w.shape[0]].set(w)


def _mixers(p2d, batch, seq, layer, is_prompt, cache, rwkv_w, sink):
    p3d = p2d.reshape(batch, seq, IN_COLS)
    mu, vecs, w2p, a2p, g2p = rwkv_w
    if is_prompt:
        oa = _attn_a_prompt(p2d, batch, seq)
        oc = _attn_c_prompt(p2d, sink, batch, seq)
        shift0 = jnp.zeros((batch, B_COLS), F32)
        s0 = jnp.zeros((batch, B_HEADS, HEAD, HEAD), F32)
        ob, wkv = _rwkv(p3d, shift0, s0, mu, vecs, w2p, a2p, g2p, nb=batch, tc=min(seq, 128), out_dtype=BF16)
    else:
        a_k, a_v, c_k, c_v, wkv0, shift0 = cache
        oa = _attn_a_sample(p2d, a_k, a_v, layer, batch, seq)
        sink_col = jnp.repeat(sink, seq)[:, None]
        oc = _attn_c_sample(p2d, c_k, c_v, layer, sink_col, batch, seq)
        ob, wkv = _rwkv(p3d, shift0[layer], wkv0[layer], mu, vecs, w2p, a2p, g2p, nb=4, tc=seq)
    return oa, ob.reshape(batch * seq, B_DIM), oc, wkv


def kernel(x_prompt, x_sample, cache_a_k, cache_a_v, cache_c_k, cache_c_v, state_b_wkv, state_b_shift, g_mix, w_in, w_out, b_mu, b_w0, b_w2, b_a0, b_a2, b_g2, b_k_k, b_k_a, b_r_k, b_lnx_w, b_lnx_b, c_sink, g_ffn, w_gate, w_up, w_down, g_final):
    depth = w_in.shape[0]
    bp, lp, d = x_prompt.shape
    bs, ls, _ = x_sample.shape
    a_win = cache_a_k.shape[2]
    c_win = cache_c_k.shape[2]
    assert (bp * lp) % PROJ_TM == 0 and lp % PROJ_TM == 0 and lp % (16 * BLOCK) == 0 and a_win >= A_BRANCHES[-1][0] and c_win == C_WINDOW and bs % 4 == 0

    flag, rope_tiles = _rope_flag()
    tabs_p = _rope_tables(jnp.arange(lp, dtype=jnp.int32))
    tabs_s = _rope_tables(jnp.tile(PAST_LEN + jnp.arange(ls, dtype=jnp.int32), bs))
    cak = jnp.transpose(cache_a_k, (0, 1, 3, 4, 2))
    cav = jnp.transpose(cache_a_v, (0, 1, 3, 4, 2))
    cck = jnp.transpose(cache_c_k, (0, 1, 3, 4, 2)).reshape(depth, bs, C_KV_DIM, c_win)
    ccv = jnp.transpose(cache_c_v, (0, 1, 3, 4, 2)).reshape(depth, bs, C_KV_DIM, c_win)
    cache = (cak, cav, cck, ccv, state_b_wkv, state_b_shift)

    xp = x_prompt.reshape(bp * lp, d)
    xs = x_sample.reshape(bs * ls, d)
    new_p = [[] for _ in range(6)]
    new_s = [[] for _ in range(6)]
    for l in range(depth):
        vecs = jnp.stack([b_w0[l], b_a0[l], b_k_k[l], b_k_a[l], b_r_k[l], b_lnx_w[l], b_lnx_b[l],
                          jnp.zeros((B_DIM,), F32)], axis=0)
        rwkv_w = (b_mu[l][None, :], vecs, _pad_rows(b_w2[l], 0), _pad_rows(b_a2[l], 96), _pad_rows(b_g2[l], 192))
        sink = c_sink[l].reshape(C_Q_HEADS)
        hp, hs = _rmsnorm(xp, g_mix[l], BF16), _rmsnorm(xs, g_mix[l], BF16)
        pp, ps = _inproj(hp, hs, w_in, l, flag, rope_tiles, tabs_p, tabs_s, PROJ_TM)
        oa_p, ob_p, oc_p, wkv_p = _mixers(pp, bp, lp, l, True, cache, rwkv_w, sink)
        oa_s, ob_s, oc_s, wkv_s = _mixers(ps, bs, ls, l, False, cache, rwkv_w, sink)
        xp, xs = _outproj((xp, xs), (oa_p, oa_s), (ob_p, ob_s), (oc_p, oc_s), w_out, l)
        hp, hs = _rmsnorm(xp, g_ffn[l], BF16), _rmsnorm(xs, g_ffn[l], BF16)
        act_p, act_s = _ffn_up(hp, hs, w_gate, w_up, l)
        xp, xs = _ffn_down(xp, xs, act_p, act_s, w_down, l)
        for p2d, batch, seq, wkv, new, is_prompt in ((pp, bp, lp, wkv_p, new_p, True), (ps, bs, ls, wkv_s, new_s, False)):
            p3d = p2d.reshape(batch, seq, IN_COLS)
            a_keep = min(a_win, seq) if is_prompt else seq
            c_keep = min(c_win, seq) if is_prompt else seq
            state = (
                p3d[:, seq - a_keep:, OFF_KA:OFF_VA].reshape(batch, a_keep, A_HEADS, HEAD),
                p3d[:, seq - a_keep:, OFF_VA:OFF_PB].reshape(batch, a_keep, A_HEADS, HEAD),
                p3d[:, seq - c_keep:, OFF_KC:OFF_VC].reshape(batch, c_keep, C_KV_DIM // HEAD, HEAD),
                p3d[:, seq - c_keep:, OFF_VC:].reshape(batch, c_keep, C_KV_DIM // HEAD, HEAD),
                wkv,
                p3d[:, -1, OFF_PB:OFF_QC],
            )
            for i in range(6):
                new[i].append(state[i])
    y_prompt = _rmsnorm(xp, g_final, F32).reshape(bp, lp, d)
    y_sample = _rmsnorm(xs, g_final, F32).reshape(bs, ls, d)
    outs_p = [jnp.stack(t, axis=0) for t in new_p]
    outs_s = [jnp.stack(t, axis=0) for t in new_s]
    return (y_prompt, y_sample, *outs_p, *outs_s)
```

```python
import functools
import math

import numpy as np
import jax
import jax.numpy as jnp
from jax import lax
from jax.experimental import pallas as pl
from jax.experimental.pallas import tpu as pltpu

F32 = jnp.float32
BF16 = jnp.bfloat16

LANES = 128
SUBLANES = 8
VMEM_LIMIT = 52 * 1024 * 1024

D_MODEL = 2048
HEAD = 64
A_DIM = 512
B_DIM = 512
C_DIM = 1024
C_KV_DIM = 128
A_HEADS = 8
B_HEADS = 8
C_Q_HEADS = 16
C_GROUP = 8
LORA_COLS = 256
B_COLS = 3 * B_DIM + LORA_COLS
IN_COLS = 3 * A_DIM + B_COLS + C_DIM + 2 * C_KV_DIM
D_FF = 5632
OFF_QA, OFF_KA, OFF_VA = 0, A_DIM, 2 * A_DIM
OFF_PB = 3 * A_DIM
OFF_LORA = OFF_PB + 3 * B_DIM
OFF_QC = OFF_PB + B_COLS
OFF_KC = OFF_QC + C_DIM
OFF_VC = OFF_KC + C_KV_DIM
A_BRANCHES = ((128, 1), (512, 4), (2048, 16))
C_WINDOW = 128
BLOCK = 128
QC_BLK = 256
IN_TN = 512
PROJ_TM = 1024
UP_TN = 512
OUT_TM, OUT_TN = 512, 1024
DOWN_TM, DOWN_TN = 512, 512
STEP_PAIRS = 16
ATTN_GROUP = 4
ATTN_GROUP_MERGING = 2
SAMPLE_RWKV_BATCH = 8
RWKV_MATMUL_PAIRS = 16
PAST_LEN = 16384
ROPE_THETA = 500000.0
ROPE_DIM = 16
RMS_EPS = 1e-6
GN_EPS = 64e-5
ATTN_SCALE = HEAD ** -0.5
NEG = -1e30
LOG2E = 1.4426950408889634
LN2 = 0.6931471805599453

_NT = (((1,), (1,)), ((), ()))


def _cparams(n_grid):
    return pltpu.CompilerParams(dimension_semantics=("arbitrary",) * n_grid, vmem_limit_bytes=VMEM_LIMIT)


def _dot(a, b):
    return jnp.dot(a, b, preferred_element_type=F32)


def _split_bf16(x):
    hi = x.astype(BF16)
    lo = (x - hi.astype(F32)).astype(BF16)
    return hi, lo


def _dot_hi(a, b):
    a_hi, a_lo = _split_bf16(a)
    b_hi, b_lo = _split_bf16(b)
    return _dot(a_hi, b_hi) + (_dot(a_lo, b_hi) + _dot(a_hi, b_lo))


def _seg_sum(x, bd):
    hi, lo = _split_bf16(x)
    return _dot(hi, bd) + _dot(lo, bd)


def _sigmoid(x):
    return 1.0 / (1.0 + jnp.exp(-x))


def _rmsnorm_kernel(x_ref, g_ref, o_ref):
    x = x_ref[...]
    ms = jnp.mean(x * x, axis=-1, keepdims=True)
    o_ref[...] = (x * lax.rsqrt(ms + RMS_EPS) * g_ref[...]).astype(o_ref.dtype)


def _rmsnorm(x, g, out_dtype):
    m, d = x.shape
    tm = min(m, 512)
    return pl.pallas_call(
        _rmsnorm_kernel,
        name="rmsnorm",
        grid=(m // tm,),
        in_specs=[pl.BlockSpec((tm, d), lambda i: (i, 0)), pl.BlockSpec((1, d), lambda i: (0, 0))],
        out_specs=pl.BlockSpec((tm, d), lambda i: (i, 0)),
        out_shape=jax.ShapeDtypeStruct((m, d), out_dtype),
        compiler_params=_cparams(1),
    )(x, g.reshape(1, d))


def _row_specs(tm_p, tm_s, n_p, width, tiled, cycle=None):
    col = (lambda j: j) if tiled else (lambda j: 0)
    row = (lambda i: jnp.minimum(i, n_p - 1)) if cycle is None else (lambda i: jnp.minimum(i, n_p - 1) % cycle)
    return (pl.BlockSpec((tm_p, width), lambda j, i: (row(i), col(j))),
            pl.BlockSpec((tm_s, width), lambda j, i: (0, col(j))))


def _dual(n_p, tile_fn, prompt_refs, sample_refs):
    i = pl.program_id(1)

    @pl.when(i < n_p)
    def _():
        tile_fn(*prompt_refs)

    @pl.when(i == n_p)
    def _():
        tile_fn(*sample_refs)


def _inproj_kernel(hp, cp, s1p, s2p, hs, cs, s1s, s2s, w_ref, flag_ref, op_ref, os_ref, wbf_ref, *, rope_tiles, n_p):
    @pl.when(pl.program_id(1) == 0)
    def _():
        wbf_ref[...] = w_ref[...].astype(BF16)

    tile = pl.program_id(0)
    has_rope = functools.reduce(jnp.logical_or, [tile == t for t in rope_tiles])

    def rows(h_ref, c_ref, s1_ref, s2_ref, o_ref):
        @pl.when(jnp.logical_not(has_rope))
        def _():
            o_ref[...] = _dot(h_ref[...], wbf_ref[...])

        @pl.when(has_rope)
        def _():
            c, s1, s2 = c_ref[...], s1_ref[...], s2_ref[...]
            half = 2 * LANES
            for h0 in range(0, o_ref.shape[1], half):
                acc = _dot(h_ref[...], wbf_ref[:, h0:h0 + half])
                for j in range(half // LANES):
                    sl = slice(h0 + j * LANES, h0 + (j + 1) * LANES)
                    x = acc[:, j * LANES:(j + 1) * LANES]
                    rot = x * c + pltpu.roll(x, LANES - 8, 1) * s1 + pltpu.roll(x, 8, 1) * s2
                    o_ref[:, sl] = jnp.where(flag_ref[:, sl] > 0.0, rot, x)

    _dual(n_p, rows, (hp, cp, s1p, s2p, op_ref), (hs, cs, s1s, s2s, os_ref))


def _inproj(h_p, h_s, w_all, layer, flag, rope_tiles, tabs_p, tabs_s, tm, tn=IN_TN):
    (m_p, k), m_s = h_p.shape, h_s.shape[0]
    n = w_all.shape[2]
    n_p = m_p // tm
    h_specs = _row_specs(tm, m_s, n_p, k, False)
    tab_specs = _row_specs(tm, m_s, n_p, LANES, False, cycle=tabs_p[0].shape[0] // tm)
    out_specs = _row_specs(tm, m_s, n_p, tn, True)
    return pl.pallas_call(
        functools.partial(_inproj_kernel, rope_tiles=rope_tiles, n_p=n_p),
        name="inproj",
        grid=(n // tn, n_p + 1),
        in_specs=[h_specs[0]] + [tab_specs[0]] * 3 + [h_specs[1]] + [tab_specs[1]] * 3 + [
            pl.BlockSpec((None, k, tn), lambda j, i: (layer, 0, j)),
            pl.BlockSpec((1, tn), lambda j, i: (0, j)),
        ],
        out_specs=list(out_specs),
        out_shape=[jax.ShapeDtypeStruct((m_p, n), F32), jax.ShapeDtypeStruct((m_s, n), F32)],
        scratch_shapes=[pltpu.VMEM((k, tn), BF16)],
        compiler_params=_cparams(2),
    )(h_p, *tabs_p, h_s, *tabs_s, w_all, flag)


def _outproj_kernel(xp, oap, obp, ocp, xs, oas, obs, ocs, w_ref, op_ref, os_ref, wbf_ref, *, n_p):
    @pl.when(pl.program_id(1) == 0)
    def _():
        wbf_ref[...] = w_ref[...].astype(BF16)

    def rows(x_ref, oa_ref, ob_ref, oc_ref, o_ref):
        acc = _dot(oa_ref[...].astype(BF16), wbf_ref[0:A_DIM, :])
        acc += _dot(ob_ref[...].astype(BF16), wbf_ref[A_DIM:A_DIM + B_DIM, :])
        acc += _dot(oc_ref[...].astype(BF16), wbf_ref[A_DIM + B_DIM:, :])
        o_ref[...] = x_ref[...] + acc

    _dual(n_p, rows, (xp, oap, obp, ocp, op_ref), (xs, oas, obs, ocs, os_ref))


def _outproj(x, oa, ob, oc, w_all, layer, tm=OUT_TM, tn=OUT_TN):
    (m_p, d), m_s = x[0].shape, x[1].shape[0]
    k = w_all.shape[1]
    n_p = m_p // tm
    specs = [_row_specs(tm, m_s, n_p, tn, True)] + [_row_specs(tm, m_s, n_p, w, False) for w in (A_DIM, B_DIM, C_DIM)]
    return pl.pallas_call(
        functools.partial(_outproj_kernel, n_p=n_p),
        name="outproj",
        grid=(d // tn, n_p + 1),
        in_specs=[s[0] for s in specs] + [s[1] for s in specs] + [
            pl.BlockSpec((None, k, tn), lambda j, i: (layer, 0, j))],
        out_specs=list(specs[0]),
        out_shape=[jax.ShapeDtypeStruct((m_p, d), F32), jax.ShapeDtypeStruct((m_s, d), F32)],
        scratch_shapes=[pltpu.VMEM((k, tn), BF16)],
        compiler_params=_cparams(2),
    )(x[0], oa[0], ob[0], oc[0], x[1], oa[1], ob[1], oc[1], w_all)


def _ffn_up_kernel(hp, hs, wg_ref, wu_ref, op_ref, os_ref, wgbf_ref, wubf_ref, *, n_p):
    @pl.when(pl.program_id(1) == 0)
    def _():
        wgbf_ref[...] = wg_ref[...].astype(BF16)
        wubf_ref[...] = wu_ref[...].astype(BF16)

    def rows(h_ref, o_ref):
        h = h_ref[...]
        gate = _dot(h, wgbf_ref[...])
        up = _dot(h, wubf_ref[...])
        o_ref[...] = (gate * _sigmoid(gate) * up).astype(o_ref.dtype)

    _dual(n_p, rows, (hp, op_ref), (hs, os_ref))


def _ffn_up(h_p, h_s, wg_all, wu_all, layer, tm=PROJ_TM, tn=UP_TN):
    (m_p, k), m_s = h_p.shape, h_s.shape[0]
    n = wg_all.shape[2]
    n_p = m_p // tm
    w_spec = pl.BlockSpec((None, k, tn), lambda j, i: (layer, 0, j))
    return pl.pallas_call(
        functools.partial(_ffn_up_kernel, n_p=n_p),
        name="ffn_up",
        grid=(n // tn, n_p + 1),
        in_specs=list(_row_specs(tm, m_s, n_p, k, False)) + [w_spec, w_spec],
        out_specs=list(_row_specs(tm, m_s, n_p, tn, True)),
        out_shape=[jax.ShapeDtypeStruct((m_p, n), BF16), jax.ShapeDtypeStruct((m_s, n), BF16)],
        scratch_shapes=[pltpu.VMEM((k, tn), BF16), pltpu.VMEM((k, tn), BF16)],
        compiler_params=_cparams(2),
    )(h_p, h_s, wg_all, wu_all)


def _ffn_down_kernel(xp, ap, xs, as_, w_ref, op_ref, os_ref, wbf_ref, *, n_p):
    @pl.when(pl.program_id(1) == 0)
    def _():
        wbf_ref[...] = w_ref[...].astype(BF16)

    def rows(x_ref, a_ref, o_ref):
        o_ref[...] = x_ref[...] + _dot(a_ref[...], wbf_ref[...])

    _dual(n_p, rows, (xp, ap, op_ref), (xs, as_, os_ref))


def _ffn_down(x_p, x_s, act_p, act_s, w_all, layer, tm=DOWN_TM, tn=DOWN_TN):
    (m_p, d), m_s = x_p.shape, x_s.shape[0]
    k = w_all.shape[1]
    n_p = m_p // tm
    x_specs = _row_specs(tm, m_s, n_p, tn, True)
    a_specs = _row_specs(tm, m_s, n_p, k, False)
    return pl.pallas_call(
        functools.partial(_ffn_down_kernel, n_p=n_p),
        name="ffn_down",
        grid=(d // tn, n_p + 1),
        in_specs=[x_specs[0], a_specs[0], x_specs[1], a_specs[1],
                  pl.BlockSpec((None, k, tn), lambda j, i: (layer, 0, j))],
        out_specs=list(x_specs),
        out_shape=[jax.ShapeDtypeStruct((m_p, d), F32), jax.ShapeDtypeStruct((m_s, d), F32)],
        scratch_shapes=[pltpu.VMEM((k, tn), BF16)],
        compiler_params=_cparams(2),
    )(x_p, act_p, x_s, act_s, w_all)


def _band_mask(window, n_keys, prev_valid):
    qi = lax.broadcasted_iota(jnp.int32, (BLOCK, n_keys), 0) + (n_keys - BLOCK)
    kj = lax.broadcasted_iota(jnp.int32, (BLOCK, n_keys), 1)
    dist = qi - kj
    band = (dist >= 0) & (dist <= window)
    if n_keys > BLOCK and prev_valid is not None:
        band = band & ((kj >= n_keys - BLOCK) | prev_valid)
    return band


def _attend_pairs(tasks, lane_lo):
    scores = []
    for q2, k2, _, mask in tasks:
        q2 = q2 * (ATTN_SCALE * LOG2E)
        for hh in range(2):
            qm = jnp.where(lane_lo if hh == 0 else ~lane_lo, q2, 0.0).astype(BF16)
            s = lax.dot_general(qm, k2, _NT, preferred_element_type=F32)
            scores.append(jnp.where(mask, s, NEG))
    probs = []
    for s in scores:
        m = jnp.max(s, axis=-1, keepdims=True)
        p = jnp.exp2(s - m)
        probs.append((p.astype(BF16), m * LN2, jnp.sum(p, axis=-1, keepdims=True)))
    out = []
    for t, (_, _, v2, _) in enumerate(tasks):
        out.append([(_dot(probs[2 * t + hh][0], v2),) + probs[2 * t + hh][1:] for hh in range(2)])
    return out


def _attn_a_kernel(q_ref, k_ref, v_ref, o_ref, oacc_ref, lacc_ref, *, seq, branches):
    lane = lax.broadcasted_iota(jnp.int32, (BLOCK, LANES), 1)
    lane_lo = lane < HEAD
    for bi, (window, dil) in enumerate(branches):
        first, last = bi == 0, bi == len(branches) - 1
        mask_first = _band_mask(window // dil, BLOCK, None)
        mask_rest = _band_mask(window // dil, 2 * BLOCK, None)

        def rows(q, r):
            if dil == 1:
                return slice(q * BLOCK, (q + 1) * BLOCK)
            return pl.ds(q * dil * BLOCK + r, BLOCK, stride=dil)

        jobs = [(q, r) for q in range(seq // (dil * BLOCK)) for r in range(dil)]
        group = ATTN_GROUP if first else ATTN_GROUP_MERGING
        for g0 in range(0, len(jobs), group):
            grp = jobs[g0:g0 + group]
            tasks = []
            for q, r in grp:
                cur = rows(q, r)
                if q == 0:
                    k2, v2 = k_ref[cur, :].astype(BF16), v_ref[cur, :].astype(BF16)
                else:
                    k2 = jnp.concatenate([k_ref[rows(q - 1, r), :], k_ref[cur, :]], axis=0).astype(BF16)
                    v2 = jnp.concatenate([v_ref[rows(q - 1, r), :], v_ref[cur, :]], axis=0).astype(BF16)
                tasks.append((q_ref[cur, :], k2, v2, mask_first if q == 0 else mask_rest))
            for (q, r), heads in zip(grp, _attend_pairs(tasks, lane_lo)):
                cur = rows(q, r)
                if not first:
                    o_prev = oacc_ref[cur, :]
                    l_prev = lacc_ref[cur, :]
                halves = []
                lse_blk = jnp.zeros((BLOCK, LANES), F32)
                for hh, (o, m, l) in enumerate(heads):
                    o = o / l
                    lse = m + jnp.log(l)
                    if not first:
                        lse_p = jnp.sum(jnp.where(lane == hh, l_prev, 0.0), axis=-1, keepdims=True)
                        mx = jnp.maximum(lse_p, lse)
                        wp = jnp.exp(lse_p - mx)
                        wi = jnp.exp(lse - mx)
                        den = wp + wi
                        o = (o_prev * wp + o * wi) / den
                        lse = mx + jnp.log(den)
                    halves.append(o)
                    lse_blk = jnp.where(lane == hh, lse, lse_blk)
                o_pair = jnp.where(lane_lo, halves[0], halves[1])
                if last:
                    o_ref[cur, :] = o_pair.astype(o_ref.dtype)
                else:
                    oacc_ref[cur, :] = o_pair
                    lacc_ref[cur, :] = lse_blk


def _attn_a_prompt(p2d, batch, seq):
    branches = tuple(sorted(A_BRANCHES, key=lambda wd: wd[1] == 1))
    blk = (seq, LANES)

    def col(off):
        return lambda b, c: (b, off // LANES + c)

    return pl.pallas_call(
        functools.partial(_attn_a_kernel, seq=seq, branches=branches),
        name="attn_a",
        grid=(batch, A_DIM // LANES),
        in_specs=[pl.BlockSpec(blk, col(OFF_QA)), pl.BlockSpec(blk, col(OFF_KA)), pl.BlockSpec(blk, col(OFF_VA))],
        out_specs=pl.BlockSpec(blk, col(0)),
        out_shape=jax.ShapeDtypeStruct((batch * seq, A_DIM), BF16),
        scratch_shapes=[pltpu.VMEM(blk, F32), pltpu.VMEM(blk, F32)],
        compiler_params=_cparams(2),
    )(p2d, p2d, p2d)


def _attn_c_kernel(sink_ref, q0_ref, q1_ref, q2_ref, q3_ref, kp_ref, kc_ref, vp_ref, vc_ref, o_ref, *, qb):
    q_refs = (q0_ref, q1_ref, q2_ref, q3_ref)
    mask_first = _band_mask(C_WINDOW, 2 * BLOCK, pl.program_id(1) > 0)
    mask_rest = _band_mask(C_WINDOW, 2 * BLOCK, None)
    lane_lo = lax.broadcasted_iota(jnp.int32, (BLOCK, LANES), 1) < HEAD
    lane_lo2 = lax.broadcasted_iota(jnp.int32, (2 * BLOCK, LANES), 1) < HEAD

    def blk_rows(q):
        return slice(q * BLOCK, (q + 1) * BLOCK)

    jobs = []
    for q in range(qb):
        k2 = jnp.concatenate([kp_ref[...] if q == 0 else kc_ref[blk_rows(q - 1), :], kc_ref[blk_rows(q), :]], axis=0)
        v2 = jnp.concatenate([vp_ref[...] if q == 0 else vc_ref[blk_rows(q - 1), :], vc_ref[blk_rows(q), :]], axis=0)
        k2r = pltpu.roll(k2, HEAD, 1)
        v2r = pltpu.roll(v2, HEAD, 1)
        kdup = [jnp.where(lane_lo2, k2, k2r).astype(BF16), jnp.where(lane_lo2, k2r, k2).astype(BF16)]
        vdup = [jnp.where(lane_lo2, v2, v2r).astype(BF16), jnp.where(lane_lo2, v2r, v2).astype(BF16)]
        for pr in range(C_Q_HEADS // 2):
            g = (2 * pr) // C_GROUP
            q2 = q_refs[pr // 2][blk_rows(q), (pr % 2) * LANES:(pr % 2 + 1) * LANES]
            jobs.append((q, pr, (q2, kdup[g], vdup[g], mask_first if q == 0 else mask_rest)))
    for g0 in range(0, len(jobs), ATTN_GROUP):
        grp = jobs[g0:g0 + ATTN_GROUP]
        for (q, pr, _), heads in zip(grp, _attend_pairs([t for _, _, t in grp], lane_lo)):
            halves = []
            for hh, (o, m, l) in enumerate(heads):
                lse = m + jnp.log(l)
                halves.append(o * (_sigmoid(lse - sink_ref[2 * pr + hh]) / l))
            o_ref[blk_rows(q), pr * LANES:(pr + 1) * LANES] = jnp.where(lane_lo, halves[0], halves[1]).astype(o_ref.dtype)


def _attn_c_prompt(p2d, sink, batch, seq):
    rows = batch * seq
    qb = max(1, min(seq // BLOCK, STEP_PAIRS // (C_Q_HEADS // 2)))
    nb = seq // (BLOCK * qb)

    def cur(col):
        return lambda b, n: (b * nb + n, col)

    def prev(col):
        return lambda b, n: (b * nb * qb + jnp.maximum(n * qb - 1, 0), col)

    kv_blk = (BLOCK * qb, C_KV_DIM)
    kv_prev = (BLOCK, C_KV_DIM)
    return pl.pallas_call(
        functools.partial(_attn_c_kernel, qb=qb),
        name="attn_c",
        grid=(batch, nb),
        in_specs=[
            pl.BlockSpec(memory_space=pltpu.SMEM),
            *[pl.BlockSpec((BLOCK * qb, QC_BLK), cur(OFF_QC // QC_BLK + i)) for i in range(C_DIM // QC_BLK)],
            pl.BlockSpec(kv_prev, prev(OFF_KC // C_KV_DIM)),
            pl.BlockSpec(kv_blk, cur(OFF_KC // C_KV_DIM)),
            pl.BlockSpec(kv_prev, prev(OFF_VC // C_KV_DIM)),
            pl.BlockSpec(kv_blk, cur(OFF_VC // C_KV_DIM)),
        ],
        out_specs=pl.BlockSpec((BLOCK * qb, C_DIM), lambda b, n: (b * nb + n, 0)),
        out_shape=jax.ShapeDtypeStruct((rows, C_DIM), BF16),
        compiler_params=_cparams(2),
    )(sink, *([p2d] * (C_DIM // QC_BLK)), p2d, p2d, p2d, p2d)


def _attn_a_sample_kernel(q_ref, kn_ref, vn_ref, kc_ref, vc_ref, cc_ref, cn_ref, o_ref):
    t = q_ref.shape[0]
    pad = jnp.zeros((BLOCK - t, A_DIM), F32)
    k_new = jnp.concatenate([kn_ref[...], pad], axis=0)
    v_new = jnp.concatenate([vn_ref[...], pad], axis=0)
    cnt_c, cnt_n = cc_ref[...], cn_ref[...]
    outs = []
    for h in range(A_HEADS):
        lanes = slice(h * HEAD, (h + 1) * HEAD)
        q = q_ref[:, lanes].astype(BF16)
        s_c = _dot(q, kc_ref[h].astype(BF16)) * ATTN_SCALE
        s_n = lax.dot_general(q, k_new[:, lanes].astype(BF16), _NT, preferred_element_type=F32) * ATTN_SCALE
        s_c = jnp.where(cnt_c > 0.0, s_c, NEG)
        s_n = jnp.where(cnt_n > 0.0, s_n, NEG)
        m = jnp.maximum(jnp.max(s_c, axis=-1, keepdims=True), jnp.max(s_n, axis=-1, keepdims=True))
        p_c = cnt_c * jnp.exp(s_c - m)
        p_n = cnt_n * jnp.exp(s_n - m)
        l = jnp.sum(p_c, axis=-1, keepdims=True) + jnp.sum(p_n, axis=-1, keepdims=True)
        o = lax.dot_general(p_c.astype(BF16), vc_ref[h].astype(BF16), _NT, preferred_element_type=F32)
        o += _dot(p_n.astype(BF16), v_new[:, lanes].astype(BF16))
        outs.append(o / l)
    o_ref[...] = jnp.concatenate(outs, axis=-1)


def _a_sample_counts(t, n_buf):
    qi = n_buf + np.arange(t)[:, None]

    def count(rows):
        delta = qi - rows[None, :]
        c = np.zeros(delta.shape, np.float32)
        for window, dil in A_BRANCHES:
            c += ((delta >= 0) & (delta <= window) & (delta % dil == 0)).astype(np.float32)
        return c

    return count(np.arange(n_buf)), count(n_buf + np.arange(BLOCK))


def _attn_a_sample(ps, cache_k, cache_v, layer, batch, t):
    n_buf = cache_k.shape[4]
    cnt_c, cnt_n = _a_sample_counts(t, n_buf)
    new_blk = (t, A_DIM)
    cache_spec = pl.BlockSpec((None, None, A_HEADS, HEAD, n_buf), lambda b: (layer, b, 0, 0, 0))
    return pl.pallas_call(
        _attn_a_sample_kernel,
        name="attn_a_sample",
        grid=(batch,),
        in_specs=[
            pl.BlockSpec(new_blk, lambda b: (b, OFF_QA // A_DIM)),
            pl.BlockSpec(new_blk, lambda b: (b, OFF_KA // A_DIM)),
            pl.BlockSpec(new_blk, lambda b: (b, OFF_VA // A_DIM)),
            cache_spec, cache_spec,
            pl.BlockSpec(cnt_c.shape, lambda b: (0, 0)),
            pl.BlockSpec(cnt_n.shape, lambda b: (0, 0)),
        ],
        out_specs=pl.BlockSpec(new_blk, lambda b: (b, 0)),
        out_shape=jax.ShapeDtypeStruct((batch * t, A_DIM), F32),
        compiler_params=_cparams(1),
    )(ps, ps, ps, cache_k, cache_v, jnp.asarray(cnt_c), jnp.asarray(cnt_n))


def _attn_c_sample_kernel(q0_ref, q1_ref, q2_ref, q3_ref, kn_ref, vn_ref, kc_ref, vc_ref, sink_ref, o_ref):
    q_refs = (q0_ref, q1_ref, q2_ref, q3_ref)
    t = q0_ref.shape[0]
    n_buf = kc_ref.shape[1]
    rows = C_Q_HEADS * t
    lane_lo = lax.broadcasted_iota(jnp.int32, (t, LANES), 1) < HEAD
    blocks = []
    for j in range(C_Q_HEADS // 2):
        chunk = q_refs[j // 2][:, (j % 2) * LANES:(j % 2 + 1) * LANES]
        rolled = pltpu.roll(chunk, HEAD, 1)
        if (2 * j) // C_GROUP == 0:
            blocks += [jnp.where(lane_lo, chunk, 0.0), jnp.where(lane_lo, rolled, 0.0)]
        else:
            blocks += [jnp.where(lane_lo, 0.0, rolled), jnp.where(lane_lo, 0.0, chunk)]
    qbd = jnp.concatenate(blocks, axis=0).astype(BF16)
    pad = jnp.zeros((BLOCK - t, C_KV_DIM), F32)
    k_new = jnp.concatenate([kn_ref[...], pad], axis=0).astype(BF16)
    v_new = jnp.concatenate([vn_ref[...], pad], axis=0).astype(BF16)
    s_c = _dot(qbd, kc_ref[...].astype(BF16)) * ATTN_SCALE
    s_n = lax.dot_general(qbd, k_new, _NT, preferred_element_type=F32) * ATTN_SCALE
    qt = lax.broadcasted_iota(jnp.int32, (rows, BLOCK), 0) % t
    kj = lax.broadcasted_iota(jnp.int32, (rows, BLOCK), 1)
    dist_c = n_buf + qt - kj
    s_c = jnp.where((dist_c >= 0) & (dist_c <= C_WINDOW), s_c, NEG)
    s_n = jnp.where(kj <= qt, s_n, NEG)
    m = jnp.maximum(jnp.max(s_c, axis=-1, keepdims=True), jnp.max(s_n, axis=-1, keepdims=True))
    p_c = jnp.exp(s_c - m)
    p_n = jnp.exp(s_n - m)
    l = jnp.sum(p_c, axis=-1, keepdims=True) + jnp.sum(p_n, axis=-1, keepdims=True)
    o = lax.dot_general(p_c.astype(BF16), vc_ref[...].astype(BF16), _NT, preferred_element_type=F32)
    o += _dot(p_n.astype(BF16), v_new)
    lse = m + jnp.log(l)
    o = o * (_sigmoid(lse - sink_ref[...]) / l)
    for j in range(C_Q_HEADS // 2):
        blk_a = o[2 * j * t:(2 * j + 1) * t, :]
        blk_b = o[(2 * j + 1) * t:(2 * j + 2) * t, :]
        if (2 * j) // C_GROUP == 0:
            out = jnp.where(lane_lo, blk_a, pltpu.roll(blk_b, HEAD, 1))
        else:
            out = jnp.where(lane_lo, pltpu.roll(blk_a, HEAD, 1), blk_b)
        o_ref[:, j * LANES:(j + 1) * LANES] = out


def _attn_c_sample(ps, cache_k, cache_v, layer, sink_col, batch, t):
    n_buf = cache_k.shape[3]
    assert n_buf == BLOCK
    kv_blk = (t, C_KV_DIM)
    cache_spec = pl.BlockSpec((None, None, C_KV_DIM, n_buf), lambda b: (layer, b, 0, 0))
    return pl.pallas_call(
        _attn_c_sample_kernel,
        name="attn_c_sample",
        grid=(batch,),
        in_specs=[
            *[pl.BlockSpec((t, QC_BLK), functools.partial(lambda b, i: (b, OFF_QC // QC_BLK + i), i=i))
              for i in range(C_DIM // QC_BLK)],
            pl.BlockSpec(kv_blk, lambda b: (b, OFF_KC // C_KV_DIM)),
            pl.BlockSpec(kv_blk, lambda b: (b, OFF_VC // C_KV_DIM)),
            cache_spec, cache_spec,
            pl.BlockSpec(sink_col.shape, lambda b: (0, 0)),
        ],
        out_specs=pl.BlockSpec((t, C_DIM), lambda b: (b, 0)),
        out_shape=jax.ShapeDtypeStruct((batch * t, C_DIM), F32),
        compiler_params=_cparams(1),
    )(*([ps] * (C_DIM // QC_BLK)), ps, ps, cache_k, cache_v, sink_col)


def _rwkv_kernel(r_ref, k_ref, v_ref, lo_ref, shift_ref, s0_ref, mu_ref, vec_ref, w2_ref, a2_ref, g2_ref,
                 o_ref, sout_ref,
                 s_scr, r_scr, d_scr, k_scr, v_scr, kk_scr, b_scr, g_scr, y_scr, wr_scr, kr_scr,
                 cx_scr, cl_scr, *, nb, tc):
    c = pl.program_id(1)
    n_pairs = B_HEADS // 2
    f32 = F32

    li2 = lax.broadcasted_iota(jnp.int32, (2 * LANES, 2 * LANES), 0) // HEAD
    lj2 = lax.broadcasted_iota(jnp.int32, (2 * LANES, 2 * LANES), 1) // HEAD
    bd2 = (li2 == lj2).astype(BF16)

    def head_sum(x):
        return jnp.concatenate([_seg_sum(x[:, j:j + 2 * LANES], bd2) for j in range(0, B_DIM, 2 * LANES)], axis=-1)

    @pl.when(c == 0)
    def _():
        for b in range(nb):
            for p in range(n_pairs):
                s_scr[b * n_pairs + p] = jnp.concatenate([s0_ref[b, 2 * p], s0_ref[b, 2 * p + 1]], axis=-1)
            cx_scr[b] = jnp.broadcast_to(shift_ref[0, b:b + 1, 0:3 * B_DIM], (SUBLANES, 3 * B_DIM))
            cl_scr[b] = jnp.broadcast_to(shift_ref[0, b:b + 1, 3 * B_DIM:], (SUBLANES, LORA_COLS))

    n_rows = nb * tc
    first_row = lax.broadcasted_iota(jnp.int32, (SUBLANES, 1), 0) == 0

    def merged(ref):
        return ref[...].reshape(n_rows, ref.shape[-1])

    def lerp(cur, carry_scr, lo_col, hi_col):
        sh = pltpu.roll(cur, 1, 0)
        parts = []
        for b in range(nb):
            head = jnp.where(first_row, carry_scr[b, 0:1, lo_col:hi_col], sh[b * tc:b * tc + SUBLANES, :])
            parts += [head] + ([sh[b * tc + SUBLANES:(b + 1) * tc, :]] if tc > SUBLANES else [])
        return jnp.concatenate(parts, axis=0)

    def keep_last(carry_scr, lo_col, hi_col, raw):
        for b in range(nb):
            carry_scr[b, :, lo_col:hi_col] = jnp.broadcast_to(raw[(b + 1) * tc - 1:(b + 1) * tc, :],
                                                             (SUBLANES, hi_col - lo_col))

    w0, a0, k_k, k_a = vec_ref[0:1, :], vec_ref[1:2, :], vec_ref[2:3, :], vec_ref[3:4, :]
    r_k, lnx_w, lnx_b = vec_ref[4:5, :], vec_ref[5:6, :], vec_ref[6:7, :]

    mixed = []
    for i, (ref, scr) in enumerate(((r_ref, cx_scr), (k_ref, cx_scr), (v_ref, cx_scr), (lo_ref, cl_scr))):
        raw = merged(ref)
        lo_col = i * B_DIM if scr is cx_scr else 0
        hi_col = lo_col + raw.shape[1]
        sh = lerp(raw, scr, lo_col, hi_col)
        keep_last(scr, lo_col, hi_col, raw)
        mixed.append(raw + (sh - raw) * mu_ref[:, i * B_DIM:i * B_DIM + raw.shape[1]])
    r, k, v, lo = mixed

    z = w0 + _dot_hi(jnp.tanh(lo), w2_ref[...])
    sp = jnp.maximum(-z, 0.0) + jnp.log(1.0 + jnp.exp(-jnp.abs(z)))
    decay = jnp.exp(-jnp.exp(-sp - 0.5))
    a = _sigmoid(a0 + _dot_hi(lo, a2_ref[...]))
    g = _dot_hi(_sigmoid(lo), g2_ref[...])
    kkr = k * k_k
    kk = kkr * lax.rsqrt(jnp.maximum(head_sum(kkr * kkr), 1e-24))
    k = k * (1.0 + (a - 1.0) * k_a)
    bb = kk * a
    wr = decay * r - kk * head_sum(bb * r)
    kr = head_sum(k * r)
    for scr, val in ((r_scr, r), (d_scr, decay), (k_scr, k), (v_scr, v), (kk_scr, kk), (b_scr, bb), (g_scr, g),
                     (wr_scr, wr), (kr_scr, kr)):
        scr[...] = val.reshape(nb, tc, B_DIM)

    sub = lax.broadcasted_iota(jnp.int32, (HEAD, LANES), 0)
    lane = lax.broadcasted_iota(jnp.int32, (HEAD, LANES), 1)
    diag = (lane % HEAD) == sub
    tok_lane = lane % HEAD
    n_all = nb * n_pairs
    grp = min(n_all, RWKV_MATMUL_PAIRS)
    groups = [list(range(g0, g0 + grp)) for g0 in range(0, n_all, grp)]

    def step8(t8, carry):
        t0 = pl.multiple_of(t8 * SUBLANES, SUBLANES)

        def rows_of(scr):
            return [scr[i // n_pairs, pl.ds(t0, SUBLANES), (i % n_pairs) * LANES:(i % n_pairs + 1) * LANES]
                    for i in range(n_all)]

        kk8, v8, d8, b8, k8, wr8, kr8 = [
            rows_of(scr) for scr in (kk_scr, v_scr, d_scr, b_scr, k_scr, wr_scr, kr_scr)]
        yb = [jnp.zeros((HEAD, LANES), f32) for _ in range(n_all)]

        vb = {}
        pairs_mm = range(SUBLANES // 2)
        for grp_ids in groups:
            lhs = jnp.concatenate(
                [jnp.concatenate([jnp.where(diag, v8[i][2 * m:2 * m + 1, :], 0.0).astype(BF16),
                                  jnp.where(diag, v8[i][2 * m + 1:2 * m + 2, :], 0.0).astype(BF16)], axis=1)
                 for m in pairs_mm for i in grp_ids], axis=0)
            res = _dot(lhs, bd2)
            for mi, m in enumerate(pairs_mm):
                for q, i in enumerate(grp_ids):
                    blk = res[(mi * len(grp_ids) + q) * HEAD:(mi * len(grp_ids) + q + 1) * HEAD, :]
                    vb[(i, 2 * m)] = blk[:, 0:LANES]
                    vb[(i, 2 * m + 1)] = blk[:, LANES:]

        for j in range(SUBLANES):
            for grp_ids in groups:
                lhs = jnp.concatenate(
                    [jnp.concatenate([s_scr[i] * kk8[i][j:j + 1, :], s_scr[i] * wr8[i][j:j + 1, :]],
                                     axis=1).astype(BF16) for i in grp_ids], axis=0)
                res = _dot(lhs, bd2)
                for q, i in enumerate(grp_ids):
                    sa = res[q * HEAD:(q + 1) * HEAD, 0:LANES]
                    u = res[q * HEAD:(q + 1) * HEAD, LANES:]
                    yb[i] = jnp.where(tok_lane == j, u, yb[i])
                    s_scr[i] = s_scr[i] * d8[i][j:j + 1, :] - sa * b8[i][j:j + 1, :] + vb[(i, j)] * k8[i][j:j + 1, :]

        for i in range(n_all):
            yt = yb[i].T
            u8 = jnp.concatenate([yt[0:SUBLANES, :], yt[HEAD:HEAD + SUBLANES, :]], axis=1)
            y_scr[i // n_pairs, pl.ds(t0, SUBLANES), (i % n_pairs) * LANES:(i % n_pairs + 1) * LANES] = (
                u8 + v8[i] * kr8[i])
        return carry

    lax.fori_loop(0, tc // SUBLANES, step8, 0, unroll=4 if tc // SUBLANES >= 4 else 1)

    y = merged(y_scr)
    yc = y - head_sum(y) * (1.0 / HEAD)
    var = head_sum(yc * yc) * (1.0 / HEAD)
    yn = yc * lax.rsqrt(var + GN_EPS) * lnx_w + lnx_b
    bonus = head_sum(merged(r_scr) * merged(k_scr) * r_k) * merged(v_scr)
    o_ref[...] = ((yn + bonus) * merged(g_scr)).reshape(nb, tc, B_DIM).astype(o_ref.dtype)

    @pl.when(c == pl.num_programs(1) - 1)
    def _():
        for b in range(nb):
            for p in range(n_pairs):
                s = s_scr[b * n_pairs + p]
                sout_ref[b, 2 * p] = s[:, 0:HEAD]
                sout_ref[b, 2 * p + 1] = s[:, HEAD:]


def _rwkv(p3d, shift0, s0, mu, vecs, w2p, a2p, g2p, nb, tc, out_dtype=F32):
    batch, seq, _ = p3d.shape
    groups = batch // nb
    chunks = seq // tc
    x_blk = (nb, tc, B_DIM)

    def xmap(col):
        return lambda g, c: (g, c, col)

    const2 = lambda g, c: (0, 0)
    scr = lambda *shape: pltpu.VMEM(shape, F32)
    o, s_out = pl.pallas_call(
        functools.partial(_rwkv_kernel, nb=nb, tc=tc),
        name="rwkv7",
        grid=(groups, chunks),
        in_specs=[
            pl.BlockSpec(x_blk, xmap(OFF_PB // B_DIM)),
            pl.BlockSpec(x_blk, xmap(OFF_PB // B_DIM + 1)),
            pl.BlockSpec(x_blk, xmap(OFF_PB // B_DIM + 2)),
            pl.BlockSpec((nb, tc, LORA_COLS), xmap(OFF_LORA // LORA_COLS)),
            pl.BlockSpec((1, nb, B_COLS), lambda g, c: (g, 0, 0)),
            pl.BlockSpec((nb, B_HEADS, HEAD, HEAD), lambda g, c: (g, 0, 0, 0)),
            pl.BlockSpec((1, B_COLS), const2),
            pl.BlockSpec((SUBLANES, B_DIM), const2),
            pl.BlockSpec((LORA_COLS, B_DIM), const2),
            pl.BlockSpec((LORA_COLS, B_DIM), const2),
            pl.BlockSpec((LORA_COLS, B_DIM), const2),
        ],
        out_specs=[
            pl.BlockSpec(x_blk, lambda g, c: (g, c, 0)),
            pl.BlockSpec((nb, B_HEADS, HEAD, HEAD), lambda g, c: (g, 0, 0, 0)),
        ],
        out_shape=[jax.ShapeDtypeStruct((batch, seq, B_DIM), out_dtype),
                   jax.ShapeDtypeStruct((batch, B_HEADS, HEAD, HEAD), F32)],
        scratch_shapes=[scr(nb * B_HEADS // 2, HEAD, LANES)] + [scr(nb, tc, B_DIM)] * 10
        + [scr(nb, SUBLANES, 3 * B_DIM), scr(nb, SUBLANES, LORA_COLS)],
        compiler_params=_cparams(2),
    )(p3d, p3d, p3d, p3d, shift0.reshape(groups, nb, B_COLS), s0, mu, vecs, w2p, a2p, g2p)
    return o, s_out


def _rope_tables(pos):
    half = ROPE_DIM // 2
    inv = jnp.exp(-math.log(ROPE_THETA) * jnp.arange(half, dtype=F32) * 2.0 / ROPE_DIM)
    ang = pos.astype(F32)[:, None] * inv[None, :]
    cos, sin = jnp.cos(ang), jnp.sin(ang)
    lm = np.arange(LANES) % HEAD
    first = jnp.asarray(lm < half)[None, :]
    second = jnp.asarray((lm >= half) & (lm < ROPE_DIM))[None, :]
    freq = np.where(lm < half, lm, np.where(lm < ROPE_DIM, lm - half, 0))
    cos_l, sin_l = cos[:, freq], sin[:, freq]
    c = jnp.where(first | second, cos_l, 1.0)
    s1 = jnp.where(first, -sin_l, 0.0)
    s2 = jnp.where(second, sin_l, 0.0)
    return c, s1, s2


def _rope_flag():
    col = np.arange(IN_COLS)
    rope = (col < OFF_VA) | ((col >= OFF_QC) & (col < OFF_VC))
    tiles = tuple(int(t) for t in np.nonzero(rope.reshape(-1, IN_TN).any(axis=1))[0])
    return jnp.asarray(rope.astype(np.float32))[None, :], tiles


def _pad_rows(w, start):
    return jnp.zeros((LORA_COLS, B_DIM), F32).at[start:start + w.shape[0]].set(w)


def _window_stack_kernel(*refs, depth):
    ins, (ko_ref, vo_ref) = refs[:2 * depth], refs[2 * depth:]
    for l in range(depth):
        @pl.when(pl.program_id(0) == l)
        def _():
            ko_ref[...] = ins[2 * l][...].T
            vo_ref[...] = ins[2 * l + 1][...].T


def _window_stack(p_layers, batch, seq, keep):
    depth = len(p_layers)
    tb = min(keep, 512)
    nblk, first = seq // tb, (seq - keep) // tb

    def in_spec(l, off):
        return pl.BlockSpec((tb, A_DIM), lambda g, b, t: (jnp.where(g == l, b * nblk + first + t, 0), off // A_DIM))

    out_spec = pl.BlockSpec((None, None, A_DIM, tb), lambda g, b, t: (g, b, 0, t))
    shape = jax.ShapeDtypeStruct((depth, batch, A_DIM, keep), F32)
    k, v = pl.pallas_call(
        functools.partial(_window_stack_kernel, depth=depth),
        name="window_stack",
        grid=(depth, batch, keep // tb),
        in_specs=[in_spec(l, off) for l in range(depth) for off in (OFF_KA, OFF_VA)],
        out_specs=[out_spec, out_spec],
        out_shape=[shape, shape],
        compiler_params=_cparams(3),
    )(*[p for p in p_layers for _ in range(2)])

    def as_output(x):
        return jnp.transpose(x.reshape(depth, batch, A_HEADS, HEAD, keep), (0, 1, 4, 2, 3))

    return as_output(k), as_output(v)


def _mixers(p2d, batch, seq, layer, is_prompt, cache, rwkv_w, sink):
    p3d = p2d.reshape(batch, seq, IN_COLS)
    mu, vecs, w2p, a2p, g2p = rwkv_w
    if is_prompt:
        oa = _attn_a_prompt(p2d, batch, seq)
        oc = _attn_c_prompt(p2d, sink, batch, seq)
        shift0 = jnp.zeros((batch, B_COLS), F32)
        s0 = jnp.zeros((batch, B_HEADS, HEAD, HEAD), F32)
        ob, wkv = _rwkv(p3d, shift0, s0, mu, vecs, w2p, a2p, g2p, nb=batch, tc=min(seq, 128), out_dtype=BF16)
    else:
        a_k, a_v, c_k, c_v, wkv0, shift0 = cache
        oa = _attn_a_sample(p2d, a_k, a_v, layer, batch, seq)
        sink_col = jnp.repeat(sink, seq)[:, None]
        oc = _attn_c_sample(p2d, c_k, c_v, layer, sink_col, batch, seq)
        ob, wkv = _rwkv(p3d, shift0[layer], wkv0[layer], mu, vecs, w2p, a2p, g2p, nb=SAMPLE_RWKV_BATCH, tc=seq)
    return oa, ob.reshape(batch * seq, B_DIM), oc, wkv


def kernel(x_prompt, x_sample, cache_a_k, cache_a_v, cache_c_k, cache_c_v, state_b_wkv, state_b_shift, g_mix, w_in, w_out, b_mu, b_w0, b_w2, b_a0, b_a2, b_g2, b_k_k, b_k_a, b_r_k, b_lnx_w, b_lnx_b, c_sink, g_ffn, w_gate, w_up, w_down, g_final):
    depth = w_in.shape[0]
    bp, lp, d = x_prompt.shape
    bs, ls, _ = x_sample.shape
    a_win = cache_a_k.shape[2]
    c_win = cache_c_k.shape[2]
    assert (bp * lp) % PROJ_TM == 0 and lp % PROJ_TM == 0 and lp % (16 * BLOCK) == 0 and a_win >= A_BRANCHES[-1][0] and c_win == C_WINDOW and bs % SAMPLE_RWKV_BATCH == 0

    flag, rope_tiles = _rope_flag()
    tabs_p = _rope_tables(jnp.arange(lp, dtype=jnp.int32))
    tabs_s = _rope_tables(jnp.tile(PAST_LEN + jnp.arange(ls, dtype=jnp.int32), bs))
    cak = jnp.transpose(cache_a_k, (0, 1, 3, 4, 2))
    cav = jnp.transpose(cache_a_v, (0, 1, 3, 4, 2))
    cck = jnp.transpose(cache_c_k, (0, 1, 3, 4, 2)).reshape(depth, bs, C_KV_DIM, c_win)
    ccv = jnp.transpose(cache_c_v, (0, 1, 3, 4, 2)).reshape(depth, bs, C_KV_DIM, c_win)
    cache = (cak, cav, cck, ccv, state_b_wkv, state_b_shift)

    xp = x_prompt.reshape(bp * lp, d)
    xs = x_sample.reshape(bs * ls, d)
    new_p = [[] for _ in range(6)]
    new_s = [[] for _ in range(6)]
    p_prompt = []
    for l in range(depth):
        vecs = jnp.stack([b_w0[l], b_a0[l], b_k_k[l], b_k_a[l], b_r_k[l], b_lnx_w[l], b_lnx_b[l],
                          jnp.zeros((B_DIM,), F32)], axis=0)
        rwkv_w = (b_mu[l][None, :], vecs, _pad_rows(b_w2[l], 0), _pad_rows(b_a2[l], 96), _pad_rows(b_g2[l], 192))
        sink = c_sink[l].reshape(C_Q_HEADS)
        hp, hs = _rmsnorm(xp, g_mix[l], BF16), _rmsnorm(xs, g_mix[l], BF16)
        pp, ps = _inproj(hp, hs, w_in, l, flag, rope_tiles, tabs_p, tabs_s, PROJ_TM)
        oa_p, ob_p, oc_p, wkv_p = _mixers(pp, bp, lp, l, True, cache, rwkv_w, sink)
        oa_s, ob_s, oc_s, wkv_s = _mixers(ps, bs, ls, l, False, cache, rwkv_w, sink)
        xp, xs = _outproj((xp, xs), (oa_p, oa_s), (ob_p, ob_s), (oc_p, oc_s), w_out, l)
        hp, hs = _rmsnorm(xp, g_ffn[l], BF16), _rmsnorm(xs, g_ffn[l], BF16)
        act_p, act_s = _ffn_up(hp, hs, w_gate, w_up, l)
        xp, xs = _ffn_down(xp, xs, act_p, act_s, w_down, l)
        for p2d, batch, seq, wkv, new, is_prompt in ((pp, bp, lp, wkv_p, new_p, True), (ps, bs, ls, wkv_s, new_s, False)):
            p3d = p2d.reshape(batch, seq, IN_COLS)
            a_keep = min(a_win, seq) if is_prompt else seq
            c_keep = min(c_win, seq) if is_prompt else seq
            state = (
                p3d[:, seq - a_keep:, OFF_KA:OFF_VA].reshape(batch, a_keep, A_HEADS, HEAD),
                p3d[:, seq - a_keep:, OFF_VA:OFF_PB].reshape(batch, a_keep, A_HEADS, HEAD),
                p3d[:, seq - c_keep:, OFF_KC:OFF_VC].reshape(batch, c_keep, C_KV_DIM // HEAD, HEAD),
                p3d[:, seq - c_keep:, OFF_VC:].reshape(batch, c_keep, C_KV_DIM // HEAD, HEAD),
                wkv,
                p3d[:, -1, OFF_PB:OFF_QC],
            )
            for i in range(6):
                new[i].append(state[i])
        p_prompt.append(pp)
    y_prompt = _rmsnorm(xp, g_final, F32).reshape(bp, lp, d)
    y_sample = _rmsnorm(xs, g_final, F32).reshape(bs, ls, d)
    outs_p = list(_window_stack(p_prompt, bp, lp, min(a_win, lp))) + [jnp.stack(t, axis=0) for t in new_p[2:]]
    outs_s = [jnp.stack(t, axis=0) for t in new_s]
    return (y_prompt, y_sample, *outs_p, *outs_s)
```

```python
import functools
import math

import numpy as np
import jax
import jax.numpy as jnp
from jax import lax
from jax.experimental import pallas as pl
from jax.experimental.pallas import tpu as pltpu

F32 = jnp.float32
BF16 = jnp.bfloat16

LANES = 128
SUBLANES = 8
VMEM_LIMIT = 52 * 1024 * 1024

D_MODEL = 2048
HEAD = 64
A_DIM = 512
B_DIM = 512
C_DIM = 1024
C_KV_DIM = 128
A_HEADS = 8
B_HEADS = 8
C_Q_HEADS = 16
C_GROUP = 8
LORA_COLS = 256
B_COLS = 3 * B_DIM + LORA_COLS
IN_COLS = 3 * A_DIM + B_COLS + C_DIM + 2 * C_KV_DIM
D_FF = 5632
OFF_QA, OFF_KA, OFF_VA = 0, A_DIM, 2 * A_DIM
OFF_PB = 3 * A_DIM
OFF_LORA = OFF_PB + 3 * B_DIM
OFF_QC = OFF_PB + B_COLS
OFF_KC = OFF_QC + C_DIM
OFF_VC = OFF_KC + C_KV_DIM
A_BRANCHES = ((128, 1), (512, 4), (2048, 16))
C_WINDOW = 128
BLOCK = 128
QC_BLK = 256
IN_TN = 512
PROJ_TM = 1024
UP_TN = 512
OUT_TM, OUT_TN = 512, 1024
DOWN_TM, DOWN_TN = 512, 512
STEP_PAIRS = 16
ATTN_GROUP = 4
ATTN_GROUP_MERGING = 2
WINDOW_STACK_ROWS = 1024
SAMPLE_RWKV_BATCH = 8
RWKV_MATMUL_PAIRS = 16
PAST_LEN = 16384
ROPE_THETA = 500000.0
ROPE_DIM = 16
RMS_EPS = 1e-6
GN_EPS = 64e-5
KK_NORM_EPS = 1e-12
ATTN_SCALE = HEAD ** -0.5
NEG = -1e30
LOG2E = 1.4426950408889634
LN2 = 0.6931471805599453

_NT = (((1,), (1,)), ((), ()))


def _cparams(n_grid):
    return pltpu.CompilerParams(dimension_semantics=("arbitrary",) * n_grid, vmem_limit_bytes=VMEM_LIMIT)


def _dot(a, b):
    return jnp.dot(a, b, preferred_element_type=F32)


def _split_bf16(x):
    hi = x.astype(BF16)
    lo = (x - hi.astype(F32)).astype(BF16)
    return hi, lo


def _dot_hi(a, b):
    a_hi, a_lo = _split_bf16(a)
    b_hi, b_lo = _split_bf16(b)
    return _dot(a_hi, b_hi) + (_dot(a_lo, b_hi) + _dot(a_hi, b_lo))


def _seg_sum(x, bd):
    hi, lo = _split_bf16(x)
    return _dot(hi, bd) + _dot(lo, bd)


def _sigmoid(x):
    return 1.0 / (1.0 + jnp.exp(-x))


def _rmsnorm_kernel(x_ref, g_ref, o_ref):
    x = x_ref[...]
    ms = jnp.mean(x * x, axis=-1, keepdims=True)
    o_ref[...] = (x * lax.rsqrt(ms + RMS_EPS) * g_ref[...]).astype(o_ref.dtype)


def _rmsnorm(x, g, out_dtype):
    m, d = x.shape
    tm = min(m, 512)
    return pl.pallas_call(
        _rmsnorm_kernel,
        name="rmsnorm",
        grid=(m // tm,),
        in_specs=[pl.BlockSpec((tm, d), lambda i: (i, 0)), pl.BlockSpec((1, d), lambda i: (0, 0))],
        out_specs=pl.BlockSpec((tm, d), lambda i: (i, 0)),
        out_shape=jax.ShapeDtypeStruct((m, d), out_dtype),
        compiler_params=_cparams(1),
    )(x, g.reshape(1, d))


def _row_specs(tm_p, tm_s, n_p, width, tiled, cycle=None):
    col = (lambda j: j) if tiled else (lambda j: 0)
    row = (lambda i: jnp.minimum(i, n_p - 1)) if cycle is None else (lambda i: jnp.minimum(i, n_p - 1) % cycle)
    return (pl.BlockSpec((tm_p, width), lambda j, i: (row(i), col(j))),
            pl.BlockSpec((tm_s, width), lambda j, i: (0, col(j))))


def _dual(n_p, tile_fn, prompt_refs, sample_refs):
    i = pl.program_id(1)

    @pl.when(i < n_p)
    def _():
        tile_fn(*prompt_refs)

    @pl.when(i == n_p)
    def _():
        tile_fn(*sample_refs)


def _inproj_kernel(hp, cp, s1p, s2p, hs, cs, s1s, s2s, w_ref, flag_ref, op_ref, os_ref, wbf_ref, *, rope_tiles, n_p):
    @pl.when(pl.program_id(1) == 0)
    def _():
        wbf_ref[...] = w_ref[...].astype(BF16)

    tile = pl.program_id(0)
    has_rope = functools.reduce(jnp.logical_or, [tile == t for t in rope_tiles])

    def rows(h_ref, c_ref, s1_ref, s2_ref, o_ref):
        @pl.when(jnp.logical_not(has_rope))
        def _():
            o_ref[...] = _dot(h_ref[...], wbf_ref[...])

        @pl.when(has_rope)
        def _():
            c, s1, s2 = c_ref[...], s1_ref[...], s2_ref[...]
            half = 2 * LANES
            for h0 in range(0, o_ref.shape[1], half):
                acc = _dot(h_ref[...], wbf_ref[:, h0:h0 + half])
                for j in range(half // LANES):
                    sl = slice(h0 + j * LANES, h0 + (j + 1) * LANES)
                    x = acc[:, j * LANES:(j + 1) * LANES]
                    rot = x * c + pltpu.roll(x, LANES - 8, 1) * s1 + pltpu.roll(x, 8, 1) * s2
                    o_ref[:, sl] = jnp.where(flag_ref[:, sl] > 0.0, rot, x)

    _dual(n_p, rows, (hp, cp, s1p, s2p, op_ref), (hs, cs, s1s, s2s, os_ref))


def _inproj(h_p, h_s, w_all, layer, flag, rope_tiles, tabs_p, tabs_s, tm, tn=IN_TN):
    (m_p, k), m_s = h_p.shape, h_s.shape[0]
    n = w_all.shape[2]
    n_p = m_p // tm
    h_specs = _row_specs(tm, m_s, n_p, k, False)
    tab_specs = _row_specs(tm, m_s, n_p, LANES, False, cycle=tabs_p[0].shape[0] // tm)
    out_specs = _row_specs(tm, m_s, n_p, tn, True)
    return pl.pallas_call(
        functools.partial(_inproj_kernel, rope_tiles=rope_tiles, n_p=n_p),
        name="inproj",
        grid=(n // tn, n_p + 1),
        in_specs=[h_specs[0]] + [tab_specs[0]] * 3 + [h_specs[1]] + [tab_specs[1]] * 3 + [
            pl.BlockSpec((None, k, tn), lambda j, i: (layer, 0, j)),
            pl.BlockSpec((1, tn), lambda j, i: (0, j)),
        ],
        out_specs=list(out_specs),
        out_shape=[jax.ShapeDtypeStruct((m_p, n), F32), jax.ShapeDtypeStruct((m_s, n), F32)],
        scratch_shapes=[pltpu.VMEM((k, tn), BF16)],
        compiler_params=_cparams(2),
    )(h_p, *tabs_p, h_s, *tabs_s, w_all, flag)


def _outproj_kernel(xp, oap, obp, ocp, xs, oas, obs, ocs, w_ref, op_ref, os_ref, wbf_ref, *, n_p):
    @pl.when(pl.program_id(1) == 0)
    def _():
        wbf_ref[...] = w_ref[...].astype(BF16)

    def rows(x_ref, oa_ref, ob_ref, oc_ref, o_ref):
        acc = _dot(oa_ref[...].astype(BF16), wbf_ref[0:A_DIM, :])
        acc += _dot(ob_ref[...].astype(BF16), wbf_ref[A_DIM:A_DIM + B_DIM, :])
        acc += _dot(oc_ref[...].astype(BF16), wbf_ref[A_DIM + B_DIM:, :])
        o_ref[...] = x_ref[...] + acc

    _dual(n_p, rows, (xp, oap, obp, ocp, op_ref), (xs, oas, obs, ocs, os_ref))


def _outproj(x, oa, ob, oc, w_all, layer, tm=OUT_TM, tn=OUT_TN):
    (m_p, d), m_s = x[0].shape, x[1].shape[0]
    k = w_all.shape[1]
    n_p = m_p // tm
    specs = [_row_specs(tm, m_s, n_p, tn, True)] + [_row_specs(tm, m_s, n_p, w, False) for w in (A_DIM, B_DIM, C_DIM)]
    return pl.pallas_call(
        functools.partial(_outproj_kernel, n_p=n_p),
        name="outproj",
        grid=(d // tn, n_p + 1),
        in_specs=[s[0] for s in specs] + [s[1] for s in specs] + [
            pl.BlockSpec((None, k, tn), lambda j, i: (layer, 0, j))],
        out_specs=list(specs[0]),
        out_shape=[jax.ShapeDtypeStruct((m_p, d), F32), jax.ShapeDtypeStruct((m_s, d), F32)],
        scratch_shapes=[pltpu.VMEM((k, tn), BF16)],
        compiler_params=_cparams(2),
    )(x[0], oa[0], ob[0], oc[0], x[1], oa[1], ob[1], oc[1], w_all)


def _ffn_up_kernel(hp, hs, wg_ref, wu_ref, op_ref, os_ref, wgbf_ref, wubf_ref, *, n_p):
    @pl.when(pl.program_id(1) == 0)
    def _():
        wgbf_ref[...] = wg_ref[...].astype(BF16)
        wubf_ref[...] = wu_ref[...].astype(BF16)

    def rows(h_ref, o_ref):
        h = h_ref[...]
        gate = _dot(h, wgbf_ref[...])
        up = _dot(h, wubf_ref[...])
        o_ref[...] = (gate * _sigmoid(gate) * up).astype(o_ref.dtype)

    _dual(n_p, rows, (hp, op_ref), (hs, os_ref))


def _ffn_up(h_p, h_s, wg_all, wu_all, layer, tm=PROJ_TM, tn=UP_TN):
    (m_p, k), m_s = h_p.shape, h_s.shape[0]
    n = wg_all.shape[2]
    n_p = m_p // tm
    w_spec = pl.BlockSpec((None, k, tn), lambda j, i: (layer, 0, j))
    return pl.pallas_call(
        functools.partial(_ffn_up_kernel, n_p=n_p),
        name="ffn_up",
        grid=(n // tn, n_p + 1),
        in_specs=list(_row_specs(tm, m_s, n_p, k, False)) + [w_spec, w_spec],
        out_specs=list(_row_specs(tm, m_s, n_p, tn, True)),
        out_shape=[jax.ShapeDtypeStruct((m_p, n), BF16), jax.ShapeDtypeStruct((m_s, n), BF16)],
        scratch_shapes=[pltpu.VMEM((k, tn), BF16), pltpu.VMEM((k, tn), BF16)],
        compiler_params=_cparams(2),
    )(h_p, h_s, wg_all, wu_all)


def _ffn_down_kernel(xp, ap, xs, as_, w_ref, op_ref, os_ref, wbf_ref, *, n_p):
    @pl.when(pl.program_id(1) == 0)
    def _():
        wbf_ref[...] = w_ref[...].astype(BF16)

    def rows(x_ref, a_ref, o_ref):
        o_ref[...] = x_ref[...] + _dot(a_ref[...], wbf_ref[...])

    _dual(n_p, rows, (xp, ap, op_ref), (xs, as_, os_ref))


def _ffn_down(x_p, x_s, act_p, act_s, w_all, layer, tm=DOWN_TM, tn=DOWN_TN):
    (m_p, d), m_s = x_p.shape, x_s.shape[0]
    k = w_all.shape[1]
    n_p = m_p // tm
    x_specs = _row_specs(tm, m_s, n_p, tn, True)
    a_specs = _row_specs(tm, m_s, n_p, k, False)
    return pl.pallas_call(
        functools.partial(_ffn_down_kernel, n_p=n_p),
        name="ffn_down",
        grid=(d // tn, n_p + 1),
        in_specs=[x_specs[0], a_specs[0], x_specs[1], a_specs[1],
                  pl.BlockSpec((None, k, tn), lambda j, i: (layer, 0, j))],
        out_specs=list(x_specs),
        out_shape=[jax.ShapeDtypeStruct((m_p, d), F32), jax.ShapeDtypeStruct((m_s, d), F32)],
        scratch_shapes=[pltpu.VMEM((k, tn), BF16)],
        compiler_params=_cparams(2),
    )(x_p, act_p, x_s, act_s, w_all)


def _band_mask(window, n_keys, prev_valid):
    qi = lax.broadcasted_iota(jnp.int32, (BLOCK, n_keys), 0) + (n_keys - BLOCK)
    kj = lax.broadcasted_iota(jnp.int32, (BLOCK, n_keys), 1)
    dist = qi - kj
    band = (dist >= 0) & (dist <= window)
    if n_keys > BLOCK and prev_valid is not None:
        band = band & ((kj >= n_keys - BLOCK) | prev_valid)
    return band


def _attend_pairs(tasks, lane_lo):
    scores = []
    for q2, k2, _, mask in tasks:
        q2 = q2 * (ATTN_SCALE * LOG2E)
        for hh in range(2):
            qm = jnp.where(lane_lo if hh == 0 else ~lane_lo, q2, 0.0).astype(BF16)
            s = lax.dot_general(qm, k2, _NT, preferred_element_type=F32)
            scores.append(jnp.where(mask, s, NEG))
    probs = []
    for s in scores:
        m = jnp.max(s, axis=-1, keepdims=True)
        p = jnp.exp2(s - m)
        probs.append((p.astype(BF16), m * LN2, jnp.sum(p, axis=-1, keepdims=True)))
    out = []
    for t, (_, _, v2, _) in enumerate(tasks):
        out.append([(_dot(probs[2 * t + hh][0], v2),) + probs[2 * t + hh][1:] for hh in range(2)])
    return out


def _attn_a_kernel(q_ref, k_ref, v_ref, o_ref, oacc_ref, lacc_ref, *, seq, branches):
    lane = lax.broadcasted_iota(jnp.int32, (BLOCK, LANES), 1)
    lane_lo = lane < HEAD
    for bi, (window, dil) in enumerate(branches):
        first, last = bi == 0, bi == len(branches) - 1
        mask_first = _band_mask(window // dil, BLOCK, None)
        mask_rest = _band_mask(window // dil, 2 * BLOCK, None)

        def rows(q, r):
            if dil == 1:
                return slice(q * BLOCK, (q + 1) * BLOCK)
            return pl.ds(q * dil * BLOCK + r, BLOCK, stride=dil)

        jobs = [(q, r) for q in range(seq // (dil * BLOCK)) for r in range(dil)]
        group = ATTN_GROUP if first else ATTN_GROUP_MERGING
        for g0 in range(0, len(jobs), group):
            grp = jobs[g0:g0 + group]
            tasks = []
            for q, r in grp:
                cur = rows(q, r)
                if q == 0:
                    k2, v2 = k_ref[cur, :].astype(BF16), v_ref[cur, :].astype(BF16)
                else:
                    k2 = jnp.concatenate([k_ref[rows(q - 1, r), :], k_ref[cur, :]], axis=0).astype(BF16)
                    v2 = jnp.concatenate([v_ref[rows(q - 1, r), :], v_ref[cur, :]], axis=0).astype(BF16)
                tasks.append((q_ref[cur, :], k2, v2, mask_first if q == 0 else mask_rest))
            for (q, r), heads in zip(grp, _attend_pairs(tasks, lane_lo)):
                cur = rows(q, r)
                if not first:
                    o_prev = oacc_ref[cur, :]
                    l_prev = lacc_ref[cur, :]
                halves = []
                lse_blk = jnp.zeros((BLOCK, LANES), F32)
                for hh, (o, m, l) in enumerate(heads):
                    o = o / l
                    lse = m + jnp.log(l)
                    if not first:
                        lse_p = jnp.sum(jnp.where(lane == hh, l_prev, 0.0), axis=-1, keepdims=True)
                        mx = jnp.maximum(lse_p, lse)
                        wp = jnp.exp(lse_p - mx)
                        wi = jnp.exp(lse - mx)
                        den = wp + wi
                        o = (o_prev * wp + o * wi) / den
                        lse = mx + jnp.log(den)
                    halves.append(o)
                    lse_blk = jnp.where(lane == hh, lse, lse_blk)
                o_pair = jnp.where(lane_lo, halves[0], halves[1])
                if last:
                    o_ref[cur, :] = o_pair.astype(o_ref.dtype)
                else:
                    oacc_ref[cur, :] = o_pair
                    lacc_ref[cur, :] = lse_blk


def _attn_a_prompt(p2d, batch, seq):
    branches = tuple(sorted(A_BRANCHES, key=lambda wd: wd[1] == 1))
    blk = (seq, LANES)

    def col(off):
        return lambda b, c: (b, off // LANES + c)

    return pl.pallas_call(
        functools.partial(_attn_a_kernel, seq=seq, branches=branches),
        name="attn_a",
        grid=(batch, A_DIM // LANES),
        in_specs=[pl.BlockSpec(blk, col(OFF_QA)), pl.BlockSpec(blk, col(OFF_KA)), pl.BlockSpec(blk, col(OFF_VA))],
        out_specs=pl.BlockSpec(blk, col(0)),
        out_shape=jax.ShapeDtypeStruct((batch * seq, A_DIM), BF16),
        scratch_shapes=[pltpu.VMEM(blk, F32), pltpu.VMEM(blk, F32)],
        compiler_params=_cparams(2),
    )(p2d, p2d, p2d)


def _attn_c_kernel(sink_ref, q0_ref, q1_ref, q2_ref, q3_ref, kp_ref, kc_ref, vp_ref, vc_ref, o_ref, *, qb):
    q_refs = (q0_ref, q1_ref, q2_ref, q3_ref)
    mask_first = _band_mask(C_WINDOW, 2 * BLOCK, pl.program_id(1) > 0)
    mask_rest = _band_mask(C_WINDOW, 2 * BLOCK, None)
    lane_lo = lax.broadcasted_iota(jnp.int32, (BLOCK, LANES), 1) < HEAD
    lane_lo2 = lax.broadcasted_iota(jnp.int32, (2 * BLOCK, LANES), 1) < HEAD

    def blk_rows(q):
        return slice(q * BLOCK, (q + 1) * BLOCK)

    jobs = []
    for q in range(qb):
        k2 = jnp.concatenate([kp_ref[...] if q == 0 else kc_ref[blk_rows(q - 1), :], kc_ref[blk_rows(q), :]], axis=0)
        v2 = jnp.concatenate([vp_ref[...] if q == 0 else vc_ref[blk_rows(q - 1), :], vc_ref[blk_rows(q), :]], axis=0)
        k2r = pltpu.roll(k2, HEAD, 1)
        v2r = pltpu.roll(v2, HEAD, 1)
        kdup = [jnp.where(lane_lo2, k2, k2r).astype(BF16), jnp.where(lane_lo2, k2r, k2).astype(BF16)]
        vdup = [jnp.where(lane_lo2, v2, v2r).astype(BF16), jnp.where(lane_lo2, v2r, v2).astype(BF16)]
        for pr in range(C_Q_HEADS // 2):
            g = (2 * pr) // C_GROUP
            q2 = q_refs[pr // 2][blk_rows(q), (pr % 2) * LANES:(pr % 2 + 1) * LANES]
            jobs.append((q, pr, (q2, kdup[g], vdup[g], mask_first if q == 0 else mask_rest)))
    for g0 in range(0, len(jobs), ATTN_GROUP):
        grp = jobs[g0:g0 + ATTN_GROUP]
        for (q, pr, _), heads in zip(grp, _attend_pairs([t for _, _, t in grp], lane_lo)):
            halves = []
            for hh, (o, m, l) in enumerate(heads):
                lse = m + jnp.log(l)
                halves.append(o * (_sigmoid(lse - sink_ref[2 * pr + hh]) / l))
            o_ref[blk_rows(q), pr * LANES:(pr + 1) * LANES] = jnp.where(lane_lo, halves[0], halves[1]).astype(o_ref.dtype)


def _attn_c_prompt(p2d, sink, batch, seq):
    rows = batch * seq
    qb = max(1, min(seq // BLOCK, STEP_PAIRS // (C_Q_HEADS // 2)))
    nb = seq // (BLOCK * qb)

    def cur(col):
        return lambda b, n: (b * nb + n, col)

    def prev(col):
        return lambda b, n: (b * nb * qb + jnp.maximum(n * qb - 1, 0), col)

    kv_blk = (BLOCK * qb, C_KV_DIM)
    kv_prev = (BLOCK, C_KV_DIM)
    return pl.pallas_call(
        functools.partial(_attn_c_kernel, qb=qb),
        name="attn_c",
        grid=(batch, nb),
        in_specs=[
            pl.BlockSpec(memory_space=pltpu.SMEM),
            *[pl.BlockSpec((BLOCK * qb, QC_BLK), cur(OFF_QC // QC_BLK + i)) for i in range(C_DIM // QC_BLK)],
            pl.BlockSpec(kv_prev, prev(OFF_KC // C_KV_DIM)),
            pl.BlockSpec(kv_blk, cur(OFF_KC // C_KV_DIM)),
            pl.BlockSpec(kv_prev, prev(OFF_VC // C_KV_DIM)),
            pl.BlockSpec(kv_blk, cur(OFF_VC // C_KV_DIM)),
        ],
        out_specs=pl.BlockSpec((BLOCK * qb, C_DIM), lambda b, n: (b * nb + n, 0)),
        out_shape=jax.ShapeDtypeStruct((rows, C_DIM), BF16),
        compiler_params=_cparams(2),
    )(sink, *([p2d] * (C_DIM // QC_BLK)), p2d, p2d, p2d, p2d)


def _attn_a_sample_kernel(q_ref, kn_ref, vn_ref, kc_ref, vc_ref, cc_ref, cn_ref, o_ref):
    t = q_ref.shape[0]
    pad = jnp.zeros((BLOCK - t, A_DIM), F32)
    k_new = jnp.concatenate([kn_ref[...], pad], axis=0)
    v_new = jnp.concatenate([vn_ref[...], pad], axis=0)
    cnt_c, cnt_n = cc_ref[...], cn_ref[...]
    outs = []
    for h in range(A_HEADS):
        lanes = slice(h * HEAD, (h + 1) * HEAD)
        q = q_ref[:, lanes].astype(BF16)
        s_c = _dot(q, kc_ref[h].astype(BF16)) * ATTN_SCALE
        s_n = lax.dot_general(q, k_new[:, lanes].astype(BF16), _NT, preferred_element_type=F32) * ATTN_SCALE
        s_c = jnp.where(cnt_c > 0.0, s_c, NEG)
        s_n = jnp.where(cnt_n > 0.0, s_n, NEG)
        m = jnp.maximum(jnp.max(s_c, axis=-1, keepdims=True), jnp.max(s_n, axis=-1, keepdims=True))
        p_c = cnt_c * jnp.exp(s_c - m)
        p_n = cnt_n * jnp.exp(s_n - m)
        l = jnp.sum(p_c, axis=-1, keepdims=True) + jnp.sum(p_n, axis=-1, keepdims=True)
        o = lax.dot_general(p_c.astype(BF16), vc_ref[h].astype(BF16), _NT, preferred_element_type=F32)
        o += _dot(p_n.astype(BF16), v_new[:, lanes].astype(BF16))
        outs.append(o / l)
    o_ref[...] = jnp.concatenate(outs, axis=-1)


def _a_sample_counts(t, n_buf):
    qi = n_buf + np.arange(t)[:, None]

    def count(rows):
        delta = qi - rows[None, :]
        c = np.zeros(delta.shape, np.float32)
        for window, dil in A_BRANCHES:
            c += ((delta >= 0) & (delta <= window) & (delta % dil == 0)).astype(np.float32)
        return c

    return count(np.arange(n_buf)), count(n_buf + np.arange(BLOCK))


def _attn_a_sample(ps, cache_k, cache_v, layer, batch, t):
    n_buf = cache_k.shape[4]
    cnt_c, cnt_n = _a_sample_counts(t, n_buf)
    new_blk = (t, A_DIM)
    cache_spec = pl.BlockSpec((None, None, A_HEADS, HEAD, n_buf), lambda b: (layer, b, 0, 0, 0))
    return pl.pallas_call(
        _attn_a_sample_kernel,
        name="attn_a_sample",
        grid=(batch,),
        in_specs=[
            pl.BlockSpec(new_blk, lambda b: (b, OFF_QA // A_DIM)),
            pl.BlockSpec(new_blk, lambda b: (b, OFF_KA // A_DIM)),
            pl.BlockSpec(new_blk, lambda b: (b, OFF_VA // A_DIM)),
            cache_spec, cache_spec,
            pl.BlockSpec(cnt_c.shape, lambda b: (0, 0)),
            pl.BlockSpec(cnt_n.shape, lambda b: (0, 0)),
        ],
        out_specs=pl.BlockSpec(new_blk, lambda b: (b, 0)),
        out_shape=jax.ShapeDtypeStruct((batch * t, A_DIM), F32),
        compiler_params=_cparams(1),
    )(ps, ps, ps, cache_k, cache_v, jnp.asarray(cnt_c), jnp.asarray(cnt_n))


def _attn_c_sample_kernel(q0_ref, q1_ref, q2_ref, q3_ref, kn_ref, vn_ref, kc_ref, vc_ref, sink_ref, o_ref):
    q_refs = (q0_ref, q1_ref, q2_ref, q3_ref)
    t = q0_ref.shape[0]
    n_buf = kc_ref.shape[1]
    rows = C_Q_HEADS * t
    lane_lo = lax.broadcasted_iota(jnp.int32, (t, LANES), 1) < HEAD
    blocks = []
    for j in range(C_Q_HEADS // 2):
        chunk = q_refs[j // 2][:, (j % 2) * LANES:(j % 2 + 1) * LANES]
        rolled = pltpu.roll(chunk, HEAD, 1)
        if (2 * j) // C_GROUP == 0:
            blocks += [jnp.where(lane_lo, chunk, 0.0), jnp.where(lane_lo, rolled, 0.0)]
        else:
            blocks += [jnp.where(lane_lo, 0.0, rolled), jnp.where(lane_lo, 0.0, chunk)]
    qbd = jnp.concatenate(blocks, axis=0).astype(BF16)
    pad = jnp.zeros((BLOCK - t, C_KV_DIM), F32)
    k_new = jnp.concatenate([kn_ref[...], pad], axis=0).astype(BF16)
    v_new = jnp.concatenate([vn_ref[...], pad], axis=0).astype(BF16)
    s_c = _dot(qbd, kc_ref[...].astype(BF16)) * ATTN_SCALE
    s_n = lax.dot_general(qbd, k_new, _NT, preferred_element_type=F32) * ATTN_SCALE
    qt = lax.broadcasted_iota(jnp.int32, (rows, BLOCK), 0) % t
    kj = lax.broadcasted_iota(jnp.int32, (rows, BLOCK), 1)
    dist_c = n_buf + qt - kj
    s_c = jnp.where((dist_c >= 0) & (dist_c <= C_WINDOW), s_c, NEG)
    s_n = jnp.where(kj <= qt, s_n, NEG)
    m = jnp.maximum(jnp.max(s_c, axis=-1, keepdims=True), jnp.max(s_n, axis=-1, keepdims=True))
    p_c = jnp.exp(s_c - m)
    p_n = jnp.exp(s_n - m)
    l = jnp.sum(p_c, axis=-1, keepdims=True) + jnp.sum(p_n, axis=-1, keepdims=True)
    o = lax.dot_general(p_c.astype(BF16), vc_ref[...].astype(BF16), _NT, preferred_element_type=F32)
    o += _dot(p_n.astype(BF16), v_new)
    lse = m + jnp.log(l)
    o = o * (_sigmoid(lse - sink_ref[...]) / l)
    for j in range(C_Q_HEADS // 2):
        blk_a = o[2 * j * t:(2 * j + 1) * t, :]
        blk_b = o[(2 * j + 1) * t:(2 * j + 2) * t, :]
        if (2 * j) // C_GROUP == 0:
            out = jnp.where(lane_lo, blk_a, pltpu.roll(blk_b, HEAD, 1))
        else:
            out = jnp.where(lane_lo, pltpu.roll(blk_a, HEAD, 1), blk_b)
        o_ref[:, j * LANES:(j + 1) * LANES] = out


def _attn_c_sample(ps, cache_k, cache_v, layer, sink_col, batch, t):
    n_buf = cache_k.shape[3]
    assert n_buf == BLOCK
    kv_blk = (t, C_KV_DIM)
    cache_spec = pl.BlockSpec((None, None, C_KV_DIM, n_buf), lambda b: (layer, b, 0, 0))
    return pl.pallas_call(
        _attn_c_sample_kernel,
        name="attn_c_sample",
        grid=(batch,),
        in_specs=[
            *[pl.BlockSpec((t, QC_BLK), functools.partial(lambda b, i: (b, OFF_QC // QC_BLK + i), i=i))
              for i in range(C_DIM // QC_BLK)],
            pl.BlockSpec(kv_blk, lambda b: (b, OFF_KC // C_KV_DIM)),
            pl.BlockSpec(kv_blk, lambda b: (b, OFF_VC // C_KV_DIM)),
            cache_spec, cache_spec,
            pl.BlockSpec(sink_col.shape, lambda b: (0, 0)),
        ],
        out_specs=pl.BlockSpec((t, C_DIM), lambda b: (b, 0)),
        out_shape=jax.ShapeDtypeStruct((batch * t, C_DIM), F32),
        compiler_params=_cparams(1),
    )(*([ps] * (C_DIM // QC_BLK)), ps, ps, cache_k, cache_v, sink_col)


def _rwkv_kernel(r_ref, k_ref, v_ref, lo_ref, shift_ref, s0_ref, mu_ref, vec_ref, w2_ref, a2_ref, g2_ref,
                 o_ref, sout_ref,
                 s_scr, r_scr, d_scr, k_scr, v_scr, kk_scr, b_scr, g_scr, y_scr, wr_scr, kr_scr,
                 cx_scr, cl_scr, *, nb, tc):
    c = pl.program_id(1)
    n_pairs = B_HEADS // 2
    f32 = F32

    li2 = lax.broadcasted_iota(jnp.int32, (2 * LANES, 2 * LANES), 0) // HEAD
    lj2 = lax.broadcasted_iota(jnp.int32, (2 * LANES, 2 * LANES), 1) // HEAD
    bd2 = (li2 == lj2).astype(BF16)

    def head_sum(x):
        return jnp.concatenate([_seg_sum(x[:, j:j + 2 * LANES], bd2) for j in range(0, B_DIM, 2 * LANES)], axis=-1)

    @pl.when(c == 0)
    def _():
        for b in range(nb):
            for p in range(n_pairs):
                s_scr[b * n_pairs + p] = jnp.concatenate([s0_ref[b, 2 * p], s0_ref[b, 2 * p + 1]], axis=-1)
            cx_scr[b] = jnp.broadcast_to(shift_ref[0, b:b + 1, 0:3 * B_DIM], (SUBLANES, 3 * B_DIM))
            cl_scr[b] = jnp.broadcast_to(shift_ref[0, b:b + 1, 3 * B_DIM:], (SUBLANES, LORA_COLS))

    n_rows = nb * tc
    first_row = lax.broadcasted_iota(jnp.int32, (SUBLANES, 1), 0) == 0

    def merged(ref):
        return ref[...].reshape(n_rows, ref.shape[-1])

    def lerp(cur, carry_scr, lo_col, hi_col):
        sh = pltpu.roll(cur, 1, 0)
        parts = []
        for b in range(nb):
            head = jnp.where(first_row, carry_scr[b, 0:1, lo_col:hi_col], sh[b * tc:b * tc + SUBLANES, :])
            parts += [head] + ([sh[b * tc + SUBLANES:(b + 1) * tc, :]] if tc > SUBLANES else [])
        return jnp.concatenate(parts, axis=0)

    def keep_last(carry_scr, lo_col, hi_col, raw):
        for b in range(nb):
            carry_scr[b, :, lo_col:hi_col] = jnp.broadcast_to(raw[(b + 1) * tc - 1:(b + 1) * tc, :],
                                                             (SUBLANES, hi_col - lo_col))

    w0, a0, k_k, k_a = vec_ref[0:1, :], vec_ref[1:2, :], vec_ref[2:3, :], vec_ref[3:4, :]
    r_k, lnx_w, lnx_b = vec_ref[4:5, :], vec_ref[5:6, :], vec_ref[6:7, :]

    mixed = []
    for i, (ref, scr) in enumerate(((r_ref, cx_scr), (k_ref, cx_scr), (v_ref, cx_scr), (lo_ref, cl_scr))):
        raw = merged(ref)
        lo_col = i * B_DIM if scr is cx_scr else 0
        hi_col = lo_col + raw.shape[1]
        sh = lerp(raw, scr, lo_col, hi_col)
        keep_last(scr, lo_col, hi_col, raw)
        mixed.append(raw + (sh - raw) * mu_ref[:, i * B_DIM:i * B_DIM + raw.shape[1]])
    r, k, v, lo = mixed

    z = w0 + _dot_hi(jnp.tanh(lo), w2_ref[...])
    sp = jnp.maximum(-z, 0.0) + jnp.log(1.0 + jnp.exp(-jnp.abs(z)))
    decay = jnp.exp(-jnp.exp(-sp - 0.5))
    a = _sigmoid(a0 + _dot_hi(lo, a2_ref[...]))
    g = _dot_hi(_sigmoid(lo), g2_ref[...])
    kkr = k * k_k
    kk = kkr * lax.rsqrt(jnp.maximum(head_sum(kkr * kkr), KK_NORM_EPS ** 2))
    k = k * (1.0 + (a - 1.0) * k_a)
    bb = kk * a
    wr = decay * r - kk * head_sum(bb * r)
    kr = head_sum(k * r)
    for scr, val in ((r_scr, r), (d_scr, decay), (k_scr, k), (v_scr, v), (kk_scr, kk), (b_scr, bb), (g_scr, g),
                     (wr_scr, wr), (kr_scr, kr)):
        scr[...] = val.reshape(nb, tc, B_DIM)

    sub = lax.broadcasted_iota(jnp.int32, (HEAD, LANES), 0)
    lane = lax.broadcasted_iota(jnp.int32, (HEAD, LANES), 1)
    diag = (lane % HEAD) == sub
    tok_lane = lane % HEAD
    n_all = nb * n_pairs
    grp = min(n_all, RWKV_MATMUL_PAIRS)
    groups = [list(range(g0, g0 + grp)) for g0 in range(0, n_all, grp)]

    def step8(t8, carry):
        t0 = pl.multiple_of(t8 * SUBLANES, SUBLANES)

        def rows_of(scr):
            return [scr[i // n_pairs, pl.ds(t0, SUBLANES), (i % n_pairs) * LANES:(i % n_pairs + 1) * LANES]
                    for i in range(n_all)]

        kk8, v8, d8, b8, k8, wr8, kr8 = [
            rows_of(scr) for scr in (kk_scr, v_scr, d_scr, b_scr, k_scr, wr_scr, kr_scr)]
        yb = [jnp.zeros((HEAD, LANES), f32) for _ in range(n_all)]

        vb = {}
        pairs_mm = range(SUBLANES // 2)
        for grp_ids in groups:
            lhs = jnp.concatenate(
                [jnp.concatenate([jnp.where(diag, v8[i][2 * m:2 * m + 1, :], 0.0).astype(BF16),
                                  jnp.where(diag, v8[i][2 * m + 1:2 * m + 2, :], 0.0).astype(BF16)], axis=1)
                 for m in pairs_mm for i in grp_ids], axis=0)
            res = _dot(lhs, bd2)
            for mi, m in enumerate(pairs_mm):
                for q, i in enumerate(grp_ids):
                    blk = res[(mi * len(grp_ids) + q) * HEAD:(mi * len(grp_ids) + q + 1) * HEAD, :]
                    vb[(i, 2 * m)] = blk[:, 0:LANES]
                    vb[(i, 2 * m + 1)] = blk[:, LANES:]

        for j in range(SUBLANES):
            for grp_ids in groups:
                lhs = jnp.concatenate(
                    [jnp.concatenate([s_scr[i] * kk8[i][j:j + 1, :], s_scr[i] * wr8[i][j:j + 1, :]],
                                     axis=1).astype(BF16) for i in grp_ids], axis=0)
                res = _dot(lhs, bd2)
                for q, i in enumerate(grp_ids):
                    sa = res[q * HEAD:(q + 1) * HEAD, 0:LANES]
                    u = res[q * HEAD:(q + 1) * HEAD, LANES:]
                    yb[i] = jnp.where(tok_lane == j, u, yb[i])
                    s_scr[i] = s_scr[i] * d8[i][j:j + 1, :] - sa * b8[i][j:j + 1, :] + vb[(i, j)] * k8[i][j:j + 1, :]

        for i in range(n_all):
            yt = yb[i].T
            u8 = jnp.concatenate([yt[0:SUBLANES, :], yt[HEAD:HEAD + SUBLANES, :]], axis=1)
            y_scr[i // n_pairs, pl.ds(t0, SUBLANES), (i % n_pairs) * LANES:(i % n_pairs + 1) * LANES] = (
                u8 + v8[i] * kr8[i])
        return carry

    lax.fori_loop(0, tc // SUBLANES, step8, 0, unroll=4 if tc // SUBLANES >= 4 else 1)

    y = merged(y_scr)
    yc = y - head_sum(y) * (1.0 / HEAD)
    var = head_sum(yc * yc) * (1.0 / HEAD)
    yn = yc * lax.rsqrt(var + GN_EPS) * lnx_w + lnx_b
    bonus = head_sum(merged(r_scr) * merged(k_scr) * r_k) * merged(v_scr)
    o_ref[...] = ((yn + bonus) * merged(g_scr)).reshape(nb, tc, B_DIM).astype(o_ref.dtype)

    @pl.when(c == pl.num_programs(1) - 1)
    def _():
        for b in range(nb):
            for p in range(n_pairs):
                s = s_scr[b * n_pairs + p]
                sout_ref[b, 2 * p] = s[:, 0:HEAD]
                sout_ref[b, 2 * p + 1] = s[:, HEAD:]


def _rwkv(p3d, shift0, s0, mu, vecs, w2p, a2p, g2p, nb, tc, out_dtype=F32):
    batch, seq, _ = p3d.shape
    groups = batch // nb
    chunks = seq // tc
    x_blk = (nb, tc, B_DIM)

    def xmap(col):
        return lambda g, c: (g, c, col)

    const2 = lambda g, c: (0, 0)
    scr = lambda *shape: pltpu.VMEM(shape, F32)
    o, s_out = pl.pallas_call(
        functools.partial(_rwkv_kernel, nb=nb, tc=tc),
        name="rwkv7",
        grid=(groups, chunks),
        in_specs=[
            pl.BlockSpec(x_blk, xmap(OFF_PB // B_DIM)),
            pl.BlockSpec(x_blk, xmap(OFF_PB // B_DIM + 1)),
            pl.BlockSpec(x_blk, xmap(OFF_PB // B_DIM + 2)),
            pl.BlockSpec((nb, tc, LORA_COLS), xmap(OFF_LORA // LORA_COLS)),
            pl.BlockSpec((1, nb, B_COLS), lambda g, c: (g, 0, 0)),
            pl.BlockSpec((nb, B_HEADS, HEAD, HEAD), lambda g, c: (g, 0, 0, 0)),
            pl.BlockSpec((1, B_COLS), const2),
            pl.BlockSpec((SUBLANES, B_DIM), const2),
            pl.BlockSpec((LORA_COLS, B_DIM), const2),
            pl.BlockSpec((LORA_COLS, B_DIM), const2),
            pl.BlockSpec((LORA_COLS, B_DIM), const2),
        ],
        out_specs=[
            pl.BlockSpec(x_blk, lambda g, c: (g, c, 0)),
            pl.BlockSpec((nb, B_HEADS, HEAD, HEAD), lambda g, c: (g, 0, 0, 0)),
        ],
        out_shape=[jax.ShapeDtypeStruct((batch, seq, B_DIM), out_dtype),
                   jax.ShapeDtypeStruct((batch, B_HEADS, HEAD, HEAD), F32)],
        scratch_shapes=[scr(nb * B_HEADS // 2, HEAD, LANES)] + [scr(nb, tc, B_DIM)] * 10
        + [scr(nb, SUBLANES, 3 * B_DIM), scr(nb, SUBLANES, LORA_COLS)],
        compiler_params=_cparams(2),
    )(p3d, p3d, p3d, p3d, shift0.reshape(groups, nb, B_COLS), s0, mu, vecs, w2p, a2p, g2p)
    return o, s_out


def _rope_tables(pos):
    half = ROPE_DIM // 2
    inv = jnp.exp(-math.log(ROPE_THETA) * jnp.arange(half, dtype=F32) * 2.0 / ROPE_DIM)
    ang = pos.astype(F32)[:, None] * inv[None, :]
    cos, sin = jnp.cos(ang), jnp.sin(ang)
    lm = np.arange(LANES) % HEAD
    first = jnp.asarray(lm < half)[None, :]
    second = jnp.asarray((lm >= half) & (lm < ROPE_DIM))[None, :]
    freq = np.where(lm < half, lm, np.where(lm < ROPE_DIM, lm - half, 0))
    cos_l, sin_l = cos[:, freq], sin[:, freq]
    c = jnp.where(first | second, cos_l, 1.0)
    s1 = jnp.where(first, -sin_l, 0.0)
    s2 = jnp.where(second, sin_l, 0.0)
    return c, s1, s2


def _rope_flag():
    col = np.arange(IN_COLS)
    rope = (col < OFF_VA) | ((col >= OFF_QC) & (col < OFF_VC))
    tiles = tuple(int(t) for t in np.nonzero(rope.reshape(-1, IN_TN).any(axis=1))[0])
    return jnp.asarray(rope.astype(np.float32))[None, :], tiles


def _pad_rows(w, start):
    return jnp.zeros((LORA_COLS, B_DIM), F32).at[start:start + w.shape[0]].set(w)


def _window_stack_kernel(*refs, depth):
    ins, (ko_ref, vo_ref) = refs[:2 * depth], refs[2 * depth:]
    for l in range(depth):
        @pl.when(pl.program_id(0) == l)
        def _():
            ko_ref[...] = ins[2 * l][...].T
            vo_ref[...] = ins[2 * l + 1][...].T


def _window_stack(p_layers, batch, seq, keep):
    depth = len(p_layers)
    tb = min(keep, WINDOW_STACK_ROWS)
    nblk, first = seq // tb, (seq - keep) // tb

    def in_spec(l, off):
        return pl.BlockSpec((tb, A_DIM), lambda g, b, t: (jnp.where(g == l, b * nblk + first + t, 0), off // A_DIM))

    out_spec = pl.BlockSpec((None, None, A_DIM, tb), lambda g, b, t: (g, b, 0, t))
    shape = jax.ShapeDtypeStruct((depth, batch, A_DIM, keep), F32)
    k, v = pl.pallas_call(
        functools.partial(_window_stack_kernel, depth=depth),
        name="window_stack",
        grid=(depth, batch, keep // tb),
        in_specs=[in_spec(l, off) for l in range(depth) for off in (OFF_KA, OFF_VA)],
        out_specs=[out_spec, out_spec],
        out_shape=[shape, shape],
        compiler_params=_cparams(3),
    )(*[p for p in p_layers for _ in range(2)])

    def as_output(x):
        return jnp.transpose(x.reshape(depth, batch, A_HEADS, HEAD, keep), (0, 1, 4, 2, 3))

    return as_output(k), as_output(v)


def _mixers(p2d, batch, seq, layer, is_prompt, cache, rwkv_w, sink):
    p3d = p2d.reshape(batch, seq, IN_COLS)
    mu, vecs, w2p, a2p, g2p = rwkv_w
    if is_prompt:
        oa = _attn_a_prompt(p2d, batch, seq)
        oc = _attn_c_prompt(p2d, sink, batch, seq)
        shift0 = jnp.zeros((batch, B_COLS), F32)
        s0 = jnp.zeros((batch, B_HEADS, HEAD, HEAD), F32)
        ob, wkv = _rwkv(p3d, shift0, s0, mu, vecs, w2p, a2p, g2p, nb=batch, tc=min(seq, 128), out_dtype=BF16)
    else:
        a_k, a_v, c_k, c_v, wkv0, shift0 = cache
        oa = _attn_a_sample(p2d, a_k, a_v, layer, batch, seq)
        sink_col = jnp.repeat(sink, seq)[:, None]
        oc = _attn_c_sample(p2d, c_k, c_v, layer, sink_col, batch, seq)
        ob, wkv = _rwkv(p3d, shift0[layer], wkv0[layer], mu, vecs, w2p, a2p, g2p, nb=SAMPLE_RWKV_BATCH, tc=seq)
    return oa, ob.reshape(batch * seq, B_DIM), oc, wkv


def kernel(x_prompt, x_sample, cache_a_k, cache_a_v, cache_c_k, cache_c_v, state_b_wkv, state_b_shift, g_mix, w_in, w_out, b_mu, b_w0, b_w2, b_a0, b_a2, b_g2, b_k_k, b_k_a, b_r_k, b_lnx_w, b_lnx_b, c_sink, g_ffn, w_gate, w_up, w_down, g_final):
    depth = w_in.shape[0]
    bp, lp, d = x_prompt.shape
    bs, ls, _ = x_sample.shape
    a_win = cache_a_k.shape[2]
    c_win = cache_c_k.shape[2]
    assert (bp * lp) % PROJ_TM == 0 and lp % PROJ_TM == 0 and lp % (16 * BLOCK) == 0 and a_win >= A_BRANCHES[-1][0] and c_win == C_WINDOW and bs % SAMPLE_RWKV_BATCH == 0

    flag, rope_tiles = _rope_flag()
    tabs_p = _rope_tables(jnp.arange(lp, dtype=jnp.int32))
    tabs_s = _rope_tables(jnp.tile(PAST_LEN + jnp.arange(ls, dtype=jnp.int32), bs))
    cak = jnp.transpose(cache_a_k, (0, 1, 3, 4, 2))
    cav = jnp.transpose(cache_a_v, (0, 1, 3, 4, 2))
    cck = jnp.transpose(cache_c_k, (0, 1, 3, 4, 2)).reshape(depth, bs, C_KV_DIM, c_win)
    ccv = jnp.transpose(cache_c_v, (0, 1, 3, 4, 2)).reshape(depth, bs, C_KV_DIM, c_win)
    cache = (cak, cav, cck, ccv, state_b_wkv, state_b_shift)

    xp = x_prompt.reshape(bp * lp, d)
    xs = x_sample.reshape(bs * ls, d)
    new_p = [[] for _ in range(6)]
    new_s = [[] for _ in range(6)]
    p_prompt = []
    for l in range(depth):
        vecs = jnp.stack([b_w0[l], b_a0[l], b_k_k[l], b_k_a[l], b_r_k[l], b_lnx_w[l], b_lnx_b[l],
                          jnp.zeros((B_DIM,), F32)], axis=0)
        a2_row = b_w2.shape[1]
        g2_row = a2_row + b_a2.shape[1]
        rwkv_w = (b_mu[l][None, :], vecs, _pad_rows(b_w2[l], 0), _pad_rows(b_a2[l], a2_row), _pad_rows(b_g2[l], g2_row))
        sink = c_sink[l].reshape(C_Q_HEADS)
        hp, hs = _rmsnorm(xp, g_mix[l], BF16), _rmsnorm(xs, g_mix[l], BF16)
        pp, ps = _inproj(hp, hs, w_in, l, flag, rope_tiles, tabs_p, tabs_s, PROJ_TM)
        oa_p, ob_p, oc_p, wkv_p = _mixers(pp, bp, lp, l, True, cache, rwkv_w, sink)
        oa_s, ob_s, oc_s, wkv_s = _mixers(ps, bs, ls, l, False, cache, rwkv_w, sink)
        xp, xs = _outproj((xp, xs), (oa_p, oa_s), (ob_p, ob_s), (oc_p, oc_s), w_out, l)
        hp, hs = _rmsnorm(xp, g_ffn[l], BF16), _rmsnorm(xs, g_ffn[l], BF16)
        act_p, act_s = _ffn_up(hp, hs, w_gate, w_up, l)
        xp, xs = _ffn_down(xp, xs, act_p, act_s, w_down, l)
        for p2d, batch, seq, wkv, new, is_prompt in ((pp, bp, lp, wkv_p, new_p, True), (ps, bs, ls, wkv_s, new_s, False)):
            p3d = p2d.reshape(batch, seq, IN_COLS)
            a_keep = min(a_win, seq) if is_prompt else seq
            c_keep = min(c_win, seq) if is_prompt else seq
            state = (
                p3d[:, seq - a_keep:, OFF_KA:OFF_VA].reshape(batch, a_keep, A_HEADS, HEAD),
                p3d[:, seq - a_keep:, OFF_VA:OFF_PB].reshape(batch, a_keep, A_HEADS, HEAD),
                p3d[:, seq - c_keep:, OFF_KC:OFF_VC].reshape(batch, c_keep, C_KV_DIM // HEAD, HEAD),
                p3d[:, seq - c_keep:, OFF_VC:].reshape(batch, c_keep, C_KV_DIM // HEAD, HEAD),
                wkv,
                p3d[:, -1, OFF_PB:OFF_QC],
            )
            for i in range(2 if is_prompt else 0, 6):
                new[i].append(state[i])
        p_prompt.append(pp)
    y_prompt = _rmsnorm(xp, g_final, F32).reshape(bp, lp, d)
    y_sample = _rmsnorm(xs, g_final, F32).reshape(bs, ls, d)
    outs_p = list(_window_stack(p_prompt, bp, lp, min(a_win, lp))) + [jnp.stack(t, axis=0) for t in new_p[2:]]
    outs_s = [jnp.stack(t, axis=0) for t in new_s]
    return (y_prompt, y_sample, *outs_p, *outs_s)
```

```python
import functools
import math

import numpy as np
import jax
import jax.numpy as jnp
from jax import lax
from jax.experimental import pallas as pl
from jax.experimental.pallas import tpu as pltpu

F32 = jnp.float32
BF16 = jnp.bfloat16

LANES = 128
SUBLANES = 8
VMEM_LIMIT = 52 * 1024 * 1024

D_MODEL = 2048
HEAD = 64
A_DIM = 512
B_DIM = 512
C_DIM = 1024
C_KV_DIM = 128
A_HEADS = 8
B_HEADS = 8
C_Q_HEADS = 16
C_GROUP = 8
LORA_COLS = 256
B_COLS = 3 * B_DIM + LORA_COLS
IN_COLS = 3 * A_DIM + B_COLS + C_DIM + 2 * C_KV_DIM
D_FF = 5632
OFF_QA, OFF_KA, OFF_VA = 0, A_DIM, 2 * A_DIM
OFF_PB = 3 * A_DIM
OFF_LORA = OFF_PB + 3 * B_DIM
OFF_QC = OFF_PB + B_COLS
OFF_KC = OFF_QC + C_DIM
OFF_VC = OFF_KC + C_KV_DIM
A_BRANCHES = ((128, 1), (512, 4), (2048, 16))
C_WINDOW = 128
BLOCK = 128
QC_BLK = 256
IN_TN = 512
PROJ_TM = 1024
UP_TN = 512
OUT_TM, OUT_TN = 512, 1024
DOWN_TM, DOWN_TN = 512, 512
ATTN_GROUP = 4
ATTN_GROUP_MERGING = 2
WINDOW_STACK_ROWS = 1024
SAMPLE_RWKV_BATCH = 8
RWKV_MATMUL_PAIRS = 16
PAST_LEN = 16384
ROPE_THETA = 500000.0
ROPE_DIM = 16
RMS_EPS = 1e-6
GN_EPS = 64e-5
KK_NORM_EPS = 1e-12
ATTN_SCALE = HEAD ** -0.5
NEG = -1e30
LOG2E = 1.4426950408889634
LN2 = 0.6931471805599453

_NT = (((1,), (1,)), ((), ()))


def _cparams(n_grid):
    return pltpu.CompilerParams(dimension_semantics=("arbitrary",) * n_grid, vmem_limit_bytes=VMEM_LIMIT)


def _dot(a, b):
    return jnp.dot(a, b, preferred_element_type=F32)


def _split_bf16(x):
    hi = x.astype(BF16)
    lo = (x - hi.astype(F32)).astype(BF16)
    return hi, lo


def _dot_hi(a, b):
    a_hi, a_lo = _split_bf16(a)
    b_hi, b_lo = _split_bf16(b)
    return _dot(a_hi, b_hi) + (_dot(a_lo, b_hi) + _dot(a_hi, b_lo))


def _seg_sum(x, bd):
    hi, lo = _split_bf16(x)
    return _dot(hi, bd) + _dot(lo, bd)


def _sigmoid(x):
    return 1.0 / (1.0 + jnp.exp(-x))


def _rmsnorm_kernel(x_ref, g_ref, o_ref):
    x = x_ref[...]
    ms = jnp.mean(x * x, axis=-1, keepdims=True)
    o_ref[...] = (x * lax.rsqrt(ms + RMS_EPS) * g_ref[...]).astype(o_ref.dtype)


def _rmsnorm(x, g, out_dtype):
    m, d = x.shape
    tm = min(m, 512)
    return pl.pallas_call(
        _rmsnorm_kernel,
        name="rmsnorm",
        grid=(m // tm,),
        in_specs=[pl.BlockSpec((tm, d), lambda i: (i, 0)), pl.BlockSpec((1, d), lambda i: (0, 0))],
        out_specs=pl.BlockSpec((tm, d), lambda i: (i, 0)),
        out_shape=jax.ShapeDtypeStruct((m, d), out_dtype),
        compiler_params=_cparams(1),
    )(x, g.reshape(1, d))


def _row_specs(tm_p, tm_s, n_p, width, tiled, cycle=None):
    col = (lambda j: j) if tiled else (lambda j: 0)
    row = (lambda i: jnp.minimum(i, n_p - 1)) if cycle is None else (lambda i: jnp.minimum(i, n_p - 1) % cycle)
    return (pl.BlockSpec((tm_p, width), lambda j, i: (row(i), col(j))),
            pl.BlockSpec((tm_s, width), lambda j, i: (0, col(j))))


def _dual(n_p, tile_fn, prompt_refs, sample_refs):
    i = pl.program_id(1)

    @pl.when(i < n_p)
    def _():
        tile_fn(*prompt_refs)

    @pl.when(i == n_p)
    def _():
        tile_fn(*sample_refs)


def _inproj_kernel(hp, cp, s1p, s2p, hs, cs, s1s, s2s, w_ref, flag_ref, op_ref, os_ref, wbf_ref, *, rope_tiles, n_p):
    @pl.when(pl.program_id(1) == 0)
    def _():
        wbf_ref[...] = w_ref[...].astype(BF16)

    tile = pl.program_id(0)
    has_rope = functools.reduce(jnp.logical_or, [tile == t for t in rope_tiles])

    def rows(h_ref, c_ref, s1_ref, s2_ref, o_ref):
        @pl.when(jnp.logical_not(has_rope))
        def _():
            o_ref[...] = _dot(h_ref[...], wbf_ref[...])

        @pl.when(has_rope)
        def _():
            c, s1, s2 = c_ref[...], s1_ref[...], s2_ref[...]
            half = 2 * LANES
            for h0 in range(0, o_ref.shape[1], half):
                acc = _dot(h_ref[...], wbf_ref[:, h0:h0 + half])
                for j in range(half // LANES):
                    sl = slice(h0 + j * LANES, h0 + (j + 1) * LANES)
                    x = acc[:, j * LANES:(j + 1) * LANES]
                    rot = x * c + pltpu.roll(x, LANES - 8, 1) * s1 + pltpu.roll(x, 8, 1) * s2
                    o_ref[:, sl] = jnp.where(flag_ref[:, sl] > 0.0, rot, x)

    _dual(n_p, rows, (hp, cp, s1p, s2p, op_ref), (hs, cs, s1s, s2s, os_ref))


def _inproj(h_p, h_s, w_all, layer, flag, rope_tiles, tabs_p, tabs_s, tm, tn=IN_TN):
    (m_p, k), m_s = h_p.shape, h_s.shape[0]
    n = w_all.shape[2]
    n_p = m_p // tm
    h_specs = _row_specs(tm, m_s, n_p, k, False)
    tab_specs = _row_specs(tm, m_s, n_p, LANES, False, cycle=tabs_p[0].shape[0] // tm)
    out_specs = _row_specs(tm, m_s, n_p, tn, True)
    return pl.pallas_call(
        functools.partial(_inproj_kernel, rope_tiles=rope_tiles, n_p=n_p),
        name="inproj",
        grid=(n // tn, n_p + 1),
        in_specs=[h_specs[0]] + [tab_specs[0]] * 3 + [h_specs[1]] + [tab_specs[1]] * 3 + [
            pl.BlockSpec((None, k, tn), lambda j, i: (layer, 0, j)),
            pl.BlockSpec((1, tn), lambda j, i: (0, j)),
        ],
        out_specs=list(out_specs),
        out_shape=[jax.ShapeDtypeStruct((m_p, n), F32), jax.ShapeDtypeStruct((m_s, n), F32)],
        scratch_shapes=[pltpu.VMEM((k, tn), BF16)],
        compiler_params=_cparams(2),
    )(h_p, *tabs_p, h_s, *tabs_s, w_all, flag)


def _outproj_kernel(xp, oap, obp, ocp, xs, oas, obs, ocs, w_ref, op_ref, os_ref, wbf_ref, *, n_p):
    @pl.when(pl.program_id(1) == 0)
    def _():
        wbf_ref[...] = w_ref[...].astype(BF16)

    def rows(x_ref, oa_ref, ob_ref, oc_ref, o_ref):
        acc = _dot(oa_ref[...].astype(BF16), wbf_ref[0:A_DIM, :])
        acc += _dot(ob_ref[...].astype(BF16), wbf_ref[A_DIM:A_DIM + B_DIM, :])
        acc += _dot(oc_ref[...].astype(BF16), wbf_ref[A_DIM + B_DIM:, :])
        o_ref[...] = x_ref[...] + acc

    _dual(n_p, rows, (xp, oap, obp, ocp, op_ref), (xs, oas, obs, ocs, os_ref))


def _outproj(x, oa, ob, oc, w_all, layer, tm=OUT_TM, tn=OUT_TN):
    (m_p, d), m_s = x[0].shape, x[1].shape[0]
    k = w_all.shape[1]
    n_p = m_p // tm
    specs = [_row_specs(tm, m_s, n_p, tn, True)] + [_row_specs(tm, m_s, n_p, w, False) for w in (A_DIM, B_DIM, C_DIM)]
    return pl.pallas_call(
        functools.partial(_outproj_kernel, n_p=n_p),
        name="outproj",
        grid=(d // tn, n_p + 1),
        in_specs=[s[0] for s in specs] + [s[1] for s in specs] + [
            pl.BlockSpec((None, k, tn), lambda j, i: (layer, 0, j))],
        out_specs=list(specs[0]),
        out_shape=[jax.ShapeDtypeStruct((m_p, d), F32), jax.ShapeDtypeStruct((m_s, d), F32)],
        scratch_shapes=[pltpu.VMEM((k, tn), BF16)],
        compiler_params=_cparams(2),
    )(x[0], oa[0], ob[0], oc[0], x[1], oa[1], ob[1], oc[1], w_all)


def _ffn_up_kernel(hp, hs, wg_ref, wu_ref, op_ref, os_ref, wgbf_ref, wubf_ref, *, n_p):
    @pl.when(pl.program_id(1) == 0)
    def _():
        wgbf_ref[...] = wg_ref[...].astype(BF16)
        wubf_ref[...] = wu_ref[...].astype(BF16)

    def rows(h_ref, o_ref):
        h = h_ref[...]
        gate = _dot(h, wgbf_ref[...])
        up = _dot(h, wubf_ref[...])
        o_ref[...] = (gate * _sigmoid(gate) * up).astype(o_ref.dtype)

    _dual(n_p, rows, (hp, op_ref), (hs, os_ref))


def _ffn_up(h_p, h_s, wg_all, wu_all, layer, tm=PROJ_TM, tn=UP_TN):
    (m_p, k), m_s = h_p.shape, h_s.shape[0]
    n = wg_all.shape[2]
    n_p = m_p // tm
    w_spec = pl.BlockSpec((None, k, tn), lambda j, i: (layer, 0, j))
    return pl.pallas_call(
        functools.partial(_ffn_up_kernel, n_p=n_p),
        name="ffn_up",
        grid=(n // tn, n_p + 1),
        in_specs=list(_row_specs(tm, m_s, n_p, k, False)) + [w_spec, w_spec],
        out_specs=list(_row_specs(tm, m_s, n_p, tn, True)),
        out_shape=[jax.ShapeDtypeStruct((m_p, n), BF16), jax.ShapeDtypeStruct((m_s, n), BF16)],
        scratch_shapes=[pltpu.VMEM((k, tn), BF16), pltpu.VMEM((k, tn), BF16)],
        compiler_params=_cparams(2),
    )(h_p, h_s, wg_all, wu_all)


def _ffn_down_kernel(xp, ap, xs, as_, w_ref, op_ref, os_ref, wbf_ref, *, n_p):
    @pl.when(pl.program_id(1) == 0)
    def _():
        wbf_ref[...] = w_ref[...].astype(BF16)

    def rows(x_ref, a_ref, o_ref):
        o_ref[...] = x_ref[...] + _dot(a_ref[...], wbf_ref[...])

    _dual(n_p, rows, (xp, ap, op_ref), (xs, as_, os_ref))


def _ffn_down(x_p, x_s, act_p, act_s, w_all, layer, tm=DOWN_TM, tn=DOWN_TN):
    (m_p, d), m_s = x_p.shape, x_s.shape[0]
    k = w_all.shape[1]
    n_p = m_p // tm
    x_specs = _row_specs(tm, m_s, n_p, tn, True)
    a_specs = _row_specs(tm, m_s, n_p, k, False)
    return pl.pallas_call(
        functools.partial(_ffn_down_kernel, n_p=n_p),
        name="ffn_down",
        grid=(d // tn, n_p + 1),
        in_specs=[x_specs[0], a_specs[0], x_specs[1], a_specs[1],
                  pl.BlockSpec((None, k, tn), lambda j, i: (layer, 0, j))],
        out_specs=list(x_specs),
        out_shape=[jax.ShapeDtypeStruct((m_p, d), F32), jax.ShapeDtypeStruct((m_s, d), F32)],
        scratch_shapes=[pltpu.VMEM((k, tn), BF16)],
        compiler_params=_cparams(2),
    )(x_p, act_p, x_s, act_s, w_all)


def _band_mask(window, n_keys, prev_valid):
    qi = lax.broadcasted_iota(jnp.int32, (BLOCK, n_keys), 0) + (n_keys - BLOCK)
    kj = lax.broadcasted_iota(jnp.int32, (BLOCK, n_keys), 1)
    dist = qi - kj
    band = (dist >= 0) & (dist <= window)
    if n_keys > BLOCK and prev_valid is not None:
        band = band & ((kj >= n_keys - BLOCK) | prev_valid)
    return band


def _attend_pairs(tasks, lane_lo):
    scores = []
    for q2, k2, _, mask in tasks:
        q2 = q2 * (ATTN_SCALE * LOG2E)
        for hh in range(2):
            qm = jnp.where(lane_lo if hh == 0 else ~lane_lo, q2, 0.0).astype(BF16)
            s = lax.dot_general(qm, k2, _NT, preferred_element_type=F32)
            scores.append(jnp.where(mask, s, NEG))
    probs = []
    for s in scores:
        m = jnp.max(s, axis=-1, keepdims=True)
        p = jnp.exp2(s - m)
        probs.append((p.astype(BF16), m * LN2, jnp.sum(p, axis=-1, keepdims=True)))
    out = []
    for t, (_, _, v2, _) in enumerate(tasks):
        out.append([(_dot(probs[2 * t + hh][0], v2),) + probs[2 * t + hh][1:] for hh in range(2)])
    return out


def _attn_a_kernel(q_ref, k_ref, v_ref, o_ref, oacc_ref, lacc_ref, *, seq, branches):
    lane = lax.broadcasted_iota(jnp.int32, (BLOCK, LANES), 1)
    lane_lo = lane < HEAD
    for bi, (window, dil) in enumerate(branches):
        first, last = bi == 0, bi == len(branches) - 1
        mask_first = _band_mask(window // dil, BLOCK, None)
        mask_rest = _band_mask(window // dil, 2 * BLOCK, None)

        def rows(q, r):
            if dil == 1:
                return slice(q * BLOCK, (q + 1) * BLOCK)
            return pl.ds(q * dil * BLOCK + r, BLOCK, stride=dil)

        jobs = [(q, r) for q in range(seq // (dil * BLOCK)) for r in range(dil)]
        group = ATTN_GROUP if first else ATTN_GROUP_MERGING
        for g0 in range(0, len(jobs), group):
            grp = jobs[g0:g0 + group]
            tasks = []
            for q, r in grp:
                cur = rows(q, r)
                if q == 0:
                    k2, v2 = k_ref[cur, :].astype(BF16), v_ref[cur, :].astype(BF16)
                else:
                    k2 = jnp.concatenate([k_ref[rows(q - 1, r), :], k_ref[cur, :]], axis=0).astype(BF16)
                    v2 = jnp.concatenate([v_ref[rows(q - 1, r), :], v_ref[cur, :]], axis=0).astype(BF16)
                tasks.append((q_ref[cur, :], k2, v2, mask_first if q == 0 else mask_rest))
            for (q, r), heads in zip(grp, _attend_pairs(tasks, lane_lo)):
                cur = rows(q, r)
                if not first:
                    o_prev = oacc_ref[cur, :]
                    l_prev = lacc_ref[cur, :]
                halves = []
                lse_blk = jnp.zeros((BLOCK, LANES), F32)
                for hh, (o, m, l) in enumerate(heads):
                    o = o / l
                    lse = m + jnp.log(l)
                    if not first:
                        lse_p = jnp.sum(jnp.where(lane == hh, l_prev, 0.0), axis=-1, keepdims=True)
                        mx = jnp.maximum(lse_p, lse)
                        wp = jnp.exp(lse_p - mx)
                        wi = jnp.exp(lse - mx)
                        den = wp + wi
                        o = (o_prev * wp + o * wi) / den
                        lse = mx + jnp.log(den)
                    halves.append(o)
                    lse_blk = jnp.where(lane == hh, lse, lse_blk)
                o_pair = jnp.where(lane_lo, halves[0], halves[1])
                if last:
                    o_ref[cur, :] = o_pair.astype(o_ref.dtype)
                else:
                    oacc_ref[cur, :] = o_pair
                    lacc_ref[cur, :] = lse_blk


def _attn_a_prompt(p2d, batch, seq):
    branches = tuple(sorted(A_BRANCHES, key=lambda wd: wd[1] == 1))
    blk = (seq, LANES)

    def col(off):
        return lambda b, c: (b, off // LANES + c)

    return pl.pallas_call(
        functools.partial(_attn_a_kernel, seq=seq, branches=branches),
        name="attn_a",
        grid=(batch, A_DIM // LANES),
        in_specs=[pl.BlockSpec(blk, col(OFF_QA)), pl.BlockSpec(blk, col(OFF_KA)), pl.BlockSpec(blk, col(OFF_VA))],
        out_specs=pl.BlockSpec(blk, col(0)),
        out_shape=jax.ShapeDtypeStruct((batch * seq, A_DIM), BF16),
        scratch_shapes=[pltpu.VMEM(blk, F32), pltpu.VMEM(blk, F32)],
        compiler_params=_cparams(2),
    )(p2d, p2d, p2d)


def _attn_c_kernel(sink_ref, qa_ref, qb_ref, k_ref, v_ref, o_ref, *, seq):
    q_refs = (qa_ref, qb_ref)
    g = pl.program_id(1)
    pairs = C_GROUP // 2
    mask_first = _band_mask(C_WINDOW, BLOCK, None)
    mask_rest = _band_mask(C_WINDOW, 2 * BLOCK, None)
    lane_lo = lax.broadcasted_iota(jnp.int32, (BLOCK, LANES), 1) < HEAD

    def blk_rows(q):
        return slice(q * BLOCK, (q + 1) * BLOCK)

    def dup(x):
        keep = (lax.broadcasted_iota(jnp.int32, x.shape, 1) < HEAD) == (g == 0)
        return jnp.where(keep, x, pltpu.roll(x, HEAD, 1)).astype(BF16)

    jobs = []
    for q in range(seq // BLOCK):
        if q == 0:
            k2, v2 = k_ref[blk_rows(0), :], v_ref[blk_rows(0), :]
        else:
            k2 = jnp.concatenate([k_ref[blk_rows(q - 1), :], k_ref[blk_rows(q), :]], axis=0)
            v2 = jnp.concatenate([v_ref[blk_rows(q - 1), :], v_ref[blk_rows(q), :]], axis=0)
        kd, vd = dup(k2), dup(v2)
        for pr in range(pairs):
            q2 = q_refs[pr // 2][blk_rows(q), (pr % 2) * LANES:(pr % 2 + 1) * LANES]
            jobs.append((q, pr, (q2, kd, vd, mask_first if q == 0 else mask_rest)))
    for g0 in range(0, len(jobs), ATTN_GROUP):
        grp = jobs[g0:g0 + ATTN_GROUP]
        for (q, pr, _), heads in zip(grp, _attend_pairs([t for _, _, t in grp], lane_lo)):
            halves = []
            for hh, (o, m, l) in enumerate(heads):
                lse = m + jnp.log(l)
                halves.append(o * (_sigmoid(lse - sink_ref[g * C_GROUP + 2 * pr + hh]) / l))
            o_ref[blk_rows(q), pr * LANES:(pr + 1) * LANES] = jnp.where(lane_lo, halves[0], halves[1]).astype(o_ref.dtype)


def _attn_c_prompt(p2d, sink, batch, seq):
    n_groups = C_Q_HEADS // C_GROUP
    grp_cols = C_GROUP * HEAD
    q_specs = [pl.BlockSpec((seq, QC_BLK), functools.partial(
        lambda b, g, i: (b, OFF_QC // QC_BLK + g * (grp_cols // QC_BLK) + i), i=i)) for i in range(grp_cols // QC_BLK)]
    return pl.pallas_call(
        functools.partial(_attn_c_kernel, seq=seq),
        name="attn_c",
        grid=(batch, n_groups),
        in_specs=[
            pl.BlockSpec(memory_space=pltpu.SMEM),
            *q_specs,
            pl.BlockSpec((seq, C_KV_DIM), lambda b, g: (b, OFF_KC // C_KV_DIM)),
            pl.BlockSpec((seq, C_KV_DIM), lambda b, g: (b, OFF_VC // C_KV_DIM)),
        ],
        out_specs=pl.BlockSpec((seq, grp_cols), lambda b, g: (b, g)),
        out_shape=jax.ShapeDtypeStruct((batch * seq, C_DIM), BF16),
        compiler_params=_cparams(2),
    )(sink, *([p2d] * len(q_specs)), p2d, p2d)


def _attn_a_sample_kernel(q_ref, kn_ref, vn_ref, kc_ref, vc_ref, cc_ref, cn_ref, o_ref):
    t = q_ref.shape[0]
    pad = jnp.zeros((BLOCK - t, A_DIM), F32)
    k_new = jnp.concatenate([kn_ref[...], pad], axis=0)
    v_new = jnp.concatenate([vn_ref[...], pad], axis=0)
    cnt_c, cnt_n = cc_ref[...], cn_ref[...]
    outs = []
    for h in range(A_HEADS):
        lanes = slice(h * HEAD, (h + 1) * HEAD)
        q = q_ref[:, lanes].astype(BF16)
        s_c = _dot(q, kc_ref[h].astype(BF16)) * ATTN_SCALE
        s_n = lax.dot_general(q, k_new[:, lanes].astype(BF16), _NT, preferred_element_type=F32) * ATTN_SCALE
        s_c = jnp.where(cnt_c > 0.0, s_c, NEG)
        s_n = jnp.where(cnt_n > 0.0, s_n, NEG)
        m = jnp.maximum(jnp.max(s_c, axis=-1, keepdims=True), jnp.max(s_n, axis=-1, keepdims=True))
        p_c = cnt_c * jnp.exp(s_c - m)
        p_n = cnt_n * jnp.exp(s_n - m)
        l = jnp.sum(p_c, axis=-1, keepdims=True) + jnp.sum(p_n, axis=-1, keepdims=True)
        o = lax.dot_general(p_c.astype(BF16), vc_ref[h].astype(BF16), _NT, preferred_element_type=F32)
        o += _dot(p_n.astype(BF16), v_new[:, lanes].astype(BF16))
        outs.append(o / l)
    o_ref[...] = jnp.concatenate(outs, axis=-1)


def _a_sample_counts(t, n_buf):
    qi = n_buf + np.arange(t)[:, None]

    def count(rows):
        delta = qi - rows[None, :]
        c = np.zeros(delta.shape, np.float32)
        for window, dil in A_BRANCHES:
            c += ((delta >= 0) & (delta <= window) & (delta % dil == 0)).astype(np.float32)
        return c

    return count(np.arange(n_buf)), count(n_buf + np.arange(BLOCK))


def _attn_a_sample(ps, cache_k, cache_v, layer, batch, t):
    n_buf = cache_k.shape[4]
    cnt_c, cnt_n = _a_sample_counts(t, n_buf)
    new_blk = (t, A_DIM)
    cache_spec = pl.BlockSpec((None, None, A_HEADS, HEAD, n_buf), lambda b: (layer, b, 0, 0, 0))
    return pl.pallas_call(
        _attn_a_sample_kernel,
        name="attn_a_sample",
        grid=(batch,),
        in_specs=[
            pl.BlockSpec(new_blk, lambda b: (b, OFF_QA // A_DIM)),
            pl.BlockSpec(new_blk, lambda b: (b, OFF_KA // A_DIM)),
            pl.BlockSpec(new_blk, lambda b: (b, OFF_VA // A_DIM)),
            cache_spec, cache_spec,
            pl.BlockSpec(cnt_c.shape, lambda b: (0, 0)),
            pl.BlockSpec(cnt_n.shape, lambda b: (0, 0)),
        ],
        out_specs=pl.BlockSpec(new_blk, lambda b: (b, 0)),
        out_shape=jax.ShapeDtypeStruct((batch * t, A_DIM), F32),
        compiler_params=_cparams(1),
    )(ps, ps, ps, cache_k, cache_v, jnp.asarray(cnt_c), jnp.asarray(cnt_n))


def _attn_c_sample_kernel(q0_ref, q1_ref, q2_ref, q3_ref, kn_ref, vn_ref, kc_ref, vc_ref, sink_ref, o_ref):
    q_refs = (q0_ref, q1_ref, q2_ref, q3_ref)
    t = q0_ref.shape[0]
    n_buf = kc_ref.shape[1]
    rows = C_Q_HEADS * t
    lane_lo = lax.broadcasted_iota(jnp.int32, (t, LANES), 1) < HEAD
    blocks = []
    for j in range(C_Q_HEADS // 2):
        chunk = q_refs[j // 2][:, (j % 2) * LANES:(j % 2 + 1) * LANES]
        rolled = pltpu.roll(chunk, HEAD, 1)
        if (2 * j) // C_GROUP == 0:
            blocks += [jnp.where(lane_lo, chunk, 0.0), jnp.where(lane_lo, rolled, 0.0)]
        else:
            blocks += [jnp.where(lane_lo, 0.0, rolled), jnp.where(lane_lo, 0.0, chunk)]
    qbd = jnp.concatenate(blocks, axis=0).astype(BF16)
    pad = jnp.zeros((BLOCK - t, C_KV_DIM), F32)
    k_new = jnp.concatenate([kn_ref[...], pad], axis=0).astype(BF16)
    v_new = jnp.concatenate([vn_ref[...], pad], axis=0).astype(BF16)
    s_c = _dot(qbd, kc_ref[...].astype(BF16)) * ATTN_SCALE
    s_n = lax.dot_general(qbd, k_new, _NT, preferred_element_type=F32) * ATTN_SCALE
    qt = lax.broadcasted_iota(jnp.int32, (rows, BLOCK), 0) % t
    kj = lax.broadcasted_iota(jnp.int32, (rows, BLOCK), 1)
    dist_c = n_buf + qt - kj
    s_c = jnp.where((dist_c >= 0) & (dist_c <= C_WINDOW), s_c, NEG)
    s_n = jnp.where(kj <= qt, s_n, NEG)
    m = jnp.maximum(jnp.max(s_c, axis=-1, keepdims=True), jnp.max(s_n, axis=-1, keepdims=True))
    p_c = jnp.exp(s_c - m)
    p_n = jnp.exp(s_n - m)
    l = jnp.sum(p_c, axis=-1, keepdims=True) + jnp.sum(p_n, axis=-1, keepdims=True)
    o = lax.dot_general(p_c.astype(BF16), vc_ref[...].astype(BF16), _NT, preferred_element_type=F32)
    o += _dot(p_n.astype(BF16), v_new)
    lse = m + jnp.log(l)
    o = o * (_sigmoid(lse - sink_ref[...]) / l)
    for j in range(C_Q_HEADS // 2):
        blk_a = o[2 * j * t:(2 * j + 1) * t, :]
        blk_b = o[(2 * j + 1) * t:(2 * j + 2) * t, :]
        if (2 * j) // C_GROUP == 0:
            out = jnp.where(lane_lo, blk_a, pltpu.roll(blk_b, HEAD, 1))
        else:
            out = jnp.where(lane_lo, pltpu.roll(blk_a, HEAD, 1), blk_b)
        o_ref[:, j * LANES:(j + 1) * LANES] = out


def _attn_c_sample(ps, cache_k, cache_v, layer, sink_col, batch, t):
    n_buf = cache_k.shape[3]
    assert n_buf == BLOCK
    kv_blk = (t, C_KV_DIM)
    cache_spec = pl.BlockSpec((None, None, C_KV_DIM, n_buf), lambda b: (layer, b, 0, 0))
    return pl.pallas_call(
        _attn_c_sample_kernel,
        name="attn_c_sample",
        grid=(batch,),
        in_specs=[
            *[pl.BlockSpec((t, QC_BLK), functools.partial(lambda b, i: (b, OFF_QC // QC_BLK + i), i=i))
              for i in range(C_DIM // QC_BLK)],
            pl.BlockSpec(kv_blk, lambda b: (b, OFF_KC // C_KV_DIM)),
            pl.BlockSpec(kv_blk, lambda b: (b, OFF_VC // C_KV_DIM)),
            cache_spec, cache_spec,
            pl.BlockSpec(sink_col.shape, lambda b: (0, 0)),
        ],
        out_specs=pl.BlockSpec((t, C_DIM), lambda b: (b, 0)),
        out_shape=jax.ShapeDtypeStruct((batch * t, C_DIM), F32),
        compiler_params=_cparams(1),
    )(*([ps] * (C_DIM // QC_BLK)), ps, ps, cache_k, cache_v, sink_col)


def _rwkv_kernel(r_ref, k_ref, v_ref, lo_ref, shift_ref, s0_ref, mu_ref, vec_ref, w2_ref, a2_ref, g2_ref,
                 o_ref, sout_ref,
                 s_scr, r_scr, d_scr, k_scr, v_scr, kk_scr, b_scr, g_scr, y_scr, wr_scr, kr_scr,
                 cx_scr, cl_scr, *, nb, tc):
    c = pl.program_id(1)
    n_pairs = B_HEADS // 2
    f32 = F32

    li2 = lax.broadcasted_iota(jnp.int32, (2 * LANES, 2 * LANES), 0) // HEAD
    lj2 = lax.broadcasted_iota(jnp.int32, (2 * LANES, 2 * LANES), 1) // HEAD
    bd2 = (li2 == lj2).astype(BF16)

    def head_sum(x):
        return jnp.concatenate([_seg_sum(x[:, j:j + 2 * LANES], bd2) for j in range(0, B_DIM, 2 * LANES)], axis=-1)

    @pl.when(c == 0)
    def _():
        for b in range(nb):
            for p in range(n_pairs):
                s_scr[b * n_pairs + p] = jnp.concatenate([s0_ref[b, 2 * p], s0_ref[b, 2 * p + 1]], axis=-1)
            cx_scr[b] = jnp.broadcast_to(shift_ref[0, b:b + 1, 0:3 * B_DIM], (SUBLANES, 3 * B_DIM))
            cl_scr[b] = jnp.broadcast_to(shift_ref[0, b:b + 1, 3 * B_DIM:], (SUBLANES, LORA_COLS))

    n_rows = nb * tc
    first_row = lax.broadcasted_iota(jnp.int32, (SUBLANES, 1), 0) == 0

    def merged(ref):
        return ref[...].reshape(n_rows, ref.shape[-1])

    def lerp(cur, carry_scr, lo_col, hi_col):
        sh = pltpu.roll(cur, 1, 0)
        parts = []
        for b in range(nb):
            head = jnp.where(first_row, carry_scr[b, 0:1, lo_col:hi_col], sh[b * tc:b * tc + SUBLANES, :])
            parts += [head] + ([sh[b * tc + SUBLANES:(b + 1) * tc, :]] if tc > SUBLANES else [])
        return jnp.concatenate(parts, axis=0)

    def keep_last(carry_scr, lo_col, hi_col, raw):
        for b in range(nb):
            carry_scr[b, :, lo_col:hi_col] = jnp.broadcast_to(raw[(b + 1) * tc - 1:(b + 1) * tc, :],
                                                             (SUBLANES, hi_col - lo_col))

    w0, a0, k_k, k_a = vec_ref[0:1, :], vec_ref[1:2, :], vec_ref[2:3, :], vec_ref[3:4, :]
    r_k, lnx_w, lnx_b = vec_ref[4:5, :], vec_ref[5:6, :], vec_ref[6:7, :]

    mixed = []
    for i, (ref, scr) in enumerate(((r_ref, cx_scr), (k_ref, cx_scr), (v_ref, cx_scr), (lo_ref, cl_scr))):
        raw = merged(ref)
        lo_col = i * B_DIM if scr is cx_scr else 0
        hi_col = lo_col + raw.shape[1]
        sh = lerp(raw, scr, lo_col, hi_col)
        keep_last(scr, lo_col, hi_col, raw)
        mixed.append(raw + (sh - raw) * mu_ref[:, i * B_DIM:i * B_DIM + raw.shape[1]])
    r, k, v, lo = mixed

    z = w0 + _dot_hi(jnp.tanh(lo), w2_ref[...])
    sp = jnp.maximum(-z, 0.0) + jnp.log(1.0 + jnp.exp(-jnp.abs(z)))
    decay = jnp.exp(-jnp.exp(-sp - 0.5))
    a = _sigmoid(a0 + _dot_hi(lo, a2_ref[...]))
    g = _dot_hi(_sigmoid(lo), g2_ref[...])
    kkr = k * k_k
    kk = kkr * lax.rsqrt(jnp.maximum(head_sum(kkr * kkr), KK_NORM_EPS ** 2))
    k = k * (1.0 + (a - 1.0) * k_a)
    bb = kk * a
    wr = decay * r - kk * head_sum(bb * r)
    kr = head_sum(k * r)
    for scr, val in ((r_scr, r), (d_scr, decay), (k_scr, k), (v_scr, v), (kk_scr, kk), (b_scr, bb), (g_scr, g),
                     (wr_scr, wr), (kr_scr, kr)):
        scr[...] = val.reshape(nb, tc, B_DIM)

    sub = lax.broadcasted_iota(jnp.int32, (HEAD, LANES), 0)
    lane = lax.broadcasted_iota(jnp.int32, (HEAD, LANES), 1)
    diag = (lane % HEAD) == sub
    tok_lane = lane % HEAD
    n_all = nb * n_pairs
    grp = min(n_all, RWKV_MATMUL_PAIRS)
    groups = [list(range(g0, g0 + grp)) for g0 in range(0, n_all, grp)]

    def step8(t8, carry):
        t0 = pl.multiple_of(t8 * SUBLANES, SUBLANES)

        def rows_of(scr):
            return [scr[i // n_pairs, pl.ds(t0, SUBLANES), (i % n_pairs) * LANES:(i % n_pairs + 1) * LANES]
                    for i in range(n_all)]

        kk8, v8, d8, b8, k8, wr8, kr8 = [
            rows_of(scr) for scr in (kk_scr, v_scr, d_scr, b_scr, k_scr, wr_scr, kr_scr)]
        yb = [jnp.zeros((HEAD, LANES), f32) for _ in range(n_all)]

        vb = {}
        pairs_mm = range(SUBLANES // 2)
        for grp_ids in groups:
            lhs = jnp.concatenate(
                [jnp.concatenate([jnp.where(diag, v8[i][2 * m:2 * m + 1, :], 0.0).astype(BF16),
                                  jnp.where(diag, v8[i][2 * m + 1:2 * m + 2, :], 0.0).astype(BF16)], axis=1)
                 for m in pairs_mm for i in grp_ids], axis=0)
            res = _dot(lhs, bd2)
            for mi, m in enumerate(pairs_mm):
                for q, i in enumerate(grp_ids):
                    blk = res[(mi * len(grp_ids) + q) * HEAD:(mi * len(grp_ids) + q + 1) * HEAD, :]
                    vb[(i, 2 * m)] = blk[:, 0:LANES]
                    vb[(i, 2 * m + 1)] = blk[:, LANES:]

        for j in range(SUBLANES):
            for grp_ids in groups:
                lhs = jnp.concatenate(
                    [jnp.concatenate([s_scr[i] * kk8[i][j:j + 1, :], s_scr[i] * wr8[i][j:j + 1, :]],
                                     axis=1).astype(BF16) for i in grp_ids], axis=0)
                res = _dot(lhs, bd2)
                for q, i in enumerate(grp_ids):
                    sa = res[q * HEAD:(q + 1) * HEAD, 0:LANES]
                    u = res[q * HEAD:(q + 1) * HEAD, LANES:]
                    yb[i] = jnp.where(tok_lane == j, u, yb[i])
                    s_scr[i] = s_scr[i] * d8[i][j:j + 1, :] - sa * b8[i][j:j + 1, :] + vb[(i, j)] * k8[i][j:j + 1, :]

        for i in range(n_all):
            yt = yb[i].T
            u8 = jnp.concatenate([yt[0:SUBLANES, :], yt[HEAD:HEAD + SUBLANES, :]], axis=1)
            y_scr[i // n_pairs, pl.ds(t0, SUBLANES), (i % n_pairs) * LANES:(i % n_pairs + 1) * LANES] = (
                u8 + v8[i] * kr8[i])
        return carry

    lax.fori_loop(0, tc // SUBLANES, step8, 0, unroll=4 if tc // SUBLANES >= 4 else 1)

    y = merged(y_scr)
    yc = y - head_sum(y) * (1.0 / HEAD)
    var = head_sum(yc * yc) * (1.0 / HEAD)
    yn = yc * lax.rsqrt(var + GN_EPS) * lnx_w + lnx_b
    bonus = head_sum(merged(r_scr) * merged(k_scr) * r_k) * merged(v_scr)
    o_ref[...] = ((yn + bonus) * merged(g_scr)).reshape(nb, tc, B_DIM).astype(o_ref.dtype)

    @pl.when(c == pl.num_programs(1) - 1)
    def _():
        for b in range(nb):
            for p in range(n_pairs):
                s = s_scr[b * n_pairs + p]
                sout_ref[b, 2 * p] = s[:, 0:HEAD]
                sout_ref[b, 2 * p + 1] = s[:, HEAD:]


def _rwkv(p3d, shift0, s0, mu, vecs, w2p, a2p, g2p, nb, tc, out_dtype=F32):
    batch, seq, _ = p3d.shape
    groups = batch // nb
    chunks = seq // tc
    x_blk = (nb, tc, B_DIM)

    def xmap(col):
        return lambda g, c: (g, c, col)

    const2 = lambda g, c: (0, 0)
    scr = lambda *shape: pltpu.VMEM(shape, F32)
    o, s_out = pl.pallas_call(
        functools.partial(_rwkv_kernel, nb=nb, tc=tc),
        name="rwkv7",
        grid=(groups, chunks),
        in_specs=[
            pl.BlockSpec(x_blk, xmap(OFF_PB // B_DIM)),
            pl.BlockSpec(x_blk, xmap(OFF_PB // B_DIM + 1)),
            pl.BlockSpec(x_blk, xmap(OFF_PB // B_DIM + 2)),
            pl.BlockSpec((nb, tc, LORA_COLS), xmap(OFF_LORA // LORA_COLS)),
            pl.BlockSpec((1, nb, B_COLS), lambda g, c: (g, 0, 0)),
            pl.BlockSpec((nb, B_HEADS, HEAD, HEAD), lambda g, c: (g, 0, 0, 0)),
            pl.BlockSpec((1, B_COLS), const2),
            pl.BlockSpec((SUBLANES, B_DIM), const2),
            pl.BlockSpec((LORA_COLS, B_DIM), const2),
            pl.BlockSpec((LORA_COLS, B_DIM), const2),
            pl.BlockSpec((LORA_COLS, B_DIM), const2),
        ],
        out_specs=[
            pl.BlockSpec(x_blk, lambda g, c: (g, c, 0)),
            pl.BlockSpec((nb, B_HEADS, HEAD, HEAD), lambda g, c: (g, 0, 0, 0)),
        ],
        out_shape=[jax.ShapeDtypeStruct((batch, seq, B_DIM), out_dtype),
                   jax.ShapeDtypeStruct((batch, B_HEADS, HEAD, HEAD), F32)],
        scratch_shapes=[scr(nb * B_HEADS // 2, HEAD, LANES)] + [scr(nb, tc, B_DIM)] * 10
        + [scr(nb, SUBLANES, 3 * B_DIM), scr(nb, SUBLANES, LORA_COLS)],
        compiler_params=_cparams(2),
    )(p3d, p3d, p3d, p3d, shift0.reshape(groups, nb, B_COLS), s0, mu, vecs, w2p, a2p, g2p)
    return o, s_out


def _rope_tables(pos):
    half = ROPE_DIM // 2
    inv = jnp.exp(-math.log(ROPE_THETA) * jnp.arange(half, dtype=F32) * 2.0 / ROPE_DIM)
    ang = pos.astype(F32)[:, None] * inv[None, :]
    cos, sin = jnp.cos(ang), jnp.sin(ang)
    lm = np.arange(LANES) % HEAD
    first = jnp.asarray(lm < half)[None, :]
    second = jnp.asarray((lm >= half) & (lm < ROPE_DIM))[None, :]
    freq = np.where(lm < half, lm, np.where(lm < ROPE_DIM, lm - half, 0))
    cos_l, sin_l = cos[:, freq], sin[:, freq]
    c = jnp.where(first | second, cos_l, 1.0)
    s1 = jnp.where(first, -sin_l, 0.0)
    s2 = jnp.where(second, sin_l, 0.0)
    return c, s1, s2


def _rope_flag():
    col = np.arange(IN_COLS)
    rope = (col < OFF_VA) | ((col >= OFF_QC) & (col < OFF_VC))
    tiles = tuple(int(t) for t in np.nonzero(rope.reshape(-1, IN_TN).any(axis=1))[0])
    return jnp.asarray(rope.astype(np.float32))[None, :], tiles


def _pad_rows(w, start):
    return jnp.zeros((LORA_COLS, B_DIM), F32).at[start:start + w.shape[0]].set(w)


def _window_stack_kernel(*refs, depth):
    ins, (ko_ref, vo_ref) = refs[:2 * depth], refs[2 * depth:]
    for l in range(depth):
        @pl.when(pl.program_id(0) == l)
        def _():
            ko_ref[...] = ins[2 * l][...].T
            vo_ref[...] = ins[2 * l + 1][...].T


def _window_stack(p_layers, batch, seq, keep):
    depth = len(p_layers)
    tb = min(keep, WINDOW_STACK_ROWS)
    nblk, first = seq // tb, (seq - keep) // tb

    def in_spec(l, off):
        return pl.BlockSpec((tb, A_DIM), lambda g, b, t: (jnp.where(g == l, b * nblk + first + t, 0), off // A_DIM))

    out_spec = pl.BlockSpec((None, None, A_DIM, tb), lambda g, b, t: (g, b, 0, t))
    shape = jax.ShapeDtypeStruct((depth, batch, A_DIM, keep), F32)
    k, v = pl.pallas_call(
        functools.partial(_window_stack_kernel, depth=depth),
        name="window_stack",
        grid=(depth, batch, keep // tb),
        in_specs=[in_spec(l, off) for l in range(depth) for off in (OFF_KA, OFF_VA)],
        out_specs=[out_spec, out_spec],
        out_shape=[shape, shape],
        compiler_params=_cparams(3),
    )(*[p for p in p_layers for _ in range(2)])

    def as_output(x):
        return jnp.transpose(x.reshape(depth, batch, A_HEADS, HEAD, keep), (0, 1, 4, 2, 3))

    return as_output(k), as_output(v)


def _mixers(p2d, batch, seq, layer, is_prompt, cache, rwkv_w, sink):
    p3d = p2d.reshape(batch, seq, IN_COLS)
    mu, vecs, w2p, a2p, g2p = rwkv_w
    if is_prompt:
        oa = _attn_a_prompt(p2d, batch, seq)
        oc = _attn_c_prompt(p2d, sink, batch, seq)
        shift0 = jnp.zeros((batch, B_COLS), F32)
        s0 = jnp.zeros((batch, B_HEADS, HEAD, HEAD), F32)
        ob, wkv = _rwkv(p3d, shift0, s0, mu, vecs, w2p, a2p, g2p, nb=batch, tc=min(seq, 128), out_dtype=BF16)
    else:
        a_k, a_v, c_k, c_v, wkv0, shift0 = cache
        oa = _attn_a_sample(p2d, a_k, a_v, layer, batch, seq)
        sink_col = jnp.repeat(sink, seq)[:, None]
        oc = _attn_c_sample(p2d, c_k, c_v, layer, sink_col, batch, seq)
        ob, wkv = _rwkv(p3d, shift0[layer], wkv0[layer], mu, vecs, w2p, a2p, g2p, nb=SAMPLE_RWKV_BATCH, tc=seq)
    return oa, ob.reshape(batch * seq, B_DIM), oc, wkv


def kernel(x_prompt, x_sample, cache_a_k, cache_a_v, cache_c_k, cache_c_v, state_b_wkv, state_b_shift, g_mix, w_in, w_out, b_mu, b_w0, b_w2, b_a0, b_a2, b_g2, b_k_k, b_k_a, b_r_k, b_lnx_w, b_lnx_b, c_sink, g_ffn, w_gate, w_up, w_down, g_final):
    depth = w_in.shape[0]
    bp, lp, d = x_prompt.shape
    bs, ls, _ = x_sample.shape
    a_win = cache_a_k.shape[2]
    c_win = cache_c_k.shape[2]
    assert (bp * lp) % PROJ_TM == 0 and lp % PROJ_TM == 0 and lp % (16 * BLOCK) == 0 and a_win >= A_BRANCHES[-1][0] and c_win == C_WINDOW and bs % SAMPLE_RWKV_BATCH == 0

    flag, rope_tiles = _rope_flag()
    tabs_p = _rope_tables(jnp.arange(lp, dtype=jnp.int32))
    tabs_s = _rope_tables(jnp.tile(PAST_LEN + jnp.arange(ls, dtype=jnp.int32), bs))
    cak = jnp.transpose(cache_a_k, (0, 1, 3, 4, 2))
    cav = jnp.transpose(cache_a_v, (0, 1, 3, 4, 2))
    cck = jnp.transpose(cache_c_k, (0, 1, 3, 4, 2)).reshape(depth, bs, C_KV_DIM, c_win)
    ccv = jnp.transpose(cache_c_v, (0, 1, 3, 4, 2)).reshape(depth, bs, C_KV_DIM, c_win)
    cache = (cak, cav, cck, ccv, state_b_wkv, state_b_shift)

    xp = x_prompt.reshape(bp * lp, d)
    xs = x_sample.reshape(bs * ls, d)
    new_p = [[] for _ in range(6)]
    new_s = [[] for _ in range(6)]
    p_prompt = []
    for l in range(depth):
        vecs = jnp.stack([b_w0[l], b_a0[l], b_k_k[l], b_k_a[l], b_r_k[l], b_lnx_w[l], b_lnx_b[l],
                          jnp.zeros((B_DIM,), F32)], axis=0)
        a2_row = b_w2.shape[1]
        g2_row = a2_row + b_a2.shape[1]
        rwkv_w = (b_mu[l][None, :], vecs, _pad_rows(b_w2[l], 0), _pad_rows(b_a2[l], a2_row), _pad_rows(b_g2[l], g2_row))
        sink = c_sink[l].reshape(C_Q_HEADS)
        hp, hs = _rmsnorm(xp, g_mix[l], BF16), _rmsnorm(xs, g_mix[l], BF16)
        pp, ps = _inproj(hp, hs, w_in, l, flag, rope_tiles, tabs_p, tabs_s, PROJ_TM)
        oa_p, ob_p, oc_p, wkv_p = _mixers(pp, bp, lp, l, True, cache, rwkv_w, sink)
        oa_s, ob_s, oc_s, wkv_s = _mixers(ps, bs, ls, l, False, cache, rwkv_w, sink)
        xp, xs = _outproj((xp, xs), (oa_p, oa_s), (ob_p, ob_s), (oc_p, oc_s), w_out, l)
        hp, hs = _rmsnorm(xp, g_ffn[l], BF16), _rmsnorm(xs, g_ffn[l], BF16)
        act_p, act_s = _ffn_up(hp, hs, w_gate, w_up, l)
        xp, xs = _ffn_down(xp, xs, act_p, act_s, w_down, l)
        for p2d, batch, seq, wkv, new, is_prompt in ((pp, bp, lp, wkv_p, new_p, True), (ps, bs, ls, wkv_s, new_s, False)):
            p3d = p2d.reshape(batch, seq, IN_COLS)
            a_keep = min(a_win, seq) if is_prompt else seq
            c_keep = min(c_win, seq) if is_prompt else seq
            state = (
                p3d[:, seq - a_keep:, OFF_KA:OFF_VA].reshape(batch, a_keep, A_HEADS, HEAD),
                p3d[:, seq - a_keep:, OFF_VA:OFF_PB].reshape(batch, a_keep, A_HEADS, HEAD),
                p3d[:, seq - c_keep:, OFF_KC:OFF_VC].reshape(batch, c_keep, C_KV_DIM // HEAD, HEAD),
                p3d[:, seq - c_keep:, OFF_VC:].reshape(batch, c_keep, C_KV_DIM // HEAD, HEAD),
                wkv,
                p3d[:, -1, OFF_PB:OFF_QC],
            )
            for i in range(2 if is_prompt else 0, 6):
                new[i].append(state[i])
        p_prompt.append(pp)
    y_prompt = _rmsnorm(xp, g_final, F32).reshape(bp, lp, d)
    y_sample = _rmsnorm(xs, g_final, F32).reshape(bs, ls, d)
    outs_p = list(_window_stack(p_prompt, bp, lp, min(a_win, lp))) + [jnp.stack(t, axis=0) for t in new_p[2:]]
    outs_s = [jnp.stack(t, axis=0) for t in new_s]
    return (y_prompt, y_sample, *outs_p, *outs_s)
```
